```python
import jax
import jax.numpy as jnp
from jax import lax
import numpy as np

D_MODEL = 1024
BATCH = 4
SEQ = 4096
DEPTH = 4
DEC_BATCH = 32
DEC_SEQ = 1
PAST_LEN = 8192
PAGE_SIZE = 128

N_EVEN = (DEPTH + 1) // 2
N_ODD = DEPTH // 2
FOX_HEADS = 8
FOX_HEAD_DIM = 64
FOX_DIM = FOX_HEADS * FOX_HEAD_DIM
Q_BLOCK = 128
CONV_CH = D_MODEL // 2
CONV_GROUPS = 8
CONV_WIDTH = 31
RET_HEADS = 4
RET_KEY_DIM = D_MODEL // RET_HEADS
RET_VAL_DIM = 2 * D_MODEL // RET_HEADS
RET_CHUNK = 128
FFN_HIDDEN = -(-8 * D_MODEL // (3 * 256)) * 256
EVEN_IN = 3 * FOX_DIM + FOX_HEADS + 2 * CONV_CH
ODD_IN = 2 * RET_HEADS * RET_KEY_DIM + 2 * RET_HEADS * RET_VAL_DIM
EPS = 1e-6
ROPE_BASE = 10000.0

kernel_name = 'fox_conformer_retnet_adaln_step'


def _rmsnorm(x, g):
    xf = x.astype(jnp.float32)
    y = xf * lax.rsqrt(jnp.mean(xf * xf, axis=-1, keepdims=True) + EPS)
    return (y * g.astype(jnp.float32)).astype(x.dtype)


def _group_norm(x, groups):
    shp = x.shape
    xf = x.astype(jnp.float32).reshape(shp[:-1] + (groups, shp[-1] // groups))
    mu = jnp.mean(xf, axis=-1, keepdims=True)
    var = jnp.mean(jnp.square(xf - mu), axis=-1, keepdims=True)
    return ((xf - mu) * lax.rsqrt(var + EPS)).reshape(shp)


def _rope(x, pos):
    half = x.shape[-1] // 2
    inv = 1.0 / (ROPE_BASE ** jnp.linspace(0.0, 1.0, half, dtype=jnp.float32))
    ang = pos.astype(jnp.float32)[:, None] * inv[None, :]
    cos = jnp.cos(ang)[None, :, None, :]
    sin = jnp.sin(ang)[None, :, None, :]
    x1, x2 = x[..., :half], x[..., half:]
    return jnp.concatenate([x1 * cos - x2 * sin, x1 * sin + x2 * cos], axis=-1)


def _fox_prompt(q, k, v, logf):
    B, L, H, hd = q.shape
    cum = jnp.cumsum(logf, axis=1)
    nb = L // Q_BLOCK
    qb = q.reshape(B, nb, Q_BLOCK, H, hd).transpose(1, 0, 2, 3, 4)
    cb = cum.reshape(B, nb, Q_BLOCK, H).transpose(1, 0, 2, 3)
    qpos = jnp.arange(L).reshape(nb, Q_BLOCK)
    kpos = jnp.arange(L)
    cum_k = cum.transpose(0, 2, 1)[:, :, None, :]
    scale = FOX_HEAD_DIM ** -0.5

    def block(args):
        qi, ci, pi = args
        s = jnp.einsum('bqhd,bkhd->bhqk', qi, k, preferred_element_type=jnp.float32) * scale
        s = s + ci.transpose(0, 2, 1)[..., None] - cum_k
        s = jnp.where(kpos[None, :] <= pi[:, None], s, -jnp.inf)
        p = jax.nn.softmax(s, axis=-1)
        return jnp.einsum('bhqk,bkhd->bqhd', p.astype(v.dtype), v)

    o = lax.map(block, (qb, cb, qpos))
    return o.transpose(1, 0, 2, 3, 4).reshape(B, L, H * hd)


def _fox_sample(q, k, v, logf, cache_k, cache_v, cache_logf, page_table, e):
    Bd, T, H, hd = q.shape
    kp = cache_k[e, page_table].reshape(Bd, -1, H, hd)
    vp = cache_v[e, page_table].reshape(Bd, -1, H, hd)
    lp = cache_logf[e, page_table].reshape(Bd, -1, H).astype(jnp.float32)
    P = kp.shape[1]
    cum_p = jnp.cumsum(lp, axis=1)
    cum_n = cum_p[:, -1:, :] + jnp.cumsum(logf, axis=1)
    cq = cum_n.transpose(0, 2, 1)
    scale = FOX_HEAD_DIM ** -0.5
    s_p = jnp.einsum('bqhd,bkhd->bhqk', q, kp, preferred_element_type=jnp.float32) * scale
    s_p = s_p + cq[..., None] - cum_p.transpose(0, 2, 1)[:, :, None, :]
    s_n = jnp.einsum('bqhd,bkhd->bhqk', q, k, preferred_element_type=jnp.float32) * scale
    s_n = s_n + cq[..., None] - cq[:, :, None, :]
    tri = jnp.arange(T)[None, :] <= jnp.arange(T)[:, None]
    s_n = jnp.where(tri, s_n, -jnp.inf)
    p = jax.nn.softmax(jnp.concatenate([s_p, s_n], axis=-1), axis=-1)
    o = (jnp.einsum('bhqk,bkhd->bqhd', p[..., :P].astype(v.dtype), vp)
         + jnp.einsum('bhqk,bkhd->bqhd', p[..., P:].astype(v.dtype), v))
    return o.reshape(Bd, T, H * hd)


def _retention(q, k, v, s0, chunk):
    B, L, H, dk = q.shape
    dv = v.shape[-1]
    n = L // chunk
    lg = jnp.log(1.0 - 2.0 ** (-5.0 - jnp.arange(H, dtype=jnp.float32)))
    i = jnp.arange(chunk, dtype=jnp.float32)
    diff = i[:, None] - i[None, :]
    dmask = jnp.where(diff >= 0, jnp.exp(jnp.where(diff >= 0, diff, 0.0)[None] * lg[:, None, None]), 0.0)
    q_dec = jnp.exp((i[None, :] + 1.0) * lg[:, None]).T
    k_dec = jnp.exp((chunk - 1.0 - i)[None, :] * lg[:, None]).T
    c_dec = jnp.exp(chunk * lg)

    def split(t):
        return t.reshape(B, n, chunk, H, t.shape[-1]).transpose(1, 0, 2, 3, 4)

    def step(S, xs):
        qc, kc, vc = xs
        a = jnp.einsum('bihd,bjhd->bhij', qc, kc) * dmask[None]
        o = (jnp.einsum('bhij,bjhe->bihe', a, vc)
             + jnp.einsum('bihd,bhde->bihe', qc * q_dec[None, :, :, None], S))
        S = S * c_dec[None, :, None, None] + jnp.einsum('bjhd,bjhe->bhde', kc * k_dec[None, :, :, None], vc)
        return S, o

    S, o = lax.scan(step, s0, (split(q), split(k), split(v)))
    return o.transpose(1, 0, 2, 3, 4).reshape(B, L, H, dv), S


def _even_mixer(h, e, fox_attend, conv_buf, W):
    B, L, _ = h.shape
    z = h @ W['w_in_even'][e]
    cuts = [FOX_DIM, 2 * FOX_DIM, 3 * FOX_DIM, 3 * FOX_DIM + FOX_HEADS, 3 * FOX_DIM + FOX_HEADS + CONV_CH]
    q, k, v, fl, a, b = jnp.split(z, cuts, axis=-1)
    q = q.reshape(B, L, FOX_HEADS, FOX_HEAD_DIM)
    k = k.reshape(B, L, FOX_HEADS, FOX_HEAD_DIM)
    v = v.reshape(B, L, FOX_HEADS, FOX_HEAD_DIM)
    logf = jax.nn.log_sigmoid(fl.astype(jnp.float32) + W['b_forget'][e].astype(jnp.float32))
    o_fox = fox_attend(e, q, k, v, logf)
    u = a * jax.nn.sigmoid(b)
    xp = jnp.concatenate([conv_buf.astype(u.dtype), u], axis=1)
    new_buf = xp[:, -(CONV_WIDTH - 1):]
    cv = lax.conv_general_dilated(
        xp, W['conv_w'][e][:, None, :].astype(u.dtype), window_strides=(1,), padding='VALID',
        dimension_numbers=('NWC', 'WIO', 'NWC'), feature_group_count=CONV_CH) + W['conv_b'][e]
    cv = _group_norm(cv, CONV_GROUPS) * W['conv_norm_g'][e].astype(jnp.float32) + W['conv_norm_b'][e].astype(jnp.float32)
    cv = jax.nn.silu(cv).astype(h.dtype)
    out = jnp.concatenate([o_fox.astype(h.dtype), cv], axis=-1) @ W['w_out_even'][e]
    return out, (k, v, logf, new_buf)


def _odd_mixer(h, o, pos, s0, chunk, W):
    B, L, _ = h.shape
    nk = RET_HEADS * RET_KEY_DIM
    nv = RET_HEADS * RET_VAL_DIM
    z = h @ W['w_in_odd'][o]
    q, k, v, g = jnp.split(z, [nk, 2 * nk, 2 * nk + nv], axis=-1)
    q = _rope(q.reshape(B, L, RET_HEADS, RET_KEY_DIM).astype(jnp.float32), pos)
    k = _rope(k.reshape(B, L, RET_HEADS, RET_KEY_DIM).astype(jnp.float32), pos) * (RET_KEY_DIM ** -0.5)
    v = v.reshape(B, L, RET_HEADS, RET_VAL_DIM).astype(jnp.float32)
    r, S = _retention(q, k, v, s0.astype(jnp.float32), chunk)
    r = _group_norm(r.reshape(B, L, nv), RET_HEADS) * W['ret_norm_g'][o].astype(jnp.float32)
    out = (jax.nn.silu(g) * r.astype(h.dtype)) @ W['w_out_odd'][o]
    return out, S


def _trunk(x, c, pos, fox_attend, conv_init, ret_init, ret_chunk, W):
    ks, vs, lfs, cbs, rs = [], [], [], [], []
    cmod = jax.nn.silu(c)
    for l in range(DEPTH):
        mod = cmod @ W['ada_w'][l] + W['ada_b'][l]
        sh1, sc1, g1, sh2, sc2, g2 = [m[:, None, :] for m in jnp.split(mod, 6, axis=-1)]
        hm = _rmsnorm(x, W['norm_mix_g'][l]) * (1.0 + sc1) + sh1
        if l % 2 == 0:
            e = l // 2
            out, (k, v, lf, cb) = _even_mixer(hm, e, fox_attend, conv_init(e), W)
            ks.append(k); vs.append(v); lfs.append(lf); cbs.append(cb)
        else:
            o = l // 2
            out, S = _odd_mixer(hm, o, pos, ret_init(o), ret_chunk, W)
            rs.append(S)
        x = x + g1 * out
        hf = _rmsnorm(x, W['norm_ffn_g'][l]) * (1.0 + sc2) + sh2
        ff = (jax.nn.silu(hf @ W['w_ffn_gate'][l]) * (hf @ W['w_ffn_up'][l])) @ W['w_ffn_down'][l]
        x = x + g2 * ff
    y = _rmsnorm(x, W['final_norm_g'])
    return y, (jnp.stack(ks), jnp.stack(vs), jnp.stack(lfs), jnp.stack(cbs), jnp.stack(rs))


def setup_inputs(seed: int = 0) -> dict:
    key = jax.random.key(seed)
    ks = jax.random.split(key, 32)
    f32 = jnp.float32
    D = D_MODEL
    n_pages = PAST_LEN // PAGE_SIZE
    n_used = DEC_BATCH * n_pages
    n_phys = n_used + n_used // 4

    def nrm(k, shape, s=1.0):
        return s * jax.random.normal(k, shape, f32)

    nv = RET_HEADS * RET_VAL_DIM
    return {
        'x_prompt': nrm(ks[0], (BATCH, SEQ, D)),
        'x_sample': nrm(ks[1], (DEC_BATCH, DEC_SEQ, D)),
        'cache_k': nrm(ks[2], (N_EVEN, n_phys, PAGE_SIZE, FOX_HEADS, FOX_HEAD_DIM)),
        'cache_v': nrm(ks[3], (N_EVEN, n_phys, PAGE_SIZE, FOX_HEADS, FOX_HEAD_DIM)),
        'cache_logf': jax.nn.log_sigmoid(nrm(ks[4], (N_EVEN, n_phys, PAGE_SIZE, FOX_HEADS)) + 3.0),
        'state_conv': nrm(ks[5], (N_EVEN, DEC_BATCH, CONV_WIDTH - 1, CONV_CH), 0.5),
        'state_ret': nrm(ks[6], (N_ODD, DEC_BATCH, RET_HEADS, RET_KEY_DIM, RET_VAL_DIM), 0.1),
        'page_table': jax.random.permutation(ks[7], n_phys)[:n_used].reshape(DEC_BATCH, n_pages).astype(jnp.int32),
        'c_prompt': nrm(ks[8], (BATCH, D)),
        'c_sample': nrm(ks[9], (DEC_BATCH, D)),
        'ada_w': nrm(ks[10], (DEPTH, D, 6 * D), 0.5 * D ** -0.5),
        'ada_b': nrm(ks[11], (DEPTH, 6 * D), 0.02),
        'norm_mix_g': 1.0 + nrm(ks[12], (DEPTH, D), 0.05),
        'norm_ffn_g': 1.0 + nrm(ks[13], (DEPTH, D), 0.05),
        'w_in_even': nrm(ks[14], (N_EVEN, D, EVEN_IN), D ** -0.5),
        'b_forget': jax.random.uniform(ks[15], (N_EVEN, FOX_HEADS), f32, 1.0, 4.0),
        'conv_w': nrm(ks[16], (N_EVEN, CONV_WIDTH, CONV_CH), CONV_WIDTH ** -0.5),
        'conv_b': nrm(ks[17], (N_EVEN, CONV_CH), 0.02),
        'conv_norm_g': 1.0 + nrm(ks[18], (N_EVEN, CONV_CH), 0.05),
        'conv_norm_b': nrm(ks[19], (N_EVEN, CONV_CH), 0.02),
        'w_out_even': nrm(ks[20], (N_EVEN, FOX_DIM + CONV_CH, D), (FOX_DIM + CONV_CH) ** -0.5),
        'w_in_odd': nrm(ks[21], (N_ODD, D, ODD_IN), D ** -0.5),
        'ret_norm_g': 1.0 + nrm(ks[22], (N_ODD, nv), 0.05),
        'w_out_odd': nrm(ks[23], (N_ODD, nv, D), nv ** -0.5),
        'w_ffn_gate': nrm(ks[24], (DEPTH, D, FFN_HIDDEN), D ** -0.5),
        'w_ffn_up': nrm(ks[25], (DEPTH, D, FFN_HIDDEN), D ** -0.5),
        'w_ffn_down': nrm(ks[26], (DEPTH, FFN_HIDDEN, D), FFN_HIDDEN ** -0.5),
        'final_norm_g': 1.0 + nrm(ks[27], (D,), 0.05),
    }


def reference(x_prompt, x_sample, cache_k, cache_v, cache_logf, state_conv, state_ret, page_table,
              c_prompt, c_sample, ada_w, ada_b, norm_mix_g, norm_ffn_g, w_in_even, b_forget,
              conv_w, conv_b, conv_norm_g, conv_norm_b, w_out_even, w_in_odd, ret_norm_g, w_out_odd,
              w_ffn_gate, w_ffn_up, w_ffn_down, final_norm_g):
    W = dict(ada_w=ada_w, ada_b=ada_b, norm_mix_g=norm_mix_g, norm_ffn_g=norm_ffn_g,
             w_in_even=w_in_even, b_forget=b_forget, conv_w=conv_w, conv_b=conv_b,
             conv_norm_g=conv_norm_g, conv_norm_b=conv_norm_b, w_out_even=w_out_even,
             w_in_odd=w_in_odd, ret_norm_g=ret_norm_g, w_out_odd=w_out_odd,
             w_ffn_gate=w_ffn_gate, w_ffn_up=w_ffn_up, w_ffn_down=w_ffn_down,
             final_norm_g=final_norm_g)
    bp, lp, _ = x_prompt.shape
    bs, ls, _ = x_sample.shape
    past_len = page_table.shape[1] * PAGE_SIZE

    y_prompt, (k_p, v_p, lf_p, conv_p, ret_p) = _trunk(
        x_prompt, c_prompt, jnp.arange(lp),
        lambda e, q, k, v, lf: _fox_prompt(q, k, v, lf),
        lambda e: jnp.zeros((bp, CONV_WIDTH - 1, CONV_CH), x_prompt.dtype),
        lambda o: jnp.zeros((bp, RET_HEADS, RET_KEY_DIM, RET_VAL_DIM), jnp.float32),
        min(RET_CHUNK, lp), W)

    y_sample, (k_s, v_s, lf_s, conv_s, ret_s) = _trunk(
        x_sample, c_sample, past_len + jnp.arange(ls),
        lambda e, q, k, v, lf: _fox_sample(q, k, v, lf, cache_k, cache_v, cache_logf, page_table, e),
        lambda e: state_conv[e],
        lambda o: state_ret[o],
        ls, W)
    ret_s = ret_s.astype(state_ret.dtype)

    return (y_prompt, y_sample, k_p, v_p, lf_p, conv_p, ret_p, k_s, v_s, lf_s, conv_s, ret_s)
```

```python
import functools
import math

import jax
import jax.numpy as jnp
import numpy as np
from jax import lax
from jax.experimental import pallas as pl
from jax.experimental.pallas import tpu as pltpu

F32 = jnp.float32
BF16 = jnp.bfloat16

EPS = 1e-6
ROPE_BASE = 10000.0
FOX_HEADS = 8
FOX_HEAD_DIM = 64
FOX_DIM = FOX_HEADS * FOX_HEAD_DIM
CONV_GROUPS = 8
CONV_WIDTH = 31
RET_HEADS = 4
PAGE_SIZE = 128
LANES = 128
CONV_HALO = 32
VMEM_LIMIT = 56 * 1024 * 1024
NEG_BIG = -1e30
FOX_TQ = 256


def _cparams(sem):
    return pltpu.CompilerParams(dimension_semantics=sem, vmem_limit_bytes=VMEM_LIMIT)


def _silu(x):
    return x * jax.nn.sigmoid(x)


def _log_sigmoid(x):
    return jnp.minimum(x, 0.0) - jnp.log1p(jnp.exp(-jnp.abs(x)))


def _norm_mod(x, g, sc, sh):
    ms = jnp.mean(x * x, axis=-1, keepdims=True)
    y = x * lax.rsqrt(ms + EPS) * g
    return y * (1.0 + sc) + sh


def _split3(x):
    hi = x.astype(BF16)
    r1 = x - hi.astype(F32)
    mid = r1.astype(BF16)
    lo = (r1 - mid.astype(F32)).astype(BF16)
    return hi, mid, lo


def _dot3(x, w_bf16):
    hi, mid, lo = _split3(x)
    d = functools.partial(jnp.dot, preferred_element_type=F32)
    return d(hi, w_bf16) + d(mid, w_bf16) + d(lo, w_bf16)


def _dot_nt(a, b):
    return lax.dot_general(a, b, (((1,), (1,)), ((), ())), preferred_element_type=F32)


def _mod_spec(rows, d, tiles_per_mod):
    return pl.BlockSpec((None, rows, d), lambda i, *_: (i // tiles_per_mod, 0, 0))


def _adaln_kernel(c_ref, w_ref, b_ref, o_ref):
    cm = _silu(c_ref[...]).astype(BF16)
    o_ref[...] = jnp.dot(cm, w_ref[...].astype(BF16), preferred_element_type=F32) + b_ref[...]


def _adaln(c_all, ada_w, ada_b, tn=1536):
    depth, d, n = ada_w.shape
    r = c_all.shape[0]
    return pl.pallas_call(
        _adaln_kernel,
        grid=(depth, n // tn),
        in_specs=[pl.BlockSpec((r, d), lambda l, j: (0, 0)),
                  pl.BlockSpec((None, d, tn), lambda l, j: (l, 0, j)),
                  pl.BlockSpec((None, 1, tn), lambda l, j: (l, 0, j))],
        out_specs=pl.BlockSpec((None, r, tn), lambda l, j: (l, 0, j)),
        out_shape=jax.ShapeDtypeStruct((depth, r, n), F32),
        compiler_params=_cparams(("arbitrary", "arbitrary")),
        name="adaln",
    )(c_all, ada_w, ada_b.reshape(depth, 1, n))


def _even_proj_kernel(x_ref, g_ref, sc_ref, sh_ref, w_ref, bf_ref, *refs, tm, tq, tiles_per_seq, with_cum):
    if with_cum:
        q_ref, k_ref, v_ref, kb_ref, vb_ref, lf_ref, u_ref, cum_ref, cumt_ref, carry_ref = refs
    else:
        q_ref, k_ref, v_ref, kb_ref, vb_ref, lf_ref, u_ref = refs
    hm = _norm_mod(x_ref[...], g_ref[...], sc_ref[...], sh_ref[...]).astype(BF16)
    z = jnp.dot(hm, w_ref[...], preferred_element_type=F32)
    c = FOX_DIM
    q_ref[...] = z[:, 0:c].astype(BF16)
    k = z[:, c:2 * c]
    v = z[:, 2 * c:3 * c]
    k_ref[...] = k
    v_ref[...] = v
    kb_ref[...] = k.astype(BF16)
    vb_ref[...] = v.astype(BF16)
    a = z[:, 3 * c:4 * c]
    b = z[:, 4 * c:5 * c]
    u_ref[...] = a * jax.nn.sigmoid(b)
    logf = _log_sigmoid(z[:, 5 * c:5 * c + LANES] + bf_ref[...])
    lf_ref[...] = logf[:, 0:FOX_HEADS]
    if with_cum:
        i = pl.program_id(0)

        @pl.when(i % tiles_per_seq == 0)
        def _():
            carry_ref[...] = jnp.zeros_like(carry_ref)

        row = lax.broadcasted_iota(jnp.int32, (tm, tm), 0)
        col = lax.broadcasted_iota(jnp.int32, (tm, tm), 1)
        tri = (col <= row).astype(BF16)
        cum = _dot3_left(tri, logf) + carry_ref[...]
        carry_ref[...] = cum[tm - 1:tm, :]
        cum_ref[...] = cum[:, 0:FOX_HEADS]
        cum_t = cum.T
        for r in range(tm // tq):
            cumt_ref[r] = cum_t[0:FOX_HEADS, r * tq:(r + 1) * tq]


def _dot3_left(w_bf16, x):
    hi, mid, lo = _split3(x)
    d = functools.partial(jnp.dot, preferred_element_type=F32)
    return d(w_bf16, hi) + d(w_bf16, mid) + d(w_bf16, lo)


def _even_proj(x, g, sc, sh, w, bf_pad, *, tm, tq, seq_len, with_cum):
    m, d = x.shape
    nmod, rows, _ = sc.shape
    n = w.shape[1]
    tiles_per_mod = (m // nmod) // tm
    c = FOX_DIM
    row_spec = lambda width: pl.BlockSpec((tm, width), lambda i: (i, 0))
    out_shape = [jax.ShapeDtypeStruct((m, c), BF16), jax.ShapeDtypeStruct((m, c), F32),
                 jax.ShapeDtypeStruct((m, c), F32), jax.ShapeDtypeStruct((m, c), BF16),
                 jax.ShapeDtypeStruct((m, c), BF16), jax.ShapeDtypeStruct((m, FOX_HEADS), F32),
                 jax.ShapeDtypeStruct((m, c), F32)]
    out_specs = [row_spec(c)] * 5 + [row_spec(FOX_HEADS), row_spec(c)]
    scratch = []
    if with_cum:
        out_shape += [jax.ShapeDtypeStruct((m, FOX_HEADS), F32), jax.ShapeDtypeStruct((m // tq, FOX_HEADS, tq), F32)]
        out_specs += [row_spec(FOX_HEADS), pl.BlockSpec((tm // tq, FOX_HEADS, tq), lambda i: (i, 0, 0))]
        scratch = [pltpu.VMEM((1, LANES), F32)]
    return pl.pallas_call(
        functools.partial(_even_proj_kernel, tm=tm, tq=tq, tiles_per_seq=max(seq_len // tm, 1), with_cum=with_cum),
        grid=(m // tm,),
        in_specs=[row_spec(d), pl.BlockSpec((1, d), lambda i: (0, 0)),
                  _mod_spec(rows, d, tiles_per_mod), _mod_spec(rows, d, tiles_per_mod),
                  pl.BlockSpec((d, n), lambda i: (0, 0)), pl.BlockSpec((1, LANES), lambda i: (0, 0))],
        out_specs=out_specs,
        out_shape=out_shape,
        scratch_shapes=scratch,
        compiler_params=_cparams(("arbitrary",)),
        name="even_proj",
    )(x, g, sc, sh, w, bf_pad)


def _fox_prompt_kernel(q_ref, k_ref, v_ref, cq_ref, ck_ref, o_ref, *, tq):
    i = pl.program_id(1)
    lane = lax.broadcasted_iota(jnp.int32, (1, LANES), 1)
    low = lane < FOX_HEAD_DIM
    row = lax.broadcasted_iota(jnp.int32, (tq, tq), 0)
    col = lax.broadcasted_iota(jnp.int32, (tq, tq), 1)
    causal = col <= row
    for pair in range(FOX_HEADS // 2):
        lanes = slice(pair * LANES, (pair + 1) * LANES)
        qp = q_ref[:, lanes].astype(F32)
        outs = []
        for sub in range(2):
            h = 2 * pair + sub
            sel = low if sub == 0 else jnp.logical_not(low)
            qm = jnp.where(sel, qp, 0.0).astype(BF16)
            cq = cq_ref[:, h:h + 1]

            def chunk(j, carry, masked):
                m_prev, l_prev, acc = carry
                start = pl.multiple_of(j * tq, tq)
                kj = k_ref[pl.ds(start, tq), lanes]
                vj = v_ref[pl.ds(start, tq), lanes]
                s = _dot_nt(qm, kj) + (cq - ck_ref[j][h:h + 1, :])
                if masked:
                    s = jnp.where(causal, s, NEG_BIG)
                m_new = jnp.maximum(m_prev, jnp.max(s, axis=-1, keepdims=True))
                alpha = jnp.exp(m_prev - m_new)
                p = jnp.exp(s - m_new)
                l_new = alpha * l_prev + jnp.sum(p, axis=-1, keepdims=True)
                acc = alpha * acc + jnp.dot(p.astype(BF16), vj, preferred_element_type=F32)
                return m_new, l_new, acc

            init = (jnp.full((tq, 1), NEG_BIG, F32), jnp.zeros((tq, 1), F32), jnp.zeros((tq, LANES), F32))
            carry = lax.fori_loop(0, i, functools.partial(chunk, masked=False), init)
            _, l_fin, acc = chunk(i, carry, True)
            outs.append(acc / l_fin)
        o_ref[:, lanes] = jnp.where(low, outs[0], outs[1]).astype(o_ref.dtype)


def _fox_prompt(q, kb, vb, cum, cumt, *, batch, seq_len, tq):
    m, c = q.shape
    nq = seq_len // tq
    return pl.pallas_call(
        functools.partial(_fox_prompt_kernel, tq=tq),
        grid=(batch, nq),
        in_specs=[pl.BlockSpec((tq, c), lambda b, i: (b * nq + i, 0)),
                  pl.BlockSpec((seq_len, c), lambda b, i: (b, 0)),
                  pl.BlockSpec((seq_len, c), lambda b, i: (b, 0)),
                  pl.BlockSpec((tq, FOX_HEADS), lambda b, i: (b * nq + i, 0)),
                  pl.BlockSpec((nq, FOX_HEADS, tq), lambda b, i: (b, 0, 0))],
        out_specs=pl.BlockSpec((tq, c), lambda b, i: (b * nq + i, 0)),
        out_shape=jax.ShapeDtypeStruct((m, c), BF16),
        compiler_params=_cparams(("arbitrary", "arbitrary")),
        name="fox_prompt",
    )(q, kb, vb, cum, cumt)


def _group_matrices(channels, groups):
    gsz = channels // groups
    ch = lax.broadcasted_iota(jnp.int32, (channels, LANES), 0)
    gr = lax.broadcasted_iota(jnp.int32, (channels, LANES), 1)
    gather = (ch // gsz == gr).astype(BF16)
    gr_t = lax.broadcasted_iota(jnp.int32, (LANES, channels), 0)
    ch_t = lax.broadcasted_iota(jnp.int32, (LANES, channels), 1)
    spread = (ch_t // gsz == gr_t).astype(BF16)
    return gather, spread, 1.0 / gsz


def _group_norm_rows(y, groups):
    gather, spread, inv = _group_matrices(y.shape[-1], groups)
    mu = _dot3(y, gather) * inv
    d = y - _dot3(mu, spread)
    var = _dot3(d * d, gather) * inv
    return d * _dot3(lax.rsqrt(var + EPS), spread)


def _conv_finish(acc, cb, gn_g, gn_b):
    cv = _group_norm_rows(acc + cb, CONV_GROUPS) * gn_g + gn_b
    return _silu(cv)


def _conv_prompt_kernel(u_ref, halo_ref, cw_ref, cb_ref, gg_ref, gb_ref, o_ref, xp_ref, y_ref, *, tl):
    i = pl.program_id(1)
    halo = halo_ref[...]
    xp_ref[0:CONV_HALO, :] = jnp.where(i == 0, jnp.zeros_like(halo), halo)
    xp_ref[CONV_HALO:CONV_HALO + tl, :] = u_ref[...]
    base = CONV_HALO - (CONV_WIDTH - 1)
    rows = 64
    for r0 in range(0, tl, rows):
        for c0 in range(0, u_ref.shape[1], LANES):
            acc = jnp.zeros((rows, LANES), F32)
            for w in range(CONV_WIDTH):
                acc = acc + xp_ref[r0 + base + w:r0 + base + w + rows, c0:c0 + LANES] * cw_ref[w:w + 1, c0:c0 + LANES]
            y_ref[r0:r0 + rows, c0:c0 + LANES] = acc
    y = _conv_finish(y_ref[...], cb_ref[...], gg_ref[...], gb_ref[...])
    o_ref[...] = y.astype(o_ref.dtype)


def _conv_prompt(u, cw, cb, gn_g, gn_b, *, batch, seq_len, tl):
    m, c = u.shape
    nl = seq_len // tl
    hb = tl // CONV_HALO
    vec = pl.BlockSpec((1, c), lambda b, i: (0, 0))
    return pl.pallas_call(
        functools.partial(_conv_prompt_kernel, tl=tl),
        grid=(batch, nl),
        in_specs=[pl.BlockSpec((tl, c), lambda b, i: (b * nl + i, 0)),
                  pl.BlockSpec((CONV_HALO, c), lambda b, i: (jnp.maximum((b * nl + i) * hb - 1, 0), 0)),
                  pl.BlockSpec((CONV_HALO, c), lambda b, i: (0, 0)), vec, vec, vec],
        out_specs=pl.BlockSpec((tl, c), lambda b, i: (b * nl + i, 0)),
        out_shape=jax.ShapeDtypeStruct((m, c), BF16),
        scratch_shapes=[pltpu.VMEM((CONV_HALO + tl, c), F32), pltpu.VMEM((tl, c), F32)],
        compiler_params=_cparams(("arbitrary", "arbitrary")),
        name="conv_prompt",
    )(u, u, cw, cb, gn_g, gn_b)


def _conv_step_kernel(st_ref, u_ref, cw_ref, cb_ref, gg_ref, gb_ref, o_ref):
    acc = u_ref[...] * cw_ref[CONV_WIDTH - 1:CONV_WIDTH, :]
    for w in range(CONV_WIDTH - 1):
        acc = acc + st_ref[w] * cw_ref[w:w + 1, :]
    o_ref[...] = _conv_finish(acc, cb_ref[...], gg_ref[...], gb_ref[...]).astype(o_ref.dtype)


def _conv_step(state_t, u, cw, cb, gn_g, gn_b):
    bd, c = u.shape
    full = lambda shape: pl.BlockSpec(shape, lambda i: (0,) * len(shape))
    return pl.pallas_call(
        _conv_step_kernel,
        grid=(1,),
        in_specs=[full(state_t.shape), full((bd, c)), full(cw.shape), full((1, c)), full((1, c)), full((1, c))],
        out_specs=full((bd, c)),
        out_shape=jax.ShapeDtypeStruct((bd, c), BF16),
        compiler_params=_cparams(("arbitrary",)),
        name="conv_step",
    )(state_t, u, cw, cb, gn_g, gn_b)


def _proj_res_kernel(*refs, n_in):
    a_refs = refs[:n_in]
    w_refs = refs[n_in:2 * n_in]
    x_ref, gate_ref, o_ref = refs[2 * n_in:]
    acc = jnp.dot(a_refs[0][...], w_refs[0][...], preferred_element_type=F32)
    for a_ref, w_ref in zip(a_refs[1:], w_refs[1:]):
        acc = acc + jnp.dot(a_ref[...], w_ref[...], preferred_element_type=F32)
    o_ref[...] = x_ref[...] + gate_ref[...] * acc


def _proj_res(acts, weights, x, gate, *, tm):
    m, d = x.shape
    nmod, rows, _ = gate.shape
    tiles_per_mod = (m // nmod) // tm
    n_in = len(acts)
    in_specs = ([pl.BlockSpec((tm, a.shape[1]), lambda i: (i, 0)) for a in acts]
                + [pl.BlockSpec(w.shape, lambda i: (0, 0)) for w in weights]
                + [pl.BlockSpec((tm, d), lambda i: (i, 0)), _mod_spec(rows, d, tiles_per_mod)])
    return pl.pallas_call(
        functools.partial(_proj_res_kernel, n_in=n_in),
        grid=(m // tm,),
        in_specs=in_specs,
        out_specs=pl.BlockSpec((tm, d), lambda i: (i, 0)),
        out_shape=jax.ShapeDtypeStruct((m, d), F32),
        compiler_params=_cparams(("arbitrary",)),
        name="proj_res",
    )(*acts, *weights, x, gate)


def _ffn_kernel(x_ref, g_ref, sc_ref, sh_ref, gate_ref, wg_ref, wu_ref, wd_ref, fg_ref, o_ref, hf_ref, acc_ref,
                *, final):
    j = pl.program_id(1)

    @pl.when(j == 0)
    def _():
        hf_ref[...] = _norm_mod(x_ref[...], g_ref[...], sc_ref[...], sh_ref[...]).astype(BF16)
        acc_ref[...] = jnp.zeros_like(acc_ref)

    hf = hf_ref[...]
    gt = jnp.dot(hf, wg_ref[...], preferred_element_type=F32)
    up = jnp.dot(hf, wu_ref[...], preferred_element_type=F32)
    acc_ref[...] += jnp.dot((_silu(gt) * up).astype(BF16), wd_ref[...], preferred_element_type=F32)

    @pl.when(j == pl.num_programs(1) - 1)
    def _():
        y = x_ref[...] + gate_ref[...] * acc_ref[...]
        if final:
            ms = jnp.mean(y * y, axis=-1, keepdims=True)
            y = y * lax.rsqrt(ms + EPS) * fg_ref[...]
        o_ref[...] = y


def _ffn(x, g, sc, sh, gate, wg, wu, wd, fg, *, tm, th, final):
    m, d = x.shape
    nmod, rows, _ = sc.shape
    hdim = wg.shape[1]
    tiles_per_mod = (m // nmod) // tm
    vec = pl.BlockSpec((1, d), lambda i, j: (0, 0))
    mod = _mod_spec(rows, d, tiles_per_mod)
    return pl.pallas_call(
        functools.partial(_ffn_kernel, final=final),
        grid=(m // tm, hdim // th),
        in_specs=[pl.BlockSpec((tm, d), lambda i, j: (i, 0)), vec, mod, mod, mod,
                  pl.BlockSpec((d, th), lambda i, j: (0, j)), pl.BlockSpec((d, th), lambda i, j: (0, j)),
                  pl.BlockSpec((th, d), lambda i, j: (j, 0)), vec],
        out_specs=pl.BlockSpec((tm, d), lambda i, j: (i, 0)),
        out_shape=jax.ShapeDtypeStruct((m, d), F32),
        scratch_shapes=[pltpu.VMEM((tm, d), BF16), pltpu.VMEM((tm, d), F32)],
        compiler_params=_cparams(("arbitrary", "arbitrary")),
        name="ffn",
    )(x, g, sc, sh, gate, wg, wu, wd, fg)


def _odd_proj_kernel(x_ref, g_ref, sc_ref, sh_ref, w_ref, cos_ref, sin_ref, o_ref, hm_ref, *, n_rope, n_plain):
    j = pl.program_id(1)

    @pl.when(j == 0)
    def _():
        hm_ref[...] = _norm_mod(x_ref[...], g_ref[...], sc_ref[...], sh_ref[...]).astype(BF16)

    z = jnp.dot(hm_ref[...], w_ref[...], preferred_element_type=F32)

    @pl.when(j < n_rope)
    def _():
        x1 = z[:, 0:LANES]
        x2 = z[:, LANES:2 * LANES]
        cos = cos_ref[...]
        sin = sin_ref[...]
        o_ref[:, 0:LANES] = (x1 * cos - x2 * sin).astype(o_ref.dtype)
        o_ref[:, LANES:2 * LANES] = (x1 * sin + x2 * cos).astype(o_ref.dtype)

    @pl.when(jnp.logical_and(j >= n_rope, j < n_rope + n_plain))
    def _():
        o_ref[...] = z.astype(o_ref.dtype)

    @pl.when(j >= n_rope + n_plain)
    def _():
        o_ref[...] = _silu(z).astype(o_ref.dtype)


def _odd_proj(x, g, sc, sh, w, cos, sin, *, tm, n_rope, n_plain):
    m, d = x.shape
    nmod, rows, _ = sc.shape
    n = w.shape[1]
    tn = 2 * LANES
    tiles_per_mod = (m // nmod) // tm
    mod = _mod_spec(rows, d, tiles_per_mod)
    return pl.pallas_call(
        functools.partial(_odd_proj_kernel, n_rope=n_rope, n_plain=n_plain),
        grid=(m // tm, n // tn),
        in_specs=[pl.BlockSpec((tm, d), lambda i, j: (i, 0)), pl.BlockSpec((1, d), lambda i, j: (0, 0)), mod, mod,
                  pl.BlockSpec((d, tn), lambda i, j: (0, j)),
                  pl.BlockSpec((tm, LANES), lambda i, j: (i, 0)), pl.BlockSpec((tm, LANES), lambda i, j: (i, 0))],
        out_specs=pl.BlockSpec((tm, tn), lambda i, j: (i, j)),
        out_shape=jax.ShapeDtypeStruct((m, n), BF16),
        scratch_shapes=[pltpu.VMEM((tm, d), BF16)],
        compiler_params=_cparams(("arbitrary", "arbitrary")),
        name="odd_proj",
    )(x, g, sc, sh, w, cos, sin)


def _ret_log_gamma(h):
    return float(np.log(np.float64(1.0) - np.float64(2.0) ** (-5.0 - h)))


def _ret_finish(o, sg, gain):
    mu = jnp.mean(o, axis=-1, keepdims=True)
    d = o - mu
    var = jnp.mean(d * d, axis=-1, keepdims=True)
    r = d * lax.rsqrt(var + EPS) * gain
    return sg.astype(F32) * r


def _ret_prompt_kernel(q_ref, k_ref, v_ref, sg_ref, gain_ref, y_ref, s_out_ref, s_ref, *, chunk, dk, dv):
    c = pl.program_id(1)

    @pl.when(c == 0)
    def _():
        s_ref[...] = jnp.zeros_like(s_ref)

    ri = lax.broadcasted_iota(jnp.int32, (chunk, chunk), 0)
    ci = lax.broadcasted_iota(jnp.int32, (chunk, chunk), 1)
    diff = (ri - ci).astype(F32)
    pos = lax.broadcasted_iota(jnp.int32, (chunk, 1), 0).astype(F32)
    for h in range(RET_HEADS):
        lg = _ret_log_gamma(h)
        dmask = jnp.where(ri >= ci, jnp.exp(jnp.maximum(diff, 0.0) * lg), 0.0)
        q_dec = jnp.exp((pos + 1.0) * lg)
        k_dec = jnp.exp((chunk - 1.0 - pos) * lg)
        c_dec = math.exp(chunk * lg)
        q = q_ref[:, h * dk:(h + 1) * dk]
        k = k_ref[:, h * dk:(h + 1) * dk]
        v = v_ref[:, h * dv:(h + 1) * dv]
        s_prev = s_ref[h]
        a = _dot_nt(q, k) * dmask
        o = (jnp.dot(a.astype(BF16), v, preferred_element_type=F32)
             + jnp.dot(q, s_prev.astype(BF16), preferred_element_type=F32) * q_dec)
        kd = (k.astype(F32) * k_dec).T.astype(BF16)
        s_ref[h] = s_prev * c_dec + jnp.dot(kd, v, preferred_element_type=F32)
        cols = slice(h * dv, (h + 1) * dv)
        y_ref[:, cols] = _ret_finish(o, sg_ref[:, cols], gain_ref[:, cols]).astype(y_ref.dtype)

    @pl.when(c == pl.num_programs(1) - 1)
    def _():
        s_out_ref[...] = s_ref[...]


def _ret_prompt(z, gain, *, batch, seq_len, chunk, dk, dv):
    m = z.shape[0]
    nc = seq_len // chunk
    nk = RET_HEADS * dk
    nv = RET_HEADS * dv
    rowblk = lambda b, c: b * nc + c
    return pl.pallas_call(
        functools.partial(_ret_prompt_kernel, chunk=chunk, dk=dk, dv=dv),
        grid=(batch, nc),
        in_specs=[pl.BlockSpec((chunk, nk), lambda b, c: (rowblk(b, c), 0)),
                  pl.BlockSpec((chunk, nk), lambda b, c: (rowblk(b, c), 1)),
                  pl.BlockSpec((chunk, nv), lambda b, c: (rowblk(b, c), (2 * nk) // nv)),
                  pl.BlockSpec((chunk, nv), lambda b, c: (rowblk(b, c), (2 * nk) // nv + 1)),
                  pl.BlockSpec((1, nv), lambda b, c: (0, 0))],
        out_specs=[pl.BlockSpec((chunk, nv), lambda b, c: (rowblk(b, c), 0)),
                   pl.BlockSpec((None, RET_HEADS, dk, dv), lambda b, c: (b, 0, 0, 0))],
        out_shape=[jax.ShapeDtypeStruct((m, nv), BF16), jax.ShapeDtypeStruct((batch, RET_HEADS, dk, dv), F32)],
        scratch_shapes=[pltpu.VMEM((RET_HEADS, dk, dv), F32)],
        compiler_params=_cparams(("arbitrary", "arbitrary")),
        name="ret_prompt",
    )(z, z, z, z, gain)


def _ret_step_kernel(z_ref, s_ref, gain_ref, y_ref, s_out_ref, *, dk, dv):
    nk = RET_HEADS * dk
    nv = RET_HEADS * dv
    row0 = lax.broadcasted_iota(jnp.int32, (LANES, 1), 0) == 0
    for h in range(RET_HEADS):
        gamma = math.exp(_ret_log_gamma(h))
        q = z_ref[:, h * dk:(h + 1) * dk]
        k = z_ref[:, nk + h * dk:nk + (h + 1) * dk]
        v = z_ref[:, 2 * nk + h * dv:2 * nk + (h + 1) * dv]
        sg = z_ref[:, 2 * nk + nv + h * dv:2 * nk + nv + (h + 1) * dv]
        s_prev = s_ref[h]
        q_rows = jnp.broadcast_to(q.astype(F32), (16, dk)).astype(BF16)
        a = jnp.sum(q.astype(F32) * k.astype(F32), axis=-1, keepdims=True)
        o = a * v.astype(F32) + jnp.dot(q_rows, s_prev.astype(BF16), preferred_element_type=F32)[0:1, :] * gamma
        k_rows = jnp.where(row0, jnp.broadcast_to(k.astype(F32), (LANES, dk)), 0.0)
        v_rows = jnp.where(row0, jnp.broadcast_to(v.astype(F32), (LANES, dv)), 0.0)
        s_out_ref[h] = s_prev * gamma + jnp.dot(k_rows.T.astype(BF16), v_rows.astype(BF16),
                                                preferred_element_type=F32)
        cols = slice(h * dv, (h + 1) * dv)
        y_ref[:, cols] = _ret_finish(o, sg, gain_ref[:, cols]).astype(y_ref.dtype)


def _ret_step(z, state, gain, *, dk, dv):
    bd, n = z.shape
    nv = RET_HEADS * dv
    z3 = z.reshape(bd, 1, n)
    blk = pl.BlockSpec((None, RET_HEADS, dk, dv), lambda b: (b, 0, 0, 0))
    y, s_new = pl.pallas_call(
        functools.partial(_ret_step_kernel, dk=dk, dv=dv),
        grid=(bd,),
        in_specs=[pl.BlockSpec((None, 1, n), lambda b: (b, 0, 0)), blk, pl.BlockSpec((1, nv), lambda b: (0, 0))],
        out_specs=[pl.BlockSpec((None, 1, nv), lambda b: (b, 0, 0)), blk],
        out_shape=[jax.ShapeDtypeStruct((bd, 1, nv), BF16), jax.ShapeDtypeStruct(state.shape, F32)],
        compiler_params=_cparams(("arbitrary",)),
        name="ret_step",
    )(z3, state, gain)
    return y.reshape(bd, nv), s_new


def _fox_decode_kernel(pt_ref, q_ref, kn_ref, vn_ref, lfn_ref, *refs, group):
    del pt_ref
    k_refs = refs[0:group]
    v_refs = refs[group:2 * group]
    lp_refs = refs[2 * group:3 * group]
    o_ref, m_ref, l_ref, r_ref, acc_ref = refs[3 * group:]
    c = pl.program_id(1)
    head = lax.broadcasted_iota(jnp.int32, (FOX_HEADS, FOX_DIM), 0)
    chan = lax.broadcasted_iota(jnp.int32, (FOX_HEADS, FOX_DIM), 1)
    own = chan // FOX_HEAD_DIM == head
    qf = jnp.where(own, jnp.broadcast_to(q_ref[...].astype(F32), (FOX_HEADS, FOX_DIM)), 0.0)
    qb = qf.astype(BF16)

    @pl.when(c == 0)
    def _():
        m_ref[...] = jnp.sum(qf * kn_ref[...], axis=-1, keepdims=True)
        l_ref[...] = jnp.ones_like(l_ref)
        r_ref[...] = jnp.zeros_like(r_ref)
        acc_ref[...] = jnp.broadcast_to(vn_ref[...], acc_ref.shape)

    rj = lax.broadcasted_iota(jnp.int32, (PAGE_SIZE, PAGE_SIZE), 0)
    ck = lax.broadcasted_iota(jnp.int32, (PAGE_SIZE, PAGE_SIZE), 1)
    later = (rj > ck).astype(BF16)
    lfn = lfn_ref[...]
    for g in range(group):
        kg = k_refs[g][...].astype(BF16)
        vg = v_refs[g][...].astype(BF16)
        lp = lp_refs[g][...]
        decay = lfn + r_ref[...] + _dot3(lp, later)
        s = _dot_nt(qb, kg) + decay
        m_prev = m_ref[...]
        m_new = jnp.maximum(m_prev, jnp.max(s, axis=-1, keepdims=True))
        alpha = jnp.exp(m_prev - m_new)
        p = jnp.exp(s - m_new)
        l_ref[...] = alpha * l_ref[...] + jnp.sum(p, axis=-1, keepdims=True)
        acc_ref[...] = alpha * acc_ref[...] + jnp.dot(p.astype(BF16), vg, preferred_element_type=F32)
        m_ref[...] = m_new
        r_ref[...] = r_ref[...] + jnp.sum(lp, axis=-1, keepdims=True)

    @pl.when(c == pl.num_programs(1) - 1)
    def _():
        o = jnp.where(own, acc_ref[...] / l_ref[...], 0.0)
        o_ref[...] = jnp.sum(o, axis=0, keepdims=True).astype(o_ref.dtype)


def _fox_decode(q, k_new, v_new, logf_new, cache_k, cache_v, cache_lpt, page_table, layer, *, group):
    bd, c = q.shape
    n_pages = page_table.shape[1]
    assert n_pages % group == 0
    pt_flat = page_table.reshape(-1)

    def page_spec(g, shape):
        def index(b, s, pt):
            return (layer, pt[b * n_pages + (n_pages - 1 - (s * group + g))], 0, 0)
        return pl.BlockSpec((None, None) + shape, index)

    row = lambda width: pl.BlockSpec((None, 1, width), lambda b, s, pt: (b, 0, 0))
    in_specs = ([row(c), row(c), row(c), pl.BlockSpec((None, FOX_HEADS, 1), lambda b, s, pt: (b, 0, 0))]
                + [page_spec(g, (PAGE_SIZE, c)) for g in range(group)]
                + [page_spec(g, (PAGE_SIZE, c)) for g in range(group)]
                + [page_spec(g, (FOX_HEADS, PAGE_SIZE)) for g in range(group)])
    out = pl.pallas_call(
        functools.partial(_fox_decode_kernel, group=group),
        grid_spec=pltpu.PrefetchScalarGridSpec(
            num_scalar_prefetch=1,
            grid=(bd, n_pages // group),
            in_specs=in_specs,
            out_specs=row(c),
            scratch_shapes=[pltpu.VMEM((FOX_HEADS, 1), F32), pltpu.VMEM((FOX_HEADS, 1), F32),
                            pltpu.VMEM((FOX_HEADS, 1), F32), pltpu.VMEM((FOX_HEADS, c), F32)]),
        out_shape=jax.ShapeDtypeStruct((bd, 1, c), BF16),
        compiler_params=_cparams(("arbitrary", "arbitrary")),
        name="fox_decode",
    )(pt_flat, q.reshape(bd, 1, c), k_new.reshape(bd, 1, c), v_new.reshape(bd, 1, c),
      logf_new.reshape(bd, FOX_HEADS, 1), *([cache_k] * group), *([cache_v] * group), *([cache_lpt] * group))
    return out.reshape(bd, c)


def _rope_tables(pos, half):
    inv = 1.0 / (ROPE_BASE ** jnp.linspace(0.0, 1.0, half, dtype=F32))
    ang = pos.astype(F32)[:, None] * inv[None, :]
    return jnp.cos(ang), jnp.sin(ang)


def _prep_weights(w_in_even, b_forget, conv_w, w_out_even, w_in_odd, w_ffn_gate, w_ffn_up, w_ffn_down, dk):
    c = FOX_DIM
    n_even, d, _ = w_in_even.shape
    q, k, v, fl, a, b = jnp.split(w_in_even, [c, 2 * c, 3 * c, 3 * c + FOX_HEADS, 3 * c + FOX_HEADS + c], axis=-1)
    fl_pad = jnp.pad(fl, ((0, 0), (0, 0), (0, LANES - FOX_HEADS)))
    w_even = jnp.concatenate([q * (FOX_HEAD_DIM ** -0.5), k, v, a, b, fl_pad], axis=-1).astype(BF16)
    bf_pad = jnp.pad(b_forget, ((0, 0), (0, LANES - FOX_HEADS)))[:, None, :]
    cw_pad = jnp.pad(conv_w, ((0, 0), (0, CONV_HALO - CONV_WIDTH), (0, 0)))
    nk = RET_HEADS * dk
    w_odd = jnp.concatenate([w_in_odd[:, :, :nk], w_in_odd[:, :, nk:2 * nk] * (dk ** -0.5), w_in_odd[:, :, 2 * nk:]],
                            axis=-1).astype(BF16)
    return dict(w_even=w_even, bf_pad=bf_pad, cw_pad=cw_pad,
                w_out_fox=w_out_even[:, :c].astype(BF16), w_out_conv=w_out_even[:, c:].astype(BF16),
                w_odd=w_odd, wg=w_ffn_gate.astype(BF16), wu=w_ffn_up.astype(BF16), wd=w_ffn_down.astype(BF16))


def _trunk(x, mod, pos, P, W, *, seq_len, even_mixer, odd_mixer, tm, th):
    depth = mod.shape[0]
    d = x.shape[1]
    cos, sin = _rope_tables(pos, LANES)
    ks, vs, lfs, cbs, rs = [], [], [], [], []
    for l in range(depth):
        sh1, sc1, g1, sh2, sc2, g2 = [mod[l][..., t * d:(t + 1) * d] for t in range(6)]
        gm = W['norm_mix_g'][l][None, :]
        if l % 2 == 0:
            e = l // 2
            out = _even_proj(x, gm, sc1, sh1, P['w_even'][e], P['bf_pad'][e], tm=tm, tq=min(FOX_TQ, tm),
                             seq_len=seq_len, with_cum=even_mixer.needs_cum)
            q, k, v, kb, vb, logf, u = out[:7]
            o_fox, cv, new_buf = even_mixer(e, q, k, v, kb, vb, logf, u, out[7:])
            ks.append(k); vs.append(v); lfs.append(logf); cbs.append(new_buf)
            x = _proj_res([o_fox, cv], [P['w_out_fox'][e], P['w_out_conv'][e]], x, g1, tm=tm)
        else:
            o = l // 2
            z = _odd_proj(x, gm, sc1, sh1, P['w_odd'][o], cos, sin, tm=tm, n_rope=8, n_plain=8)
            y, s_new = odd_mixer(o, z)
            rs.append(s_new)
            x = _proj_res([y], [P['w_out_odd'][o]], x, g1, tm=tm)
        x = _ffn(x, W['norm_ffn_g'][l][None, :], sc2, sh2, g2, P['wg'][l], P['wu'][l], P['wd'][l],
                 W['final_norm_g'][None, :], tm=tm, th=th, final=(l == depth - 1))
    return x, (ks, vs, lfs, cbs, rs)


class _PromptEven:
    needs_cum = True

    def __init__(self, batch, seq_len, W, P):
        self.batch, self.seq_len, self.W, self.P = batch, seq_len, W, P

    def __call__(self, e, q, k, v, kb, vb, logf, u, extra):
        cum, cumt = extra
        W, P = self.W, self.P
        o_fox = _fox_prompt(q, kb, vb, cum, cumt, batch=self.batch, seq_len=self.seq_len, tq=FOX_TQ)
        cv = _conv_prompt(u, P['cw_pad'][e], W['conv_b'][e][None, :], W['conv_norm_g'][e][None, :],
                          W['conv_norm_b'][e][None, :], batch=self.batch, seq_len=self.seq_len, tl=256)
        c = u.shape[1]
        new_buf = u.reshape(self.batch, self.seq_len, c)[:, self.seq_len - (CONV_WIDTH - 1):]
        return o_fox, cv, new_buf


class _SampleEven:
    needs_cum = False

    def __init__(self, W, P, cache_k, cache_v, cache_lpt, state_conv, page_table):
        self.W, self.P = W, P
        self.cache_k, self.cache_v, self.cache_lpt = cache_k, cache_v, cache_lpt
        self.state_conv, self.page_table = state_conv, page_table

    def __call__(self, e, q, k, v, kb, vb, logf, u, extra):
        W, P = self.W, self.P
        o_fox = _fox_decode(q, k, v, logf, self.cache_k, self.cache_v, self.cache_lpt, self.page_table, e, group=8)
        st = self.state_conv[e]
        cv = _conv_step(jnp.transpose(st, (1, 0, 2)), u, P['cw_pad'][e], W['conv_b'][e][None, :],
                        W['conv_norm_g'][e][None, :], W['conv_norm_b'][e][None, :])
        new_buf = jnp.concatenate([st[:, 1:], u[:, None, :]], axis=1)
        return o_fox, cv, new_buf


def kernel(x_prompt, x_sample, cache_k, cache_v, cache_logf, state_conv, state_ret, page_table, c_prompt, c_sample,
           ada_w, ada_b, norm_mix_g, norm_ffn_g, w_in_even, b_forget, conv_w, conv_b, conv_norm_g, conv_norm_b,
           w_out_even, w_in_odd, ret_norm_g, w_out_odd, w_ffn_gate, w_ffn_up, w_ffn_down, final_norm_g):
    bp, lp, d = x_prompt.shape
    bs, ls, _ = x_sample.shape
    assert ls == 1, "the decode path handles one new token per sequence"
    depth = ada_w.shape[0]
    n_even, n_phys, page, heads, hd = cache_k.shape
    assert (page, heads, hd) == (PAGE_SIZE, FOX_HEADS, FOX_HEAD_DIM)
    dk, dv = state_ret.shape[-2:]
    past_len = page_table.shape[1] * PAGE_SIZE

    W = dict(norm_mix_g=norm_mix_g, norm_ffn_g=norm_ffn_g, conv_b=conv_b, conv_norm_g=conv_norm_g,
             conv_norm_b=conv_norm_b, final_norm_g=final_norm_g)
    P = _prep_weights(w_in_even, b_forget, conv_w, w_out_even, w_in_odd, w_ffn_gate, w_ffn_up, w_ffn_down, dk)
    P['w_out_odd'] = w_out_odd.astype(BF16)
    gain = ret_norm_g[:, None, :]

    mod = _adaln(jnp.concatenate([c_prompt, c_sample], axis=0), ada_w, ada_b)
    mod_p = mod[:, :bp].reshape(depth, bp, 1, 6 * d)
    mod_s = mod[:, bp:].reshape(depth, 1, bs, 6 * d)

    def odd_prompt(o, z):
        return _ret_prompt(z, gain[o], batch=bp, seq_len=lp, chunk=min(256, lp), dk=dk, dv=dv)

    y_p, (k_p, v_p, lf_p, cb_p, r_p) = _trunk(
        x_prompt.reshape(bp * lp, d), mod_p, jnp.tile(jnp.arange(lp), bp), P, W, seq_len=lp,
        even_mixer=_PromptEven(bp, lp, W, P), odd_mixer=odd_prompt, tm=512, th=256)

    ck = cache_k.reshape(n_even, n_phys, PAGE_SIZE, FOX_DIM)
    cv = cache_v.reshape(n_even, n_phys, PAGE_SIZE, FOX_DIM)
    clpt = jnp.transpose(cache_logf, (0, 1, 3, 2))

    def odd_sample(o, z):
        return _ret_step(z, state_ret[o], gain[o], dk=dk, dv=dv)

    y_s, (k_s, v_s, lf_s, cb_s, r_s) = _trunk(
        x_sample.reshape(bs, d), mod_s, jnp.full((bs,), past_len, jnp.int32), P, W, seq_len=1,
        even_mixer=_SampleEven(W, P, ck, cv, clpt, state_conv, page_table), odd_mixer=odd_sample, tm=bs, th=1408)

    hshape = (FOX_HEADS, FOX_HEAD_DIM)
    return (y_p.reshape(bp, lp, d), y_s.reshape(bs, ls, d),
            jnp.stack(k_p).reshape((n_even, bp, lp) + hshape), jnp.stack(v_p).reshape((n_even, bp, lp) + hshape),
            jnp.stack(lf_p).reshape(n_even, bp, lp, FOX_HEADS), jnp.stack(cb_p), jnp.stack(r_p),
            jnp.stack(k_s).reshape((n_even, bs, ls) + hshape), jnp.stack(v_s).reshape((n_even, bs, ls) + hshape),
            jnp.stack(lf_s).reshape(n_even, bs, ls, FOX_HEADS), jnp.stack(cb_s),
            jnp.stack(r_s).astype(state_ret.dtype))
```

```python
import functools
import math

import jax
import jax.numpy as jnp
import numpy as np
from jax import lax
from jax.experimental import pallas as pl
from jax.experimental.pallas import tpu as pltpu

F32 = jnp.float32
BF16 = jnp.bfloat16

EPS = 1e-6
ROPE_BASE = 10000.0
FOX_HEADS = 8
FOX_HEAD_DIM = 64
FOX_DIM = FOX_HEADS * FOX_HEAD_DIM
CONV_GROUPS = 8
CONV_WIDTH = 31
RET_HEADS = 4
PAGE_SIZE = 128
LANES = 128
CONV_HALO = 32
VMEM_LIMIT = 56 * 1024 * 1024
NEG_BIG = -1e30
FOX_TQ = 256


def _cparams(sem):
    return pltpu.CompilerParams(dimension_semantics=sem, vmem_limit_bytes=VMEM_LIMIT)


def _silu(x):
    return x * jax.nn.sigmoid(x)


def _log_sigmoid(x):
    return jnp.minimum(x, 0.0) - jnp.log1p(jnp.exp(-jnp.abs(x)))


def _norm_mod(x, g, sc, sh):
    ms = jnp.mean(x * x, axis=-1, keepdims=True)
    y = x * lax.rsqrt(ms + EPS) * g
    return y * (1.0 + sc) + sh


def _split3(x):
    hi = x.astype(BF16)
    r1 = x - hi.astype(F32)
    mid = r1.astype(BF16)
    lo = (r1 - mid.astype(F32)).astype(BF16)
    return hi, mid, lo


def _dot3(x, w_bf16):
    hi, mid, lo = _split3(x)
    d = functools.partial(jnp.dot, preferred_element_type=F32)
    return d(hi, w_bf16) + d(mid, w_bf16) + d(lo, w_bf16)


def _dot_nt(a, b):
    return lax.dot_general(a, b, (((1,), (1,)), ((), ())), preferred_element_type=F32)


def _mod_spec(rows, d, tiles_per_mod):
    return pl.BlockSpec((None, rows, d), lambda i, *_: (i // tiles_per_mod, 0, 0))


def _adaln_kernel(c_ref, w_ref, b_ref, o_ref):
    cm = _silu(c_ref[...]).astype(BF16)
    o_ref[...] = jnp.dot(cm, w_ref[...].astype(BF16), preferred_element_type=F32) + b_ref[...]


def _adaln(c_all, ada_w, ada_b, tn=1536):
    depth, d, n = ada_w.shape
    r = c_all.shape[0]
    return pl.pallas_call(
        _adaln_kernel,
        grid=(depth, n // tn),
        in_specs=[pl.BlockSpec((r, d), lambda l, j: (0, 0)),
                  pl.BlockSpec((None, d, tn), lambda l, j: (l, 0, j)),
                  pl.BlockSpec((None, 1, tn), lambda l, j: (l, 0, j))],
        out_specs=pl.BlockSpec((None, r, tn), lambda l, j: (l, 0, j)),
        out_shape=jax.ShapeDtypeStruct((depth, r, n), F32),
        compiler_params=_cparams(("arbitrary", "arbitrary")),
        name="adaln",
    )(c_all, ada_w, ada_b.reshape(depth, 1, n))


def _even_proj_kernel(x_ref, g_ref, sc_ref, sh_ref, w_ref, bf_ref, *refs, tm, tq, tiles_per_seq, with_cum):
    if with_cum:
        q_ref, k_ref, v_ref, lf_ref, u_ref, kb_ref, vt_ref, cum_ref, cumt_ref, carry_ref = refs
    else:
        q_ref, k_ref, v_ref, lf_ref, u_ref = refs
    hm = _norm_mod(x_ref[...], g_ref[...], sc_ref[...], sh_ref[...]).astype(BF16)
    z = jnp.dot(hm, w_ref[...], preferred_element_type=F32)
    c = FOX_DIM
    q_ref[...] = z[:, 0:c].astype(BF16)
    k = z[:, c:2 * c]
    v = z[:, 2 * c:3 * c]
    k_ref[...] = k
    v_ref[...] = v
    if with_cum:
        kb_ref[...] = k.astype(BF16)
        v_t = v.T
        for r in range(tm // tq):
            vt_ref[r] = v_t[:, r * tq:(r + 1) * tq].astype(BF16)
    a = z[:, 3 * c:4 * c]
    b = z[:, 4 * c:5 * c]
    u_ref[...] = a * jax.nn.sigmoid(b)
    logf = _log_sigmoid(z[:, 5 * c:5 * c + LANES] + bf_ref[...])
    lf_ref[...] = logf[:, 0:FOX_HEADS]
    if with_cum:
        i = pl.program_id(0)

        @pl.when(i % tiles_per_seq == 0)
        def _():
            carry_ref[...] = jnp.zeros_like(carry_ref)

        row = lax.broadcasted_iota(jnp.int32, (tm, tm), 0)
        col = lax.broadcasted_iota(jnp.int32, (tm, tm), 1)
        tri = (col <= row).astype(BF16)
        cum = _dot3_left(tri, logf) + carry_ref[...]
        carry_ref[...] = cum[tm - 1:tm, :]
        cum_ref[...] = cum[:, 0:FOX_HEADS]
        cum_t = cum.T
        for r in range(tm // tq):
            cumt_ref[r] = cum_t[0:FOX_HEADS, r * tq:(r + 1) * tq]


def _dot3_left(w_bf16, x):
    hi, mid, lo = _split3(x)
    d = functools.partial(jnp.dot, preferred_element_type=F32)
    return d(w_bf16, hi) + d(w_bf16, mid) + d(w_bf16, lo)


def _even_proj(x, g, sc, sh, w, bf_pad, *, tm, tq, seq_len, with_cum):
    m, d = x.shape
    nmod, rows, _ = sc.shape
    n = w.shape[1]
    tiles_per_mod = (m // nmod) // tm
    c = FOX_DIM
    row_spec = lambda width: pl.BlockSpec((tm, width), lambda i: (i, 0))
    out_shape = [jax.ShapeDtypeStruct((m, c), BF16), jax.ShapeDtypeStruct((m, c), F32),
                 jax.ShapeDtypeStruct((m, c), F32), jax.ShapeDtypeStruct((m, FOX_HEADS), F32),
                 jax.ShapeDtypeStruct((m, c), F32)]
    out_specs = [row_spec(c)] * 3 + [row_spec(FOX_HEADS), row_spec(c)]
    scratch = []
    if with_cum:
        out_shape += [jax.ShapeDtypeStruct((m, c), BF16), jax.ShapeDtypeStruct((m // tq, c, tq), BF16),
                      jax.ShapeDtypeStruct((m, FOX_HEADS), F32), jax.ShapeDtypeStruct((m // tq, FOX_HEADS, tq), F32)]
        out_specs += [row_spec(c), pl.BlockSpec((tm // tq, c, tq), lambda i: (i, 0, 0)),
                      row_spec(FOX_HEADS), pl.BlockSpec((tm // tq, FOX_HEADS, tq), lambda i: (i, 0, 0))]
        scratch = [pltpu.VMEM((1, LANES), F32)]
    return pl.pallas_call(
        functools.partial(_even_proj_kernel, tm=tm, tq=tq, tiles_per_seq=max(seq_len // tm, 1), with_cum=with_cum),
        grid=(m // tm,),
        in_specs=[row_spec(d), pl.BlockSpec((1, d), lambda i: (0, 0)),
                  _mod_spec(rows, d, tiles_per_mod), _mod_spec(rows, d, tiles_per_mod),
                  pl.BlockSpec((d, n), lambda i: (0, 0)), pl.BlockSpec((1, LANES), lambda i: (0, 0))],
        out_specs=out_specs,
        out_shape=out_shape,
        scratch_shapes=scratch,
        compiler_params=_cparams(("arbitrary",)),
        name="even_proj",
    )(x, g, sc, sh, w, bf_pad)


def _fox_prompt_kernel(q_ref, k_ref, vt_ref, ck_ref, cq_ref, o_ref, qm_ref, m_ref, l_ref, acc_ref, *, tq):
    i = pl.program_id(1)
    hd = FOX_HEAD_DIM
    low = lax.broadcasted_iota(jnp.int32, (1, LANES), 1) < hd
    for pair in range(FOX_HEADS // 2):
        qp = q_ref[:, pair * LANES:(pair + 1) * LANES].astype(F32)
        qm_ref[2 * pair] = jnp.where(low, qp, 0.0).astype(BF16)
        qm_ref[2 * pair + 1] = jnp.where(low, 0.0, qp).astype(BF16)
    m_ref[...] = jnp.full_like(m_ref, NEG_BIG)
    l_ref[...] = jnp.zeros_like(l_ref)
    acc_ref[...] = jnp.zeros_like(acc_ref)
    key = lax.broadcasted_iota(jnp.int32, (tq, tq), 0)
    qry = lax.broadcasted_iota(jnp.int32, (tq, tq), 1)
    causal = key <= qry

    def step(j, masked):
        start = pl.multiple_of(j * tq, tq)
        cq_all = cq_ref[i]
        qk = []
        for h in range(FOX_HEADS):
            pair = h // 2
            kj = k_ref[pl.ds(start, tq), pair * LANES:(pair + 1) * LANES]
            qk.append(_dot_nt(kj, qm_ref[h]))
        for h in range(FOX_HEADS):
            t = qk[h] - ck_ref[pl.ds(start, tq), h:h + 1]
            if masked:
                t = jnp.where(causal, t, NEG_BIG)
            cq = cq_all[h:h + 1, :]
            m_prev = m_ref[h:h + 1, :]
            m_new = jnp.maximum(m_prev, jnp.max(t, axis=0, keepdims=True) + cq)
            p = jnp.exp(t + (cq - m_new))
            alpha = jnp.exp(m_prev - m_new)
            l_ref[h:h + 1, :] = alpha * l_ref[h:h + 1, :] + jnp.sum(p, axis=0, keepdims=True)
            rows = slice(h * hd, (h + 1) * hd)
            acc_ref[rows, :] = alpha * acc_ref[rows, :] + jnp.dot(vt_ref[j][rows, :], p.astype(BF16),
                                                                  preferred_element_type=F32)
            m_ref[h:h + 1, :] = m_new

    def body(j, carry):
        step(j, False)
        return carry

    lax.fori_loop(0, i, body, 0)
    step(i, True)
    for pair in range(FOX_HEADS // 2):
        inv0 = 1.0 / l_ref[2 * pair:2 * pair + 1, :]
        inv1 = 1.0 / l_ref[2 * pair + 1:2 * pair + 2, :]
        top = acc_ref[2 * pair * hd:(2 * pair + 1) * hd, :] * inv0
        bot = acc_ref[(2 * pair + 1) * hd:(2 * pair + 2) * hd, :] * inv1
        o_ref[:, pair * LANES:(pair + 1) * LANES] = jnp.concatenate([top, bot], axis=0).T.astype(o_ref.dtype)


def _fox_prompt(q, kb, vt, cum, cumt, *, batch, seq_len, tq):
    m, c = q.shape
    nq = seq_len // tq
    return pl.pallas_call(
        functools.partial(_fox_prompt_kernel, tq=tq),
        grid=(batch, nq),
        in_specs=[pl.BlockSpec((tq, c), lambda b, i: (b * nq + i, 0)),
                  pl.BlockSpec((seq_len, c), lambda b, i: (b, 0)),
                  pl.BlockSpec((nq, c, tq), lambda b, i: (b, 0, 0)),
                  pl.BlockSpec((seq_len, FOX_HEADS), lambda b, i: (b, 0)),
                  pl.BlockSpec((nq, FOX_HEADS, tq), lambda b, i: (b, 0, 0))],
        out_specs=pl.BlockSpec((tq, c), lambda b, i: (b * nq + i, 0)),
        out_shape=jax.ShapeDtypeStruct((m, c), BF16),
        scratch_shapes=[pltpu.VMEM((FOX_HEADS, tq, LANES), BF16), pltpu.VMEM((FOX_HEADS, tq), F32),
                        pltpu.VMEM((FOX_HEADS, tq), F32), pltpu.VMEM((c, tq), F32)],
        compiler_params=_cparams(("arbitrary", "arbitrary")),
        name="fox_prompt",
    )(q, kb, vt, cum, cumt)


def _group_matrices(channels, groups):
    gsz = channels // groups
    ch = lax.broadcasted_iota(jnp.int32, (channels, LANES), 0)
    gr = lax.broadcasted_iota(jnp.int32, (channels, LANES), 1)
    gather = (ch // gsz == gr).astype(BF16)
    gr_t = lax.broadcasted_iota(jnp.int32, (LANES, channels), 0)
    ch_t = lax.broadcasted_iota(jnp.int32, (LANES, channels), 1)
    spread = (ch_t // gsz == gr_t).astype(BF16)
    return gather, spread, 1.0 / gsz


def _group_norm_rows(y, groups):
    gather, spread, inv = _group_matrices(y.shape[-1], groups)
    mu = _dot3(y, gather) * inv
    d = y - _dot3(mu, spread)
    var = _dot3(d * d, gather) * inv
    return d * _dot3(lax.rsqrt(var + EPS), spread)


def _conv_finish(acc, cb, gn_g, gn_b):
    cv = _group_norm_rows(acc + cb, CONV_GROUPS) * gn_g + gn_b
    return _silu(cv)


def _conv_prompt_kernel(u_ref, halo_ref, cw_ref, cb_ref, gg_ref, gb_ref, o_ref, xp_ref, y_ref, *, tl):
    i = pl.program_id(1)
    halo = halo_ref[...]
    xp_ref[0:CONV_HALO, :] = jnp.where(i == 0, jnp.zeros_like(halo), halo)
    xp_ref[CONV_HALO:CONV_HALO + tl, :] = u_ref[...]
    base = CONV_HALO - (CONV_WIDTH - 1)
    rows = 64
    for r0 in range(0, tl, rows):
        for c0 in range(0, u_ref.shape[1], LANES):
            acc = jnp.zeros((rows, LANES), F32)
            for w in range(CONV_WIDTH):
                acc = acc + xp_ref[r0 + base + w:r0 + base + w + rows, c0:c0 + LANES] * cw_ref[w:w + 1, c0:c0 + LANES]
            y_ref[r0:r0 + rows, c0:c0 + LANES] = acc
    y = _conv_finish(y_ref[...], cb_ref[...], gg_ref[...], gb_ref[...])
    o_ref[...] = y.astype(o_ref.dtype)


def _conv_prompt(u, cw, cb, gn_g, gn_b, *, batch, seq_len, tl):
    m, c = u.shape
    nl = seq_len // tl
    hb = tl // CONV_HALO
    vec = pl.BlockSpec((1, c), lambda b, i: (0, 0))
    return pl.pallas_call(
        functools.partial(_conv_prompt_kernel, tl=tl),
        grid=(batch, nl),
        in_specs=[pl.BlockSpec((tl, c), lambda b, i: (b * nl + i, 0)),
                  pl.BlockSpec((CONV_HALO, c), lambda b, i: (jnp.maximum((b * nl + i) * hb - 1, 0), 0)),
                  pl.BlockSpec((CONV_HALO, c), lambda b, i: (0, 0)), vec, vec, vec],
        out_specs=pl.BlockSpec((tl, c), lambda b, i: (b * nl + i, 0)),
        out_shape=jax.ShapeDtypeStruct((m, c), BF16),
        scratch_shapes=[pltpu.VMEM((CONV_HALO + tl, c), F32), pltpu.VMEM((tl, c), F32)],
        compiler_params=_cparams(("arbitrary", "arbitrary")),
        name="conv_prompt",
    )(u, u, cw, cb, gn_g, gn_b)


def _conv_step_kernel(st_ref, u_ref, cw_ref, cb_ref, gg_ref, gb_ref, o_ref):
    acc = u_ref[...] * cw_ref[CONV_WIDTH - 1:CONV_WIDTH, :]
    for w in range(CONV_WIDTH - 1):
        acc = acc + st_ref[w] * cw_ref[w:w + 1, :]
    o_ref[...] = _conv_finish(acc, cb_ref[...], gg_ref[...], gb_ref[...]).astype(o_ref.dtype)


def _conv_step(state_t, u, cw, cb, gn_g, gn_b):
    bd, c = u.shape
    full = lambda shape: pl.BlockSpec(shape, lambda i: (0,) * len(shape))
    return pl.pallas_call(
        _conv_step_kernel,
        grid=(1,),
        in_specs=[full(state_t.shape), full((bd, c)), full(cw.shape), full((1, c)), full((1, c)), full((1, c))],
        out_specs=full((bd, c)),
        out_shape=jax.ShapeDtypeStruct((bd, c), BF16),
        compiler_params=_cparams(("arbitrary",)),
        name="conv_step",
    )(state_t, u, cw, cb, gn_g, gn_b)


def _proj_res_kernel(*refs, n_in):
    a_refs = refs[:n_in]
    w_refs = refs[n_in:2 * n_in]
    x_ref, gate_ref, o_ref = refs[2 * n_in:]
    acc = jnp.dot(a_refs[0][...], w_refs[0][...], preferred_element_type=F32)
    for a_ref, w_ref in zip(a_refs[1:], w_refs[1:]):
        acc = acc + jnp.dot(a_ref[...], w_ref[...], preferred_element_type=F32)
    o_ref[...] = x_ref[...] + gate_ref[...] * acc


def _proj_res(acts, weights, x, gate, *, tm):
    m, d = x.shape
    nmod, rows, _ = gate.shape
    tiles_per_mod = (m // nmod) // tm
    n_in = len(acts)
    in_specs = ([pl.BlockSpec((tm, a.shape[1]), lambda i: (i, 0)) for a in acts]
                + [pl.BlockSpec(w.shape, lambda i: (0, 0)) for w in weights]
                + [pl.BlockSpec((tm, d), lambda i: (i, 0)), _mod_spec(rows, d, tiles_per_mod)])
    return pl.pallas_call(
        functools.partial(_proj_res_kernel, n_in=n_in),
        grid=(m // tm,),
        in_specs=in_specs,
        out_specs=pl.BlockSpec((tm, d), lambda i: (i, 0)),
        out_shape=jax.ShapeDtypeStruct((m, d), F32),
        compiler_params=_cparams(("arbitrary",)),
        name="proj_res",
    )(*acts, *weights, x, gate)


def _ffn_kernel(x_ref, g_ref, sc_ref, sh_ref, gate_ref, wg_ref, wu_ref, wd_ref, fg_ref, o_ref, hf_ref, acc_ref,
                *, final):
    j = pl.program_id(1)

    @pl.when(j == 0)
    def _():
        hf_ref[...] = _norm_mod(x_ref[...], g_ref[...], sc_ref[...], sh_ref[...]).astype(BF16)
        acc_ref[...] = jnp.zeros_like(acc_ref)

    hf = hf_ref[...]
    gt = jnp.dot(hf, wg_ref[...], preferred_element_type=F32)
    up = jnp.dot(hf, wu_ref[...], preferred_element_type=F32)
    acc_ref[...] += jnp.dot((_silu(gt) * up).astype(BF16), wd_ref[...], preferred_element_type=F32)

    @pl.when(j == pl.num_programs(1) - 1)
    def _():
        y = x_ref[...] + gate_ref[...] * acc_ref[...]
        if final:
            ms = jnp.mean(y * y, axis=-1, keepdims=True)
            y = y * lax.rsqrt(ms + EPS) * fg_ref[...]
        o_ref[...] = y


def _ffn(x, g, sc, sh, gate, wg, wu, wd, fg, *, tm, th, final):
    m, d = x.shape
    nmod, rows, _ = sc.shape
    hdim = wg.shape[1]
    tiles_per_mod = (m // nmod) // tm
    vec = pl.BlockSpec((1, d), lambda i, j: (0, 0))
    mod = _mod_spec(rows, d, tiles_per_mod)
    return pl.pallas_call(
        functools.partial(_ffn_kernel, final=final),
        grid=(m // tm, hdim // th),
        in_specs=[pl.BlockSpec((tm, d), lambda i, j: (i, 0)), vec, mod, mod, mod,
                  pl.BlockSpec((d, th), lambda i, j: (0, j)), pl.BlockSpec((d, th), lambda i, j: (0, j)),
                  pl.BlockSpec((th, d), lambda i, j: (j, 0)), vec],
        out_specs=pl.BlockSpec((tm, d), lambda i, j: (i, 0)),
        out_shape=jax.ShapeDtypeStruct((m, d), F32),
        scratch_shapes=[pltpu.VMEM((tm, d), BF16), pltpu.VMEM((tm, d), F32)],
        compiler_params=_cparams(("arbitrary", "arbitrary")),
        name="ffn",
    )(x, g, sc, sh, gate, wg, wu, wd, fg)


def _odd_proj_kernel(x_ref, g_ref, sc_ref, sh_ref, w_ref, cos_ref, sin_ref, o_ref, hm_ref, *, n_rope, n_plain):
    j = pl.program_id(1)

    @pl.when(j == 0)
    def _():
        hm_ref[...] = _norm_mod(x_ref[...], g_ref[...], sc_ref[...], sh_ref[...]).astype(BF16)

    z = jnp.dot(hm_ref[...], w_ref[...], preferred_element_type=F32)

    @pl.when(j < n_rope)
    def _():
        cos = cos_ref[...]
        sin = sin_ref[...]
        for c0 in range(0, z.shape[1], 2 * LANES):
            x1 = z[:, c0:c0 + LANES]
            x2 = z[:, c0 + LANES:c0 + 2 * LANES]
            o_ref[:, c0:c0 + LANES] = (x1 * cos - x2 * sin).astype(o_ref.dtype)
            o_ref[:, c0 + LANES:c0 + 2 * LANES] = (x1 * sin + x2 * cos).astype(o_ref.dtype)

    @pl.when(jnp.logical_and(j >= n_rope, j < n_rope + n_plain))
    def _():
        o_ref[...] = z.astype(o_ref.dtype)

    @pl.when(j >= n_rope + n_plain)
    def _():
        o_ref[...] = _silu(z).astype(o_ref.dtype)


def _odd_proj(x, g, sc, sh, w, cos, sin, *, tm, tn, n_rope, n_plain):
    m, d = x.shape
    nmod, rows, _ = sc.shape
    n = w.shape[1]
    tiles_per_mod = (m // nmod) // tm
    mod = _mod_spec(rows, d, tiles_per_mod)
    return pl.pallas_call(
        functools.partial(_odd_proj_kernel, n_rope=n_rope, n_plain=n_plain),
        grid=(m // tm, n // tn),
        in_specs=[pl.BlockSpec((tm, d), lambda i, j: (i, 0)), pl.BlockSpec((1, d), lambda i, j: (0, 0)), mod, mod,
                  pl.BlockSpec((d, tn), lambda i, j: (0, j)),
                  pl.BlockSpec((tm, LANES), lambda i, j: (i, 0)), pl.BlockSpec((tm, LANES), lambda i, j: (i, 0))],
        out_specs=pl.BlockSpec((tm, tn), lambda i, j: (i, j)),
        out_shape=jax.ShapeDtypeStruct((m, n), BF16),
        scratch_shapes=[pltpu.VMEM((tm, d), BF16)],
        compiler_params=_cparams(("arbitrary", "arbitrary")),
        name="odd_proj",
    )(x, g, sc, sh, w, cos, sin)


def _ret_log_gamma(h):
    return float(np.log(np.float64(1.0) - np.float64(2.0) ** (-5.0 - h)))


def _ret_finish(o, sg, gain):
    mu = jnp.mean(o, axis=-1, keepdims=True)
    d = o - mu
    var = jnp.mean(d * d, axis=-1, keepdims=True)
    r = d * lax.rsqrt(var + EPS) * gain
    return sg.astype(F32) * r


def _ret_prompt_kernel(q_ref, k_ref, v_ref, sg_ref, gain_ref, y_ref, s_out_ref, s_ref, *, chunk, dk, dv):
    c = pl.program_id(1)

    @pl.when(c == 0)
    def _():
        s_ref[...] = jnp.zeros_like(s_ref)

    ri = lax.broadcasted_iota(jnp.int32, (chunk, chunk), 0)
    ci = lax.broadcasted_iota(jnp.int32, (chunk, chunk), 1)
    diff = (ri - ci).astype(F32)
    pos = lax.broadcasted_iota(jnp.int32, (chunk, 1), 0).astype(F32)
    for h in range(RET_HEADS):
        lg = _ret_log_gamma(h)
        dmask = jnp.where(ri >= ci, jnp.exp(jnp.maximum(diff, 0.0) * lg), 0.0)
        q_dec = jnp.exp((pos + 1.0) * lg)
        k_dec = jnp.exp((chunk - 1.0 - pos) * lg)
        c_dec = math.exp(chunk * lg)
        q = q_ref[:, h * dk:(h + 1) * dk]
        k = k_ref[:, h * dk:(h + 1) * dk]
        v = v_ref[:, h * dv:(h + 1) * dv]
        s_prev = s_ref[h]
        a = _dot_nt(q, k) * dmask
        o = (jnp.dot(a.astype(BF16), v, preferred_element_type=F32)
             + jnp.dot(q, s_prev.astype(BF16), preferred_element_type=F32) * q_dec)
        kd = (k.astype(F32) * k_dec).T.astype(BF16)
        s_ref[h] = s_prev * c_dec + jnp.dot(kd, v, preferred_element_type=F32)
        cols = slice(h * dv, (h + 1) * dv)
        y_ref[:, cols] = _ret_finish(o, sg_ref[:, cols], gain_ref[:, cols]).astype(y_ref.dtype)

    @pl.when(c == pl.num_programs(1) - 1)
    def _():
        s_out_ref[...] = s_ref[...]


def _ret_prompt(z, gain, *, batch, seq_len, chunk, dk, dv):
    m = z.shape[0]
    nc = seq_len // chunk
    nk = RET_HEADS * dk
    nv = RET_HEADS * dv
    rowblk = lambda b, c: b * nc + c
    return pl.pallas_call(
        functools.partial(_ret_prompt_kernel, chunk=chunk, dk=dk, dv=dv),
        grid=(batch, nc),
        in_specs=[pl.BlockSpec((chunk, nk), lambda b, c: (rowblk(b, c), 0)),
                  pl.BlockSpec((chunk, nk), lambda b, c: (rowblk(b, c), 1)),
                  pl.BlockSpec((chunk, nv), lambda b, c: (rowblk(b, c), (2 * nk) // nv)),
                  pl.BlockSpec((chunk, nv), lambda b, c: (rowblk(b, c), (2 * nk) // nv + 1)),
                  pl.BlockSpec((1, nv), lambda b, c: (0, 0))],
        out_specs=[pl.BlockSpec((chunk, nv), lambda b, c: (rowblk(b, c), 0)),
                   pl.BlockSpec((None, RET_HEADS, dk, dv), lambda b, c: (b, 0, 0, 0))],
        out_shape=[jax.ShapeDtypeStruct((m, nv), BF16), jax.ShapeDtypeStruct((batch, RET_HEADS, dk, dv), F32)],
        scratch_shapes=[pltpu.VMEM((RET_HEADS, dk, dv), F32)],
        compiler_params=_cparams(("arbitrary", "arbitrary")),
        name="ret_prompt",
    )(z, z, z, z, gain)


def _ret_step_kernel(z_ref, s_ref, gain_ref, y_ref, s_out_ref, *, dk, dv):
    nk = RET_HEADS * dk
    nv = RET_HEADS * dv
    row0 = lax.broadcasted_iota(jnp.int32, (LANES, 1), 0) == 0
    for h in range(RET_HEADS):
        gamma = math.exp(_ret_log_gamma(h))
        q = z_ref[:, h * dk:(h + 1) * dk]
        k = z_ref[:, nk + h * dk:nk + (h + 1) * dk]
        v = z_ref[:, 2 * nk + h * dv:2 * nk + (h + 1) * dv]
        sg = z_ref[:, 2 * nk + nv + h * dv:2 * nk + nv + (h + 1) * dv]
        s_prev = s_ref[h]
        q_rows = jnp.broadcast_to(q.astype(F32), (16, dk)).astype(BF16)
        a = jnp.sum(q.astype(F32) * k.astype(F32), axis=-1, keepdims=True)
        o = a * v.astype(F32) + jnp.dot(q_rows, s_prev.astype(BF16), preferred_element_type=F32)[0:1, :] * gamma
        k_rows = jnp.where(row0, jnp.broadcast_to(k.astype(F32), (LANES, dk)), 0.0)
        v_rows = jnp.where(row0, jnp.broadcast_to(v.astype(F32), (LANES, dv)), 0.0)
        s_out_ref[h] = s_prev * gamma + jnp.dot(k_rows.T.astype(BF16), v_rows.astype(BF16),
                                                preferred_element_type=F32)
        cols = slice(h * dv, (h + 1) * dv)
        y_ref[:, cols] = _ret_finish(o, sg, gain_ref[:, cols]).astype(y_ref.dtype)


def _ret_step(z, states, layer, gain, *, dk, dv):
    bd, n = z.shape
    nv = RET_HEADS * dv
    z3 = z.reshape(bd, 1, n)
    blk = pl.BlockSpec((None, RET_HEADS, dk, dv), lambda b: (b, 0, 0, 0))
    y, s_new = pl.pallas_call(
        functools.partial(_ret_step_kernel, dk=dk, dv=dv),
        grid=(bd,),
        in_specs=[pl.BlockSpec((None, 1, n), lambda b: (b, 0, 0)),
                  pl.BlockSpec((None, None, RET_HEADS, dk, dv), lambda b: (layer, b, 0, 0, 0)),
                  pl.BlockSpec((1, nv), lambda b: (0, 0))],
        out_specs=[pl.BlockSpec((None, 1, nv), lambda b: (b, 0, 0)), blk],
        out_shape=[jax.ShapeDtypeStruct((bd, 1, nv), BF16), jax.ShapeDtypeStruct(states.shape[1:], F32)],
        compiler_params=_cparams(("arbitrary",)),
        name="ret_step",
    )(z3, states, gain)
    return y.reshape(bd, nv), s_new


def _fox_decode_kernel(pt_ref, q_ref, kn_ref, vn_ref, lfn_ref, *refs, group):
    del pt_ref
    k_refs = refs[0:group]
    v_refs = refs[group:2 * group]
    lp_refs = refs[2 * group:3 * group]
    o_ref, m_ref, l_ref, r_ref, acc_ref = refs[3 * group:]
    c = pl.program_id(1)
    n = PAGE_SIZE * FOX_HEADS
    q = q_ref[...]

    @pl.when(c == 0)
    def _():
        m_ref[...] = jnp.sum(q * kn_ref[...], axis=-1, keepdims=True)
        l_ref[...] = jnp.ones_like(l_ref)
        r_ref[...] = jnp.zeros_like(r_ref)
        acc_ref[...] = vn_ref[...]

    lane = lax.broadcasted_iota(jnp.int32, (1, n), 1)
    lp = jnp.concatenate([lp_refs[g][...] for g in range(group)], axis=0)
    suffix = lp
    total = lp
    stride = FOX_HEADS
    while stride < n:
        ahead = pltpu.roll(suffix, n - stride, axis=1)
        suffix = suffix + jnp.where(lane < n - stride, ahead, 0.0)
        total = total + pltpu.roll(total, stride, axis=1)
        stride *= 2
    page_row = lax.broadcasted_iota(jnp.int32, (group, 1), 0)
    before = total
    shift = 1
    while shift < group:
        before = before + jnp.where(page_row >= shift, pltpu.roll(before, shift, axis=0), 0.0)
        shift *= 2
    decay = lfn_ref[...] + r_ref[...] + (before - total) + (suffix - lp)
    r_ref[...] = r_ref[...] + before[group - 1:group, :]

    head = lax.broadcasted_iota(jnp.int32, (FOX_HEADS, 1), 0)
    own = (lane & (FOX_HEADS - 1)) == head
    pad = jnp.zeros((FOX_HEADS, FOX_HEAD_DIM), F32)
    q_rows = jnp.concatenate([q, pad], axis=0).astype(BF16)
    scores = []
    for g in range(group):
        kg = k_refs[g][...].reshape(n, FOX_HEAD_DIM).astype(BF16)
        s = _dot_nt(q_rows, kg)[0:FOX_HEADS, :] + decay[g:g + 1, :]
        scores.append(jnp.where(own, s, NEG_BIG))
    m_prev = m_ref[...]
    m_new = m_prev
    for s in scores:
        m_new = jnp.maximum(m_new, jnp.max(s, axis=-1, keepdims=True))
    alpha = jnp.exp(m_prev - m_new)
    l_new = alpha * l_ref[...]
    acc = alpha * acc_ref[...]
    pad_p = jnp.zeros((FOX_HEADS, n), F32)
    for g in range(group):
        p = jnp.exp(scores[g] - m_new)
        l_new = l_new + jnp.sum(p, axis=-1, keepdims=True)
        vg = v_refs[g][...].reshape(n, FOX_HEAD_DIM).astype(BF16)
        p_rows = jnp.concatenate([p, pad_p], axis=0).astype(BF16)
        acc = acc + jnp.dot(p_rows, vg, preferred_element_type=F32)[0:FOX_HEADS, :]
    m_ref[...] = m_new
    l_ref[...] = l_new
    acc_ref[...] = acc

    @pl.when(c == pl.num_programs(1) - 1)
    def _():
        o_ref[...] = acc_ref[...] / l_ref[...]


def _fox_decode(q, k_new, v_new, logf_new, cache_k, cache_v, cache_lp, page_table, layer, *, group):
    bd = q.shape[0]
    n_pages = page_table.shape[1]
    assert n_pages % group == 0
    pt_flat = page_table.reshape(-1)
    n = PAGE_SIZE * FOX_HEADS
    hshape = (FOX_HEADS, FOX_HEAD_DIM)

    def page_spec(g, shape):
        def index(b, s, pt):
            return (layer, pt[b * n_pages + (n_pages - 1 - (s * group + g))]) + (0,) * len(shape)
        return pl.BlockSpec((None, None) + shape, index)

    tok = pl.BlockSpec((None,) + hshape, lambda b, s, pt: (b, 0, 0))
    in_specs = ([tok, tok, tok, pl.BlockSpec((None, 1, n), lambda b, s, pt: (b, 0, 0))]
                + [page_spec(g, (PAGE_SIZE,) + hshape) for g in range(group)]
                + [page_spec(g, (PAGE_SIZE,) + hshape) for g in range(group)]
                + [page_spec(g, (1, n)) for g in range(group)])
    return pl.pallas_call(
        functools.partial(_fox_decode_kernel, group=group),
        grid_spec=pltpu.PrefetchScalarGridSpec(
            num_scalar_prefetch=1,
            grid=(bd, n_pages // group),
            in_specs=in_specs,
            out_specs=tok,
            scratch_shapes=[pltpu.VMEM((FOX_HEADS, 1), F32), pltpu.VMEM((FOX_HEADS, 1), F32),
                            pltpu.VMEM((1, n), F32), pltpu.VMEM(hshape, F32)]),
        out_shape=jax.ShapeDtypeStruct((bd,) + hshape, F32),
        compiler_params=_cparams(("arbitrary", "arbitrary")),
        name="fox_decode",
    )(pt_flat, q, k_new, v_new, logf_new, *([cache_k] * group), *([cache_v] * group), *([cache_lp] * group))


def _rope_tables(pos, half):
    inv = 1.0 / (ROPE_BASE ** jnp.linspace(0.0, 1.0, half, dtype=F32))
    ang = pos.astype(F32)[:, None] * inv[None, :]
    return jnp.cos(ang), jnp.sin(ang)


def _prep_weights(w_in_even, b_forget, conv_w, w_out_even, w_in_odd, w_ffn_gate, w_ffn_up, w_ffn_down, dk):
    c = FOX_DIM
    n_even, d, _ = w_in_even.shape
    q, k, v, fl, a, b = jnp.split(w_in_even, [c, 2 * c, 3 * c, 3 * c + FOX_HEADS, 3 * c + FOX_HEADS + c], axis=-1)
    fl_pad = jnp.pad(fl, ((0, 0), (0, 0), (0, LANES - FOX_HEADS)))
    w_even = jnp.concatenate([q * (FOX_HEAD_DIM ** -0.5), k, v, a, b, fl_pad], axis=-1).astype(BF16)
    bf_pad = jnp.pad(b_forget, ((0, 0), (0, LANES - FOX_HEADS)))[:, None, :]
    cw_pad = jnp.pad(conv_w, ((0, 0), (0, CONV_HALO - CONV_WIDTH), (0, 0)))
    nk = RET_HEADS * dk
    w_odd = jnp.concatenate([w_in_odd[:, :, :nk], w_in_odd[:, :, nk:2 * nk] * (dk ** -0.5), w_in_odd[:, :, 2 * nk:]],
                            axis=-1).astype(BF16)
    return dict(w_even=w_even, bf_pad=bf_pad, cw_pad=cw_pad,
                w_out_fox=w_out_even[:, :c].astype(BF16), w_out_conv=w_out_even[:, c:].astype(BF16),
                w_odd=w_odd, wg=w_ffn_gate.astype(BF16), wu=w_ffn_up.astype(BF16), wd=w_ffn_down.astype(BF16))


def _tiles(rows, seq_len):
    if seq_len == 1:
        return dict(tm=rows, tm_ffn=rows, th=1408, tm_odd=rows, tn_odd=1024)
    return dict(tm=min(512, seq_len), tm_ffn=min(512, seq_len), th=1408, tm_odd=min(1024, seq_len), tn_odd=1024)


def _trunk(x, mod, pos, P, W, *, seq_len, even_mixer, odd_mixer):
    depth = mod.shape[0]
    d = x.shape[1]
    t = _tiles(x.shape[0], seq_len)
    tm = t['tm']
    cos, sin = _rope_tables(pos, LANES)
    nk = P['nk']
    ks, vs, lfs, cbs, rs = [], [], [], [], []
    for l in range(depth):
        sh1, sc1, g1, sh2, sc2, g2 = [mod[l][..., s * d:(s + 1) * d] for s in range(6)]
        gm = W['norm_mix_g'][l][None, :]
        if l % 2 == 0:
            e = l // 2
            out = _even_proj(x, gm, sc1, sh1, P['w_even'][e], P['bf_pad'][e], tm=tm, tq=min(FOX_TQ, tm),
                             seq_len=seq_len, with_cum=even_mixer.needs_cum)
            q, k, v, logf, u = out[:5]
            o_fox, cv, new_buf = even_mixer(e, q, k, v, logf, u, out[5:])
            ks.append(k); vs.append(v); lfs.append(logf); cbs.append(new_buf)
            x = _proj_res([o_fox, cv], [P['w_out_fox'][e], P['w_out_conv'][e]], x, g1, tm=tm)
        else:
            o = l // 2
            n_odd = P['w_odd'].shape[-1]
            z = _odd_proj(x, gm, sc1, sh1, P['w_odd'][o], cos, sin, tm=t['tm_odd'], tn=t['tn_odd'],
                          n_rope=2 * nk // t['tn_odd'], n_plain=(n_odd - 2 * nk) // 2 // t['tn_odd'])
            y, s_new = odd_mixer(o, z)
            rs.append(s_new)
            x = _proj_res([y], [P['w_out_odd'][o]], x, g1, tm=tm)
        x = _ffn(x, W['norm_ffn_g'][l][None, :], sc2, sh2, g2, P['wg'][l], P['wu'][l], P['wd'][l],
                 W['final_norm_g'][None, :], tm=t['tm_ffn'], th=t['th'], final=(l == depth - 1))
    return x, (ks, vs, lfs, cbs, rs)


class _PromptEven:
    needs_cum = True

    def __init__(self, batch, seq_len, W, P):
        self.batch, self.seq_len, self.W, self.P = batch, seq_len, W, P

    def __call__(self, e, q, k, v, logf, u, extra):
        kb, vt, cum, cumt = extra
        W, P = self.W, self.P
        o_fox = _fox_prompt(q, kb, vt, cum, cumt, batch=self.batch, seq_len=self.seq_len, tq=FOX_TQ)
        cv = _conv_prompt(u, P['cw_pad'][e], W['conv_b'][e][None, :], W['conv_norm_g'][e][None, :],
                          W['conv_norm_b'][e][None, :], batch=self.batch, seq_len=self.seq_len, tl=256)
        c = u.shape[1]
        new_buf = u.reshape(self.batch, self.seq_len, c)[:, self.seq_len - (CONV_WIDTH - 1):]
        return o_fox, cv, new_buf


class _SampleEven:
    needs_cum = False

    def __init__(self, W, P, cache_k, cache_v, cache_lp, state_conv, page_table):
        self.W, self.P = W, P
        self.cache_k, self.cache_v, self.cache_lp = cache_k, cache_v, cache_lp
        self.state_conv, self.page_table = state_conv, page_table

    def __call__(self, e, q, k, v, logf, u, extra):
        W, P = self.W, self.P
        bd = q.shape[0]
        heads = lambda t: t.astype(F32).reshape(bd, FOX_HEADS, FOX_HEAD_DIM)
        lf_keys = jnp.tile(logf, (1, PAGE_SIZE))[:, None, :]
        o_fox = _fox_decode(heads(q), heads(k), heads(v), lf_keys, self.cache_k, self.cache_v, self.cache_lp,
                            self.page_table, e, group=8).reshape(bd, FOX_DIM).astype(BF16)
        st = self.state_conv[e]
        cv = _conv_step(jnp.transpose(st, (1, 0, 2)), u, P['cw_pad'][e], W['conv_b'][e][None, :],
                        W['conv_norm_g'][e][None, :], W['conv_norm_b'][e][None, :])
        new_buf = jnp.concatenate([st[:, 1:], u[:, None, :]], axis=1)
        return o_fox, cv, new_buf


def kernel(x_prompt, x_sample, cache_k, cache_v, cache_logf, state_conv, state_ret, page_table, c_prompt, c_sample,
           ada_w, ada_b, norm_mix_g, norm_ffn_g, w_in_even, b_forget, conv_w, conv_b, conv_norm_g, conv_norm_b,
           w_out_even, w_in_odd, ret_norm_g, w_out_odd, w_ffn_gate, w_ffn_up, w_ffn_down, final_norm_g):
    bp, lp, d = x_prompt.shape
    bs, ls, _ = x_sample.shape
    assert ls == 1, "the decode path handles one new token per sequence"
    depth = ada_w.shape[0]
    n_even, n_phys, page, heads, hd = cache_k.shape
    assert (page, heads, hd) == (PAGE_SIZE, FOX_HEADS, FOX_HEAD_DIM)
    dk, dv = state_ret.shape[-2:]
    past_len = page_table.shape[1] * PAGE_SIZE

    W = dict(norm_mix_g=norm_mix_g, norm_ffn_g=norm_ffn_g, conv_b=conv_b, conv_norm_g=conv_norm_g,
             conv_norm_b=conv_norm_b, final_norm_g=final_norm_g)
    P = _prep_weights(w_in_even, b_forget, conv_w, w_out_even, w_in_odd, w_ffn_gate, w_ffn_up, w_ffn_down, dk)
    P['w_out_odd'] = w_out_odd.astype(BF16)
    P['nk'] = RET_HEADS * dk
    gain = ret_norm_g[:, None, :]

    mod = _adaln(jnp.concatenate([c_prompt, c_sample], axis=0), ada_w, ada_b)
    mod_p = mod[:, :bp].reshape(depth, bp, 1, 6 * d)
    mod_s = mod[:, bp:].reshape(depth, 1, bs, 6 * d)

    def odd_prompt(o, z):
        return _ret_prompt(z, gain[o], batch=bp, seq_len=lp, chunk=min(256, lp), dk=dk, dv=dv)

    y_p, (k_p, v_p, lf_p, cb_p, r_p) = _trunk(
        x_prompt.reshape(bp * lp, d), mod_p, jnp.tile(jnp.arange(lp), bp), P, W, seq_len=lp,
        even_mixer=_PromptEven(bp, lp, W, P), odd_mixer=odd_prompt)

    clp = cache_logf.reshape(n_even, n_phys, 1, PAGE_SIZE * FOX_HEADS)

    def odd_sample(o, z):
        return _ret_step(z, state_ret, o, gain[o], dk=dk, dv=dv)

    y_s, (k_s, v_s, lf_s, cb_s, r_s) = _trunk(
        x_sample.reshape(bs, d), mod_s, jnp.full((bs,), past_len, jnp.int32), P, W, seq_len=1,
        even_mixer=_SampleEven(W, P, cache_k, cache_v, clp, state_conv, page_table), odd_mixer=odd_sample)

    hshape = (FOX_HEADS, FOX_HEAD_DIM)
    return (y_p.reshape(bp, lp, d), y_s.reshape(bs, ls, d),
            jnp.stack(k_p).reshape((n_even, bp, lp) + hshape), jnp.stack(v_p).reshape((n_even, bp, lp) + hshape),
            jnp.stack(lf_p).reshape(n_even, bp, lp, FOX_HEADS), jnp.stack(cb_p), jnp.stack(r_p),
            jnp.stack(k_s).reshape((n_even, bs, ls) + hshape), jnp.stack(v_s).reshape((n_even, bs, ls) + hshape),
            jnp.stack(lf_s).reshape(n_even, bs, ls, FOX_HEADS), jnp.stack(cb_s),
            jnp.stack(r_s).astype(state_ret.dtype))
```

```python
import functools
import math

import jax
import jax.numpy as jnp
import numpy as np
from jax import lax
from jax.experimental import pallas as pl
from jax.experimental.pallas import tpu as pltpu

F32 = jnp.float32
BF16 = jnp.bfloat16

EPS = 1e-6
ROPE_BASE = 10000.0
FOX_HEADS = 8
FOX_HEAD_DIM = 64
FOX_DIM = FOX_HEADS * FOX_HEAD_DIM
CONV_GROUPS = 8
CONV_WIDTH = 31
RET_HEADS = 4
PAGE_SIZE = 128
LANES = 128
CONV_HALO = 32
VMEM_LIMIT = 56 * 1024 * 1024
NEG_BIG = -1e30
FOX_TQ = 256


def _cparams(sem):
    return pltpu.CompilerParams(dimension_semantics=sem, vmem_limit_bytes=VMEM_LIMIT)


def _silu(x):
    return x * jax.nn.sigmoid(x)


def _log_sigmoid(x):
    return jnp.minimum(x, 0.0) - jnp.log1p(jnp.exp(-jnp.abs(x)))


def _norm_mod(x, g, sc, sh):
    ms = jnp.mean(x * x, axis=-1, keepdims=True)
    y = x * lax.rsqrt(ms + EPS) * g
    return y * (1.0 + sc) + sh


def _split3(x):
    hi = x.astype(BF16)
    r1 = x - hi.astype(F32)
    mid = r1.astype(BF16)
    lo = (r1 - mid.astype(F32)).astype(BF16)
    return hi, mid, lo


def _dot3(x, w_bf16):
    hi, mid, lo = _split3(x)
    d = functools.partial(jnp.dot, preferred_element_type=F32)
    return d(hi, w_bf16) + d(mid, w_bf16) + d(lo, w_bf16)


def _dot_nt(a, b):
    return lax.dot_general(a, b, (((1,), (1,)), ((), ())), preferred_element_type=F32)


def _mod_spec(rows, d, tiles_per_mod):
    return pl.BlockSpec((None, rows, d), lambda i, *_: (i // tiles_per_mod, 0, 0))


def _adaln_kernel(c_ref, w_ref, b_ref, o_ref):
    cm = _silu(c_ref[...]).astype(BF16)
    o_ref[...] = jnp.dot(cm, w_ref[...].astype(BF16), preferred_element_type=F32) + b_ref[...]


def _adaln(c_all, ada_w, ada_b, tn=1536):
    depth, d, n = ada_w.shape
    r = c_all.shape[0]
    return pl.pallas_call(
        _adaln_kernel,
        grid=(depth, n // tn),
        in_specs=[pl.BlockSpec((r, d), lambda l, j: (0, 0)),
                  pl.BlockSpec((None, d, tn), lambda l, j: (l, 0, j)),
                  pl.BlockSpec((None, 1, tn), lambda l, j: (l, 0, j))],
        out_specs=pl.BlockSpec((None, r, tn), lambda l, j: (l, 0, j)),
        out_shape=jax.ShapeDtypeStruct((depth, r, n), F32),
        compiler_params=_cparams(("arbitrary", "arbitrary")),
        name="adaln",
    )(c_all, ada_w, ada_b.reshape(depth, 1, n))


def _even_proj_kernel(x_ref, g_ref, sc_ref, sh_ref, w_ref, bf_ref, *refs, tm, tq, tiles_per_seq, with_cum):
    if with_cum:
        q_ref, k_ref, v_ref, lf_ref, u_ref, kb_ref, vt_ref, cum_ref, cumt_ref, carry_ref = refs
    else:
        q_ref, k_ref, v_ref, lf_ref, u_ref = refs
    hm = _norm_mod(x_ref[...], g_ref[...], sc_ref[...], sh_ref[...]).astype(BF16)
    z = jnp.dot(hm, w_ref[...], preferred_element_type=F32)
    c = FOX_DIM
    q_ref[...] = z[:, 0:c].astype(BF16)
    k = z[:, c:2 * c]
    v = z[:, 2 * c:3 * c]
    a = z[:, 3 * c:4 * c]
    b = z[:, 4 * c:5 * c]
    u_ref[...] = a * jax.nn.sigmoid(b)
    logf = _log_sigmoid(z[:, 5 * c:5 * c + LANES] + bf_ref[...])
    if not with_cum:
        k_ref[...] = k
        v_ref[...] = v
        lf_ref[...] = logf[:, 0:FOX_HEADS]
    else:
        v_t = v.T
        k_ref[...] = k.T
        v_ref[...] = v_t
        lf_ref[...] = logf.T[0:FOX_HEADS, :]
        kb_ref[...] = k.astype(BF16)
        for r in range(tm // tq):
            vt_ref[r] = v_t[:, r * tq:(r + 1) * tq].astype(BF16)
        i = pl.program_id(0)

        @pl.when(i % tiles_per_seq == 0)
        def _():
            carry_ref[...] = jnp.zeros_like(carry_ref)

        row = lax.broadcasted_iota(jnp.int32, (tm, tm), 0)
        col = lax.broadcasted_iota(jnp.int32, (tm, tm), 1)
        tri = (col <= row).astype(BF16)
        cum = _dot3_left(tri, logf) + carry_ref[...]
        carry_ref[...] = cum[tm - 1:tm, :]
        cum_ref[...] = cum[:, 0:FOX_HEADS]
        cum_t = cum.T
        for r in range(tm // tq):
            cumt_ref[r] = cum_t[0:FOX_HEADS, r * tq:(r + 1) * tq]


def _dot3_left(w_bf16, x):
    hi, mid, lo = _split3(x)
    d = functools.partial(jnp.dot, preferred_element_type=F32)
    return d(w_bf16, hi) + d(w_bf16, mid) + d(w_bf16, lo)


def _even_proj(x, g, sc, sh, w, bf_pad, *, tm, tq, seq_len, with_cum):
    m, d = x.shape
    nmod, rows, _ = sc.shape
    n = w.shape[1]
    tiles_per_mod = (m // nmod) // tm
    c = FOX_DIM
    row_spec = lambda width: pl.BlockSpec((tm, width), lambda i: (i, 0))
    tiles_per_seq = max(seq_len // tm, 1)
    if with_cum:
        nseq = m // seq_len
        seq_major = lambda width: (jax.ShapeDtypeStruct((nseq, width, seq_len), F32),
                                   pl.BlockSpec((None, width, tm), lambda i: (i // tiles_per_seq, 0, i % tiles_per_seq)))
        kvl = [seq_major(c), seq_major(c), seq_major(FOX_HEADS)]
    else:
        kvl = [(jax.ShapeDtypeStruct((m, w_), F32), row_spec(w_)) for w_ in (c, c, FOX_HEADS)]
    out_shape = [jax.ShapeDtypeStruct((m, c), BF16)] + [s for s, _ in kvl] + [jax.ShapeDtypeStruct((m, c), F32)]
    out_specs = [row_spec(c)] + [s for _, s in kvl] + [row_spec(c)]
    scratch = []
    if with_cum:
        out_shape += [jax.ShapeDtypeStruct((m, c), BF16), jax.ShapeDtypeStruct((m // tq, c, tq), BF16),
                      jax.ShapeDtypeStruct((m, FOX_HEADS), F32), jax.ShapeDtypeStruct((m // tq, FOX_HEADS, tq), F32)]
        out_specs += [row_spec(c), pl.BlockSpec((tm // tq, c, tq), lambda i: (i, 0, 0)),
                      row_spec(FOX_HEADS), pl.BlockSpec((tm // tq, FOX_HEADS, tq), lambda i: (i, 0, 0))]
        scratch = [pltpu.VMEM((1, LANES), F32)]
    return pl.pallas_call(
        functools.partial(_even_proj_kernel, tm=tm, tq=tq, tiles_per_seq=tiles_per_seq, with_cum=with_cum),
        grid=(m // tm,),
        in_specs=[row_spec(d), pl.BlockSpec((1, d), lambda i: (0, 0)),
                  _mod_spec(rows, d, tiles_per_mod), _mod_spec(rows, d, tiles_per_mod),
                  pl.BlockSpec((d, n), lambda i: (0, 0)), pl.BlockSpec((1, LANES), lambda i: (0, 0))],
        out_specs=out_specs,
        out_shape=out_shape,
        scratch_shapes=scratch,
        compiler_params=_cparams(("arbitrary",)),
        name="even_proj",
    )(x, g, sc, sh, w, bf_pad)


def _fox_prompt_kernel(q_ref, k_ref, vt_ref, ck_ref, cq_ref, o_ref, qm_ref, m_ref, l_ref, acc_ref, *, tq):
    i = pl.program_id(1)
    hd = FOX_HEAD_DIM
    low = lax.broadcasted_iota(jnp.int32, (1, LANES), 1) < hd
    for pair in range(FOX_HEADS // 2):
        qp = q_ref[:, pair * LANES:(pair + 1) * LANES].astype(F32)
        qm_ref[2 * pair] = jnp.where(low, qp, 0.0).astype(BF16)
        qm_ref[2 * pair + 1] = jnp.where(low, 0.0, qp).astype(BF16)
    m_ref[...] = jnp.full_like(m_ref, NEG_BIG)
    l_ref[...] = jnp.zeros_like(l_ref)
    acc_ref[...] = jnp.zeros_like(acc_ref)
    key = lax.broadcasted_iota(jnp.int32, (tq, tq), 0)
    qry = lax.broadcasted_iota(jnp.int32, (tq, tq), 1)
    causal = key <= qry

    def step(j, masked):
        start = pl.multiple_of(j * tq, tq)
        cq_all = cq_ref[i]
        qk = []
        for h in range(FOX_HEADS):
            pair = h // 2
            kj = k_ref[pl.ds(start, tq), pair * LANES:(pair + 1) * LANES]
            qk.append(_dot_nt(kj, qm_ref[h]))
        for h in range(FOX_HEADS):
            t = qk[h] - ck_ref[pl.ds(start, tq), h:h + 1]
            if masked:
                t = jnp.where(causal, t, NEG_BIG)
            cq = cq_all[h:h + 1, :]
            m_prev = m_ref[h:h + 1, :]
            m_new = jnp.maximum(m_prev, jnp.max(t, axis=0, keepdims=True) + cq)
            p = jnp.exp(t + (cq - m_new))
            alpha = jnp.exp(m_prev - m_new)
            l_ref[h:h + 1, :] = alpha * l_ref[h:h + 1, :] + jnp.sum(p, axis=0, keepdims=True)
            rows = slice(h * hd, (h + 1) * hd)
            acc_ref[rows, :] = alpha * acc_ref[rows, :] + jnp.dot(vt_ref[j][rows, :], p.astype(BF16),
                                                                  preferred_element_type=F32)
            m_ref[h:h + 1, :] = m_new

    def body(j, carry):
        step(j, False)
        return carry

    lax.fori_loop(0, i, body, 0)
    step(i, True)
    for pair in range(FOX_HEADS // 2):
        inv0 = 1.0 / l_ref[2 * pair:2 * pair + 1, :]
        inv1 = 1.0 / l_ref[2 * pair + 1:2 * pair + 2, :]
        top = acc_ref[2 * pair * hd:(2 * pair + 1) * hd, :] * inv0
        bot = acc_ref[(2 * pair + 1) * hd:(2 * pair + 2) * hd, :] * inv1
        o_ref[:, pair * LANES:(pair + 1) * LANES] = jnp.concatenate([top, bot], axis=0).T.astype(o_ref.dtype)


def _fox_prompt(q, kb, vt, cum, cumt, *, batch, seq_len, tq):
    m, c = q.shape
    nq = seq_len // tq
    return pl.pallas_call(
        functools.partial(_fox_prompt_kernel, tq=tq),
        grid=(batch, nq),
        in_specs=[pl.BlockSpec((tq, c), lambda b, i: (b * nq + i, 0)),
                  pl.BlockSpec((seq_len, c), lambda b, i: (b, 0)),
                  pl.BlockSpec((nq, c, tq), lambda b, i: (b, 0, 0)),
                  pl.BlockSpec((seq_len, FOX_HEADS), lambda b, i: (b, 0)),
                  pl.BlockSpec((nq, FOX_HEADS, tq), lambda b, i: (b, 0, 0))],
        out_specs=pl.BlockSpec((tq, c), lambda b, i: (b * nq + i, 0)),
        out_shape=jax.ShapeDtypeStruct((m, c), BF16),
        scratch_shapes=[pltpu.VMEM((FOX_HEADS, tq, LANES), BF16), pltpu.VMEM((FOX_HEADS, tq), F32),
                        pltpu.VMEM((FOX_HEADS, tq), F32), pltpu.VMEM((c, tq), F32)],
        compiler_params=_cparams(("arbitrary", "arbitrary")),
        name="fox_prompt",
    )(q, kb, vt, cum, cumt)


def _group_matrices(channels, groups):
    gsz = channels // groups
    ch = lax.broadcasted_iota(jnp.int32, (channels, LANES), 0)
    gr = lax.broadcasted_iota(jnp.int32, (channels, LANES), 1)
    gather = (ch // gsz == gr).astype(BF16)
    gr_t = lax.broadcasted_iota(jnp.int32, (LANES, channels), 0)
    ch_t = lax.broadcasted_iota(jnp.int32, (LANES, channels), 1)
    spread = (ch_t // gsz == gr_t).astype(BF16)
    return gather, spread, 1.0 / gsz


def _group_norm_rows(y, groups):
    gather, spread, inv = _group_matrices(y.shape[-1], groups)
    mu = _dot3(y, gather) * inv
    d = y - _dot3(mu, spread)
    var = _dot3(d * d, gather) * inv
    return d * _dot3(lax.rsqrt(var + EPS), spread)


def _conv_finish(acc, cb, gn_g, gn_b):
    cv = _group_norm_rows(acc + cb, CONV_GROUPS) * gn_g + gn_b
    return _silu(cv)


def _conv_prompt_kernel(u_ref, halo_ref, cw_ref, cb_ref, gg_ref, gb_ref, o_ref, xp_ref, y_ref, *, tl):
    i = pl.program_id(1)
    halo = halo_ref[...]
    xp_ref[0:CONV_HALO, :] = jnp.where(i == 0, jnp.zeros_like(halo), halo)
    xp_ref[CONV_HALO:CONV_HALO + tl, :] = u_ref[...]
    base = CONV_HALO - (CONV_WIDTH - 1)
    rows = 64
    for r0 in range(0, tl, rows):
        for c0 in range(0, u_ref.shape[1], LANES):
            acc = jnp.zeros((rows, LANES), F32)
            for w in range(CONV_WIDTH):
                acc = acc + xp_ref[r0 + base + w:r0 + base + w + rows, c0:c0 + LANES] * cw_ref[w:w + 1, c0:c0 + LANES]
            y_ref[r0:r0 + rows, c0:c0 + LANES] = acc
    y = _conv_finish(y_ref[...], cb_ref[...], gg_ref[...], gb_ref[...])
    o_ref[...] = y.astype(o_ref.dtype)


def _conv_prompt(u, cw, cb, gn_g, gn_b, *, batch, seq_len, tl):
    m, c = u.shape
    nl = seq_len // tl
    hb = tl // CONV_HALO
    vec = pl.BlockSpec((1, c), lambda b, i: (0, 0))
    return pl.pallas_call(
        functools.partial(_conv_prompt_kernel, tl=tl),
        grid=(batch, nl),
        in_specs=[pl.BlockSpec((tl, c), lambda b, i: (b * nl + i, 0)),
                  pl.BlockSpec((CONV_HALO, c), lambda b, i: (jnp.maximum((b * nl + i) * hb - 1, 0), 0)),
                  pl.BlockSpec((CONV_HALO, c), lambda b, i: (0, 0)), vec, vec, vec],
        out_specs=pl.BlockSpec((tl, c), lambda b, i: (b * nl + i, 0)),
        out_shape=jax.ShapeDtypeStruct((m, c), BF16),
        scratch_shapes=[pltpu.VMEM((CONV_HALO + tl, c), F32), pltpu.VMEM((tl, c), F32)],
        compiler_params=_cparams(("arbitrary", "arbitrary")),
        name="conv_prompt",
    )(u, u, cw, cb, gn_g, gn_b)


def _conv_step_kernel(st_ref, u_ref, cw_ref, cb_ref, gg_ref, gb_ref, o_ref):
    acc = u_ref[...] * cw_ref[CONV_WIDTH - 1:CONV_WIDTH, :]
    for w in range(CONV_WIDTH - 1):
        acc = acc + st_ref[w] * cw_ref[w:w + 1, :]
    o_ref[...] = _conv_finish(acc, cb_ref[...], gg_ref[...], gb_ref[...]).astype(o_ref.dtype)


def _conv_step(state_t, u, cw, cb, gn_g, gn_b):
    bd, c = u.shape
    full = lambda shape: pl.BlockSpec(shape, lambda i: (0,) * len(shape))
    return pl.pallas_call(
        _conv_step_kernel,
        grid=(1,),
        in_specs=[full(state_t.shape), full((bd, c)), full(cw.shape), full((1, c)), full((1, c)), full((1, c))],
        out_specs=full((bd, c)),
        out_shape=jax.ShapeDtypeStruct((bd, c), BF16),
        compiler_params=_cparams(("arbitrary",)),
        name="conv_step",
    )(state_t, u, cw, cb, gn_g, gn_b)


def _proj_res_kernel(*refs, n_in):
    a_refs = refs[:n_in]
    w_refs = refs[n_in:2 * n_in]
    x_ref, gate_ref, o_ref = refs[2 * n_in:]
    acc = jnp.dot(a_refs[0][...], w_refs[0][...], preferred_element_type=F32)
    for a_ref, w_ref in zip(a_refs[1:], w_refs[1:]):
        acc = acc + jnp.dot(a_ref[...], w_ref[...], preferred_element_type=F32)
    o_ref[...] = x_ref[...] + gate_ref[...] * acc


def _proj_res(acts, weights, x, gate, *, tm):
    m, d = x.shape
    nmod, rows, _ = gate.shape
    tiles_per_mod = (m // nmod) // tm
    n_in = len(acts)
    in_specs = ([pl.BlockSpec((tm, a.shape[1]), lambda i: (i, 0)) for a in acts]
                + [pl.BlockSpec(w.shape, lambda i: (0, 0)) for w in weights]
                + [pl.BlockSpec((tm, d), lambda i: (i, 0)), _mod_spec(rows, d, tiles_per_mod)])
    return pl.pallas_call(
        functools.partial(_proj_res_kernel, n_in=n_in),
        grid=(m // tm,),
        in_specs=in_specs,
        out_specs=pl.BlockSpec((tm, d), lambda i: (i, 0)),
        out_shape=jax.ShapeDtypeStruct((m, d), F32),
        compiler_params=_cparams(("arbitrary",)),
        name="proj_res",
    )(*acts, *weights, x, gate)


def _ffn_kernel(x_ref, g_ref, sc_ref, sh_ref, gate_ref, wg_ref, wu_ref, wd_ref, fg_ref, o_ref, hf_ref, acc_ref,
                *, final):
    j = pl.program_id(1)

    @pl.when(j == 0)
    def _():
        hf_ref[...] = _norm_mod(x_ref[...], g_ref[...], sc_ref[...], sh_ref[...]).astype(BF16)
        acc_ref[...] = jnp.zeros_like(acc_ref)

    hf = hf_ref[...]
    gt = jnp.dot(hf, wg_ref[...], preferred_element_type=F32)
    up = jnp.dot(hf, wu_ref[...], preferred_element_type=F32)
    acc_ref[...] += jnp.dot((_silu(gt) * up).astype(BF16), wd_ref[...], preferred_element_type=F32)

    @pl.when(j == pl.num_programs(1) - 1)
    def _():
        y = x_ref[...] + gate_ref[...] * acc_ref[...]
        if final:
            ms = jnp.mean(y * y, axis=-1, keepdims=True)
            y = y * lax.rsqrt(ms + EPS) * fg_ref[...]
        o_ref[...] = y


def _ffn(x, g, sc, sh, gate, wg, wu, wd, fg, *, tm, th, final):
    m, d = x.shape
    nmod, rows, _ = sc.shape
    hdim = wg.shape[1]
    tiles_per_mod = (m // nmod) // tm
    vec = pl.BlockSpec((1, d), lambda i, j: (0, 0))
    mod = _mod_spec(rows, d, tiles_per_mod)
    return pl.pallas_call(
        functools.partial(_ffn_kernel, final=final),
        grid=(m // tm, hdim // th),
        in_specs=[pl.BlockSpec((tm, d), lambda i, j: (i, 0)), vec, mod, mod, mod,
                  pl.BlockSpec((d, th), lambda i, j: (0, j)), pl.BlockSpec((d, th), lambda i, j: (0, j)),
                  pl.BlockSpec((th, d), lambda i, j: (j, 0)), vec],
        out_specs=pl.BlockSpec((tm, d), lambda i, j: (i, 0)),
        out_shape=jax.ShapeDtypeStruct((m, d), F32),
        scratch_shapes=[pltpu.VMEM((tm, d), BF16), pltpu.VMEM((tm, d), F32)],
        compiler_params=_cparams(("arbitrary", "arbitrary")),
        name="ffn",
    )(x, g, sc, sh, gate, wg, wu, wd, fg)


def _odd_proj_kernel(x_ref, g_ref, sc_ref, sh_ref, w_ref, cos_ref, sin_ref, o_ref, hm_ref, *, n_rope, n_plain):
    j = pl.program_id(1)

    @pl.when(j == 0)
    def _():
        hm_ref[...] = _norm_mod(x_ref[...], g_ref[...], sc_ref[...], sh_ref[...]).astype(BF16)

    z = jnp.dot(hm_ref[...], w_ref[...], preferred_element_type=F32)

    @pl.when(j < n_rope)
    def _():
        cos = cos_ref[...]
        sin = sin_ref[...]
        for c0 in range(0, z.shape[1], 2 * LANES):
            x1 = z[:, c0:c0 + LANES]
            x2 = z[:, c0 + LANES:c0 + 2 * LANES]
            o_ref[:, c0:c0 + LANES] = (x1 * cos - x2 * sin).astype(o_ref.dtype)
            o_ref[:, c0 + LANES:c0 + 2 * LANES] = (x1 * sin + x2 * cos).astype(o_ref.dtype)

    @pl.when(jnp.logical_and(j >= n_rope, j < n_rope + n_plain))
    def _():
        o_ref[...] = z.astype(o_ref.dtype)

    @pl.when(j >= n_rope + n_plain)
    def _():
        o_ref[...] = _silu(z).astype(o_ref.dtype)


def _odd_proj(x, g, sc, sh, w, cos, sin, *, tm, tn, n_rope, n_plain):
    m, d = x.shape
    nmod, rows, _ = sc.shape
    n = w.shape[1]
    tiles_per_mod = (m // nmod) // tm
    mod = _mod_spec(rows, d, tiles_per_mod)
    return pl.pallas_call(
        functools.partial(_odd_proj_kernel, n_rope=n_rope, n_plain=n_plain),
        grid=(m // tm, n // tn),
        in_specs=[pl.BlockSpec((tm, d), lambda i, j: (i, 0)), pl.BlockSpec((1, d), lambda i, j: (0, 0)), mod, mod,
                  pl.BlockSpec((d, tn), lambda i, j: (0, j)),
                  pl.BlockSpec((tm, LANES), lambda i, j: (i, 0)), pl.BlockSpec((tm, LANES), lambda i, j: (i, 0))],
        out_specs=pl.BlockSpec((tm, tn), lambda i, j: (i, j)),
        out_shape=jax.ShapeDtypeStruct((m, n), BF16),
        scratch_shapes=[pltpu.VMEM((tm, d), BF16)],
        compiler_params=_cparams(("arbitrary", "arbitrary")),
        name="odd_proj",
    )(x, g, sc, sh, w, cos, sin)


def _ret_log_gamma(h):
    return float(np.log(np.float64(1.0) - np.float64(2.0) ** (-5.0 - h)))


def _ret_finish(o, sg, gain):
    mu = jnp.mean(o, axis=-1, keepdims=True)
    d = o - mu
    var = jnp.mean(d * d, axis=-1, keepdims=True)
    r = d * lax.rsqrt(var + EPS) * gain
    return sg.astype(F32) * r


def _ret_prompt_kernel(q_ref, k_ref, v_ref, sg_ref, gain_ref, y_ref, s_out_ref, s_ref, *, chunk, dk, dv):
    c = pl.program_id(1)

    @pl.when(c == 0)
    def _():
        s_ref[...] = jnp.zeros_like(s_ref)

    ri = lax.broadcasted_iota(jnp.int32, (chunk, chunk), 0)
    ci = lax.broadcasted_iota(jnp.int32, (chunk, chunk), 1)
    diff = (ri - ci).astype(F32)
    pos = lax.broadcasted_iota(jnp.int32, (chunk, 1), 0).astype(F32)
    for h in range(RET_HEADS):
        lg = _ret_log_gamma(h)
        dmask = jnp.where(ri >= ci, jnp.exp(jnp.maximum(diff, 0.0) * lg), 0.0)
        q_dec = jnp.exp((pos + 1.0) * lg)
        k_dec = jnp.exp((chunk - 1.0 - pos) * lg)
        c_dec = math.exp(chunk * lg)
        q = q_ref[:, h * dk:(h + 1) * dk]
        k = k_ref[:, h * dk:(h + 1) * dk]
        v = v_ref[:, h * dv:(h + 1) * dv]
        s_prev = s_ref[h]
        a = _dot_nt(q, k) * dmask
        o = (jnp.dot(a.astype(BF16), v, preferred_element_type=F32)
             + jnp.dot(q, s_prev.astype(BF16), preferred_element_type=F32) * q_dec)
        kd = (k.astype(F32) * k_dec).T.astype(BF16)
        s_ref[h] = s_prev * c_dec + jnp.dot(kd, v, preferred_element_type=F32)
        cols = slice(h * dv, (h + 1) * dv)
        y_ref[:, cols] = _ret_finish(o, sg_ref[:, cols], gain_ref[:, cols]).astype(y_ref.dtype)

    @pl.when(c == pl.num_programs(1) - 1)
    def _():
        s_out_ref[...] = s_ref[...]


def _ret_prompt(z, gain, *, batch, seq_len, chunk, dk, dv):
    m = z.shape[0]
    nc = seq_len // chunk
    nk = RET_HEADS * dk
    nv = RET_HEADS * dv
    rowblk = lambda b, c: b * nc + c
    return pl.pallas_call(
        functools.partial(_ret_prompt_kernel, chunk=chunk, dk=dk, dv=dv),
        grid=(batch, nc),
        in_specs=[pl.BlockSpec((chunk, nk), lambda b, c: (rowblk(b, c), 0)),
                  pl.BlockSpec((chunk, nk), lambda b, c: (rowblk(b, c), 1)),
                  pl.BlockSpec((chunk, nv), lambda b, c: (rowblk(b, c), (2 * nk) // nv)),
                  pl.BlockSpec((chunk, nv), lambda b, c: (rowblk(b, c), (2 * nk) // nv + 1)),
                  pl.BlockSpec((1, nv), lambda b, c: (0, 0))],
        out_specs=[pl.BlockSpec((chunk, nv), lambda b, c: (rowblk(b, c), 0)),
                   pl.BlockSpec((None, RET_HEADS, dk, dv), lambda b, c: (b, 0, 0, 0))],
        out_shape=[jax.ShapeDtypeStruct((m, nv), BF16), jax.ShapeDtypeStruct((batch, RET_HEADS, dk, dv), F32)],
        scratch_shapes=[pltpu.VMEM((RET_HEADS, dk, dv), F32)],
        compiler_params=_cparams(("arbitrary", "arbitrary")),
        name="ret_prompt",
    )(z, z, z, z, gain)


def _ret_step_kernel(z_ref, s_ref, gain_ref, y_ref, s_out_ref, *, dk, dv):
    nk = RET_HEADS * dk
    nv = RET_HEADS * dv
    row0 = lax.broadcasted_iota(jnp.int32, (LANES, 1), 0) == 0
    for h in range(RET_HEADS):
        gamma = math.exp(_ret_log_gamma(h))
        q = z_ref[:, h * dk:(h + 1) * dk]
        k = z_ref[:, nk + h * dk:nk + (h + 1) * dk]
        v = z_ref[:, 2 * nk + h * dv:2 * nk + (h + 1) * dv]
        sg = z_ref[:, 2 * nk + nv + h * dv:2 * nk + nv + (h + 1) * dv]
        s_prev = s_ref[h]
        q_rows = jnp.broadcast_to(q.astype(F32), (16, dk)).astype(BF16)
        a = jnp.sum(q.astype(F32) * k.astype(F32), axis=-1, keepdims=True)
        o = a * v.astype(F32) + jnp.dot(q_rows, s_prev.astype(BF16), preferred_element_type=F32)[0:1, :] * gamma
        k_rows = jnp.where(row0, jnp.broadcast_to(k.astype(F32), (LANES, dk)), 0.0)
        v_rows = jnp.where(row0, jnp.broadcast_to(v.astype(F32), (LANES, dv)), 0.0)
        s_out_ref[h] = s_prev * gamma + jnp.dot(k_rows.T.astype(BF16), v_rows.astype(BF16),
                                                preferred_element_type=F32)
        cols = slice(h * dv, (h + 1) * dv)
        y_ref[:, cols] = _ret_finish(o, sg, gain_ref[:, cols]).astype(y_ref.dtype)


def _ret_step(z, states, layer, gain, *, dk, dv):
    bd, n = z.shape
    nv = RET_HEADS * dv
    z3 = z.reshape(bd, 1, n)
    blk = pl.BlockSpec((None, RET_HEADS, dk, dv), lambda b: (b, 0, 0, 0))
    y, s_new = pl.pallas_call(
        functools.partial(_ret_step_kernel, dk=dk, dv=dv),
        grid=(bd,),
        in_specs=[pl.BlockSpec((None, 1, n), lambda b: (b, 0, 0)),
                  pl.BlockSpec((None, None, RET_HEADS, dk, dv), lambda b: (layer, b, 0, 0, 0)),
                  pl.BlockSpec((1, nv), lambda b: (0, 0))],
        out_specs=[pl.BlockSpec((None, 1, nv), lambda b: (b, 0, 0)), blk],
        out_shape=[jax.ShapeDtypeStruct((bd, 1, nv), BF16), jax.ShapeDtypeStruct(states.shape[1:], F32)],
        compiler_params=_cparams(("arbitrary",)),
        name="ret_step",
    )(z3, states, gain)
    return y.reshape(bd, nv), s_new


def _fox_decode_kernel(pt_ref, q_ref, qb_ref, kn_ref, vn_ref, lfn_ref, *refs, group):
    del pt_ref
    k_refs = refs[0:group]
    v_refs = refs[group:2 * group]
    lp_refs = refs[2 * group:3 * group]
    o_ref, m_ref, l_ref, w_ref, r_ref, acc_ref = refs[3 * group:]
    c = pl.program_id(1)
    head = lax.broadcasted_iota(jnp.int32, (FOX_HEADS, FOX_DIM), 0)
    chan = lax.broadcasted_iota(jnp.int32, (FOX_HEADS, FOX_DIM), 1)
    own = chan // FOX_HEAD_DIM == head

    def per_head_to_channels(col):
        return jnp.sum(jnp.where(own, jnp.broadcast_to(col, own.shape), 0.0), axis=0, keepdims=True)

    @pl.when(c == 0)
    def _():
        prod = jnp.broadcast_to(q_ref[...] * kn_ref[...], own.shape)
        m_ref[...] = jnp.sum(jnp.where(own, prod, 0.0), axis=-1, keepdims=True)
        l_ref[...] = jnp.ones_like(l_ref)
        w_ref[...] = jnp.ones_like(w_ref)
        r_ref[...] = jnp.zeros_like(r_ref)
        acc_ref[...] = jnp.zeros_like(acc_ref)

    lane = lax.broadcasted_iota(jnp.int32, (1, PAGE_SIZE), 1)
    head_row = lax.broadcasted_iota(jnp.int32, (FOX_HEADS, 1), 0)
    lfn = lfn_ref[...]
    later = r_ref[...]
    scores = []
    for g in range(group):
        lp = lp_refs[g][...]
        suffix = lp
        stride = 1
        while stride < PAGE_SIZE:
            ahead = pltpu.roll(suffix, PAGE_SIZE - stride, axis=1)
            suffix = suffix + jnp.where(lane < PAGE_SIZE - stride, ahead, 0.0)
            stride *= 2
        s = lfn + later + (suffix - lp)
        later = later + jnp.sum(lp, axis=-1, keepdims=True)
        for h in range(FOX_HEADS):
            qk = jnp.sum(k_refs[g][h] * qb_ref[h], axis=0, keepdims=True)
            s = s + jnp.where(head_row == h, jnp.broadcast_to(qk, s.shape), 0.0)
        scores.append(s)
    r_ref[...] = later
    m_prev = m_ref[...]
    m_new = m_prev
    for s in scores:
        m_new = jnp.maximum(m_new, jnp.max(s, axis=-1, keepdims=True))
    alpha = jnp.exp(m_prev - m_new)
    l_new = alpha * l_ref[...]
    probs = []
    for s in scores:
        p = jnp.exp(s - m_new)
        l_new = l_new + jnp.sum(p, axis=-1, keepdims=True)
        probs.append(p)
    for h in range(FOX_HEADS):
        acc = acc_ref[h] * alpha[h:h + 1, :]
        for g in range(group):
            acc = acc + v_refs[g][h] * probs[g][h:h + 1, :]
        acc_ref[h] = acc
    m_ref[...] = m_new
    l_ref[...] = l_new
    w_ref[...] = alpha * w_ref[...]

    @pl.when(c == pl.num_programs(1) - 1)
    def _():
        acc_t = acc_ref[...].reshape(FOX_DIM, PAGE_SIZE).T
        pv = jnp.sum(acc_t, axis=0, keepdims=True)
        o = (pv + per_head_to_channels(w_ref[...]) * vn_ref[...]) / per_head_to_channels(l_ref[...])
        o_ref[...] = o.astype(o_ref.dtype)


def _fox_decode(q, k_new, v_new, logf_new, cache_kt, cache_vt, cache_lpt, page_table, layer, *, group):
    bd, c = q.shape
    n_pages = page_table.shape[1]
    group = min(group, n_pages)
    assert n_pages % group == 0
    pt_flat = page_table.reshape(-1)
    q = q.astype(F32)
    q_lanes = jnp.broadcast_to(q.reshape(bd, FOX_HEADS, FOX_HEAD_DIM, 1), (bd, FOX_HEADS, FOX_HEAD_DIM, PAGE_SIZE))

    def page_spec(g, shape):
        def index(b, s, pt):
            return (layer, pt[b * n_pages + (n_pages - 1 - (s * group + g))]) + (0,) * len(shape)
        return pl.BlockSpec((None, None) + shape, index)

    row = pl.BlockSpec((None, 1, c), lambda b, s, pt: (b, 0, 0))
    slab = (FOX_HEADS, FOX_HEAD_DIM, PAGE_SIZE)
    in_specs = ([row, pl.BlockSpec((None,) + slab, lambda b, s, pt: (b, 0, 0, 0)), row, row,
                 pl.BlockSpec((None, FOX_HEADS, 1), lambda b, s, pt: (b, 0, 0))]
                + [page_spec(g, slab) for g in range(group)]
                + [page_spec(g, slab) for g in range(group)]
                + [page_spec(g, (FOX_HEADS, PAGE_SIZE)) for g in range(group)])
    col = pltpu.VMEM((FOX_HEADS, 1), F32)
    out = pl.pallas_call(
        functools.partial(_fox_decode_kernel, group=group),
        grid_spec=pltpu.PrefetchScalarGridSpec(
            num_scalar_prefetch=1,
            grid=(bd, n_pages // group),
            in_specs=in_specs,
            out_specs=row,
            scratch_shapes=[col, col, col, col, pltpu.VMEM(slab, F32)]),
        out_shape=jax.ShapeDtypeStruct((bd, 1, c), BF16),
        compiler_params=_cparams(("arbitrary", "arbitrary")),
        name="fox_decode",
    )(pt_flat, q.reshape(bd, 1, c), q_lanes, k_new.reshape(bd, 1, c), v_new.reshape(bd, 1, c),
      logf_new.reshape(bd, FOX_HEADS, 1), *([cache_kt] * group), *([cache_vt] * group), *([cache_lpt] * group))
    return out.reshape(bd, c)


def _rope_tables(pos, half):
    inv = 1.0 / (ROPE_BASE ** jnp.linspace(0.0, 1.0, half, dtype=F32))
    ang = pos.astype(F32)[:, None] * inv[None, :]
    return jnp.cos(ang), jnp.sin(ang)


def _prep_weights(w_in_even, b_forget, conv_w, w_out_even, w_in_odd, w_ffn_gate, w_ffn_up, w_ffn_down, dk):
    c = FOX_DIM
    n_even, d, _ = w_in_even.shape
    q, k, v, fl, a, b = jnp.split(w_in_even, [c, 2 * c, 3 * c, 3 * c + FOX_HEADS, 3 * c + FOX_HEADS + c], axis=-1)
    fl_pad = jnp.pad(fl, ((0, 0), (0, 0), (0, LANES - FOX_HEADS)))
    w_even = jnp.concatenate([q * (FOX_HEAD_DIM ** -0.5), k, v, a, b, fl_pad], axis=-1).astype(BF16)
    bf_pad = jnp.pad(b_forget, ((0, 0), (0, LANES - FOX_HEADS)))[:, None, :]
    cw_pad = jnp.pad(conv_w, ((0, 0), (0, CONV_HALO - CONV_WIDTH), (0, 0)))
    nk = RET_HEADS * dk
    w_odd = jnp.concatenate([w_in_odd[:, :, :nk], w_in_odd[:, :, nk:2 * nk] * (dk ** -0.5), w_in_odd[:, :, 2 * nk:]],
                            axis=-1).astype(BF16)
    return dict(w_even=w_even, bf_pad=bf_pad, cw_pad=cw_pad,
                w_out_fox=w_out_even[:, :c].astype(BF16), w_out_conv=w_out_even[:, c:].astype(BF16),
                w_odd=w_odd, wg=w_ffn_gate.astype(BF16), wu=w_ffn_up.astype(BF16), wd=w_ffn_down.astype(BF16))


def _tiles(rows, seq_len):
    if seq_len == 1:
        return dict(tm=rows, tm_ffn=rows, th=1408, tm_odd=rows, tn_odd=1024)
    return dict(tm=min(512, seq_len), tm_ffn=min(512, seq_len), th=1408, tm_odd=min(1024, seq_len), tn_odd=1024)


def _trunk(x, mod, pos, P, W, *, seq_len, even_mixer, odd_mixer):
    depth = mod.shape[0]
    d = x.shape[1]
    t = _tiles(x.shape[0], seq_len)
    tm = t['tm']
    cos, sin = _rope_tables(pos, LANES)
    nk = P['nk']
    ks, vs, lfs, cbs, rs = [], [], [], [], []
    for l in range(depth):
        sh1, sc1, g1, sh2, sc2, g2 = [mod[l][..., s * d:(s + 1) * d] for s in range(6)]
        gm = W['norm_mix_g'][l][None, :]
        if l % 2 == 0:
            e = l // 2
            out = _even_proj(x, gm, sc1, sh1, P['w_even'][e], P['bf_pad'][e], tm=tm, tq=min(FOX_TQ, tm),
                             seq_len=seq_len, with_cum=even_mixer.needs_cum)
            q, k, v, logf, u = out[:5]
            o_fox, cv, new_buf = even_mixer(e, q, k, v, logf, u, out[5:])
            ks.append(k); vs.append(v); lfs.append(logf); cbs.append(new_buf)
            x = _proj_res([o_fox, cv], [P['w_out_fox'][e], P['w_out_conv'][e]], x, g1, tm=tm)
        else:
            o = l // 2
            n_odd = P['w_odd'].shape[-1]
            z = _odd_proj(x, gm, sc1, sh1, P['w_odd'][o], cos, sin, tm=t['tm_odd'], tn=t['tn_odd'],
                          n_rope=2 * nk // t['tn_odd'], n_plain=(n_odd - 2 * nk) // 2 // t['tn_odd'])
            y, s_new = odd_mixer(o, z)
            rs.append(s_new)
            x = _proj_res([y], [P['w_out_odd'][o]], x, g1, tm=tm)
        x = _ffn(x, W['norm_ffn_g'][l][None, :], sc2, sh2, g2, P['wg'][l], P['wu'][l], P['wd'][l],
                 W['final_norm_g'][None, :], tm=t['tm_ffn'], th=t['th'], final=(l == depth - 1))
    return x, (ks, vs, lfs, cbs, rs)


class _PromptEven:
    needs_cum = True

    def __init__(self, batch, seq_len, W, P):
        self.batch, self.seq_len, self.W, self.P = batch, seq_len, W, P

    def __call__(self, e, q, k, v, logf, u, extra):
        kb, vt, cum, cumt = extra
        W, P = self.W, self.P
        o_fox = _fox_prompt(q, kb, vt, cum, cumt, batch=self.batch, seq_len=self.seq_len, tq=FOX_TQ)
        cv = _conv_prompt(u, P['cw_pad'][e], W['conv_b'][e][None, :], W['conv_norm_g'][e][None, :],
                          W['conv_norm_b'][e][None, :], batch=self.batch, seq_len=self.seq_len, tl=256)
        c = u.shape[1]
        new_buf = u.reshape(self.batch, self.seq_len, c)[:, self.seq_len - (CONV_WIDTH - 1):]
        return o_fox, cv, new_buf


class _SampleEven:
    needs_cum = False

    def __init__(self, W, P, cache_k, cache_v, cache_lp, state_conv, page_table):
        self.W, self.P = W, P
        self.cache_k, self.cache_v, self.cache_lp = cache_k, cache_v, cache_lp
        self.state_conv, self.page_table = state_conv, page_table

    def __call__(self, e, q, k, v, logf, u, extra):
        W, P = self.W, self.P
        o_fox = _fox_decode(q, k, v, logf, self.cache_k, self.cache_v, self.cache_lp, self.page_table, e, group=16)
        st = self.state_conv[e]
        cv = _conv_step(jnp.transpose(st, (1, 0, 2)), u, P['cw_pad'][e], W['conv_b'][e][None, :],
                        W['conv_norm_g'][e][None, :], W['conv_norm_b'][e][None, :])
        new_buf = jnp.concatenate([st[:, 1:], u[:, None, :]], axis=1)
        return o_fox, cv, new_buf


def kernel(x_prompt, x_sample, cache_k, cache_v, cache_logf, state_conv, state_ret, page_table, c_prompt, c_sample,
           ada_w, ada_b, norm_mix_g, norm_ffn_g, w_in_even, b_forget, conv_w, conv_b, conv_norm_g, conv_norm_b,
           w_out_even, w_in_odd, ret_norm_g, w_out_odd, w_ffn_gate, w_ffn_up, w_ffn_down, final_norm_g):
    bp, lp, d = x_prompt.shape
    bs, ls, _ = x_sample.shape
    assert ls == 1, "the decode path handles one new token per sequence"
    depth = ada_w.shape[0]
    n_even, n_phys, page, heads, hd = cache_k.shape
    assert (page, heads, hd) == (PAGE_SIZE, FOX_HEADS, FOX_HEAD_DIM)
    dk, dv = state_ret.shape[-2:]
    past_len = page_table.shape[1] * PAGE_SIZE

    W = dict(norm_mix_g=norm_mix_g, norm_ffn_g=norm_ffn_g, conv_b=conv_b, conv_norm_g=conv_norm_g,
             conv_norm_b=conv_norm_b, final_norm_g=final_norm_g)
    P = _prep_weights(w_in_even, b_forget, conv_w, w_out_even, w_in_odd, w_ffn_gate, w_ffn_up, w_ffn_down, dk)
    P['w_out_odd'] = w_out_odd.astype(BF16)
    P['nk'] = RET_HEADS * dk
    gain = ret_norm_g[:, None, :]

    mod = _adaln(jnp.concatenate([c_prompt, c_sample], axis=0), ada_w, ada_b)
    mod_p = mod[:, :bp].reshape(depth, bp, 1, 6 * d)
    mod_s = mod[:, bp:].reshape(depth, 1, bs, 6 * d)

    def odd_prompt(o, z):
        return _ret_prompt(z, gain[o], batch=bp, seq_len=lp, chunk=min(256, lp), dk=dk, dv=dv)

    y_p, (k_p, v_p, lf_p, cb_p, r_p) = _trunk(
        x_prompt.reshape(bp * lp, d), mod_p, jnp.tile(jnp.arange(lp), bp), P, W, seq_len=lp,
        even_mixer=_PromptEven(bp, lp, W, P), odd_mixer=odd_prompt)

    ckt = jnp.transpose(cache_k, (0, 1, 3, 4, 2))
    cvt = jnp.transpose(cache_v, (0, 1, 3, 4, 2))
    clp = jnp.transpose(cache_logf, (0, 1, 3, 2))

    def odd_sample(o, z):
        return _ret_step(z, state_ret, o, gain[o], dk=dk, dv=dv)

    y_s, (k_s, v_s, lf_s, cb_s, r_s) = _trunk(
        x_sample.reshape(bs, d), mod_s, jnp.full((bs,), past_len, jnp.int32), P, W, seq_len=1,
        even_mixer=_SampleEven(W, P, ckt, cvt, clp, state_conv, page_table), odd_mixer=odd_sample)

    hshape = (FOX_HEADS, FOX_HEAD_DIM)

    def seq_last_to_heads(ts):
        return jnp.transpose(jnp.stack(ts).reshape((n_even, bp) + hshape + (lp,)), (0, 1, 4, 2, 3))

    return (y_p.reshape(bp, lp, d), y_s.reshape(bs, ls, d),
            seq_last_to_heads(k_p), seq_last_to_heads(v_p),
            jnp.transpose(jnp.stack(lf_p), (0, 1, 3, 2)), jnp.stack(cb_p), jnp.stack(r_p),
            jnp.stack(k_s).reshape((n_even, bs, ls) + hshape), jnp.stack(v_s).reshape((n_even, bs, ls) + hshape),
            jnp.stack(lf_s).reshape(n_even, bs, ls, FOX_HEADS), jnp.stack(cb_s),
            jnp.stack(r_s).astype(state_ret.dtype))
```

```python
import functools
import math

import jax
import jax.numpy as jnp
import numpy as np
from jax import lax
from jax.experimental import pallas as pl
from jax.experimental.pallas import tpu as pltpu

F32 = jnp.float32
BF16 = jnp.bfloat16

EPS = 1e-6
ROPE_BASE = 10000.0
FOX_HEADS = 8
FOX_HEAD_DIM = 64
FOX_DIM = FOX_HEADS * FOX_HEAD_DIM
CONV_GROUPS = 8
CONV_WIDTH = 31
RET_HEADS = 4
PAGE_SIZE = 128
LANES = 128
SUBLANES = 8
CONV_HALO = 32
VMEM_LIMIT = 56 * 1024 * 1024
NEG_BIG = -1e30
LOG2E = math.log2(math.e)
FOX_TQ = 256


def _cparams(sem):
    return pltpu.CompilerParams(dimension_semantics=sem, vmem_limit_bytes=VMEM_LIMIT)


def _silu(x):
    return x * jax.nn.sigmoid(x)


def _log_sigmoid(x):
    return jnp.minimum(x, 0.0) - jnp.log1p(jnp.exp(-jnp.abs(x)))


def _norm_mod(x, g, sc, sh):
    ms = jnp.mean(x * x, axis=-1, keepdims=True)
    y = x * lax.rsqrt(ms + EPS) * g
    return y * (1.0 + sc) + sh


def _split3(x):
    hi = x.astype(BF16)
    r1 = x - hi.astype(F32)
    mid = r1.astype(BF16)
    lo = (r1 - mid.astype(F32)).astype(BF16)
    return hi, mid, lo


def _dot3(x, w_bf16):
    hi, mid, lo = _split3(x)
    d = functools.partial(jnp.dot, preferred_element_type=F32)
    return d(hi, w_bf16) + d(mid, w_bf16) + d(lo, w_bf16)


def _dot_nt(a, b):
    return lax.dot_general(a, b, (((1,), (1,)), ((), ())), preferred_element_type=F32)


def _mod_spec(rows, d, tiles_per_mod):
    return pl.BlockSpec((None, rows, d), lambda i, *_: (i // tiles_per_mod, 0, 0))


def _adaln_kernel(c_ref, w_ref, b_ref, o_ref):
    cm = _silu(c_ref[...]).astype(BF16)
    o_ref[...] = jnp.dot(cm, w_ref[...].astype(BF16), preferred_element_type=F32) + b_ref[...]


def _adaln(c_all, ada_w, ada_b, tn=1536):
    depth, d, n = ada_w.shape
    r = c_all.shape[0]
    return pl.pallas_call(
        _adaln_kernel,
        grid=(depth, n // tn),
        in_specs=[pl.BlockSpec((r, d), lambda l, j: (0, 0)),
                  pl.BlockSpec((None, d, tn), lambda l, j: (l, 0, j)),
                  pl.BlockSpec((None, 1, tn), lambda l, j: (l, 0, j))],
        out_specs=pl.BlockSpec((None, r, tn), lambda l, j: (l, 0, j)),
        out_shape=jax.ShapeDtypeStruct((depth, r, n), F32),
        compiler_params=_cparams(("arbitrary", "arbitrary")),
        name="adaln",
    )(c_all, ada_w, ada_b.reshape(depth, 1, n))


def _even_proj_kernel(x_ref, g_ref, sc_ref, sh_ref, wqkv_ref, wab_ref, wf_ref, bf_ref, *refs,
                      tm, tq, tiles_per_seq, with_cum):
    if with_cum:
        q_ref, k_ref, v_ref, lf_ref, u_ref, kb_ref, vt_ref, cum_ref, cumt_ref, carry_ref = refs
    else:
        q_ref, k_ref, v_ref, lf_ref, u_ref = refs
    hm = _norm_mod(x_ref[...], g_ref[...], sc_ref[...], sh_ref[...]).astype(BF16)
    c = FOX_DIM
    qkv = jnp.dot(hm, wqkv_ref[...], preferred_element_type=F32)
    q_ref[...] = qkv[:, 0:c].astype(BF16)
    k = qkv[:, c:2 * c]
    v = qkv[:, 2 * c:3 * c]
    ab = jnp.dot(hm, wab_ref[...], preferred_element_type=F32)
    u_ref[...] = ab[:, 0:c] * jax.nn.sigmoid(ab[:, c:2 * c])
    fl = jnp.dot(hm, wf_ref[...], preferred_element_type=F32)
    logf = _log_sigmoid(fl + bf_ref[...])
    if not with_cum:
        k_ref[...] = k
        v_ref[...] = v
        lf_ref[...] = logf[:, 0:FOX_HEADS]
    else:
        v_t = v.T
        k_ref[...] = k.T
        v_ref[...] = v_t
        lf_ref[...] = logf.T[0:FOX_HEADS, :]
        kb_ref[...] = (k * LOG2E).astype(BF16)
        for r in range(tm // tq):
            vt_ref[r] = v_t[:, r * tq:(r + 1) * tq].astype(BF16)
        i = pl.program_id(0)

        @pl.when(i % tiles_per_seq == 0)
        def _():
            carry_ref[...] = jnp.zeros_like(carry_ref)

        row = lax.broadcasted_iota(jnp.int32, (tm, tm), 0)
        col = lax.broadcasted_iota(jnp.int32, (tm, tm), 1)
        tri = (col <= row).astype(BF16)
        cum = _dot3_left(tri, logf) + carry_ref[...]
        carry_ref[...] = cum[tm - 1:tm, :]
        cum = cum * LOG2E
        cum_ref[...] = cum[:, 0:FOX_HEADS]
        cum_t = cum.T
        for r in range(tm // tq):
            cumt_ref[r] = cum_t[0:FOX_HEADS, r * tq:(r + 1) * tq]


def _dot3_left(w_bf16, x):
    hi, mid, lo = _split3(x)
    d = functools.partial(jnp.dot, preferred_element_type=F32)
    return d(w_bf16, hi) + d(w_bf16, mid) + d(w_bf16, lo)


def _even_proj(x, g, sc, sh, weights, bf_pad, *, tm, tq, seq_len, with_cum):
    m, d = x.shape
    nmod, rows, _ = sc.shape
    tiles_per_mod = (m // nmod) // tm
    c = FOX_DIM
    row_spec = lambda width: pl.BlockSpec((tm, width), lambda i: (i, 0))
    tiles_per_seq = max(seq_len // tm, 1)
    if with_cum:
        nseq = m // seq_len
        seq_major = lambda width: (jax.ShapeDtypeStruct((nseq, width, seq_len), F32),
                                   pl.BlockSpec((None, width, tm), lambda i: (i // tiles_per_seq, 0, i % tiles_per_seq)))
        kvl = [seq_major(c), seq_major(c), seq_major(FOX_HEADS)]
    else:
        kvl = [(jax.ShapeDtypeStruct((m, w_), F32), row_spec(w_)) for w_ in (c, c, FOX_HEADS)]
    out_shape = [jax.ShapeDtypeStruct((m, c), BF16)] + [s for s, _ in kvl] + [jax.ShapeDtypeStruct((m, c), F32)]
    out_specs = [row_spec(c)] + [s for _, s in kvl] + [row_spec(c)]
    scratch = []
    if with_cum:
        out_shape += [jax.ShapeDtypeStruct((m, c), BF16), jax.ShapeDtypeStruct((m // tq, c, tq), BF16),
                      jax.ShapeDtypeStruct((m, FOX_HEADS), F32), jax.ShapeDtypeStruct((m // tq, FOX_HEADS, tq), F32)]
        out_specs += [row_spec(c), pl.BlockSpec((tm // tq, c, tq), lambda i: (i, 0, 0)),
                      row_spec(FOX_HEADS), pl.BlockSpec((tm // tq, FOX_HEADS, tq), lambda i: (i, 0, 0))]
        scratch = [pltpu.VMEM((1, LANES), F32)]
    return pl.pallas_call(
        functools.partial(_even_proj_kernel, tm=tm, tq=tq, tiles_per_seq=tiles_per_seq, with_cum=with_cum),
        grid=(m // tm,),
        in_specs=[row_spec(d), pl.BlockSpec((1, d), lambda i: (0, 0)),
                  _mod_spec(rows, d, tiles_per_mod), _mod_spec(rows, d, tiles_per_mod),
                  *[pl.BlockSpec(w.shape, lambda i: (0, 0)) for w in weights],
                  pl.BlockSpec((1, LANES), lambda i: (0, 0))],
        out_specs=out_specs,
        out_shape=out_shape,
        scratch_shapes=scratch,
        compiler_params=_cparams(("arbitrary",)),
        name="even_proj",
    )(x, g, sc, sh, *weights, bf_pad)


def _fox_prompt_kernel(q_ref, k_ref, vt_ref, ck_ref, cq_ref, o_ref, qm_ref, m_ref, acc_ref, *, tq):
    i = pl.program_id(1)
    hd = FOX_HEAD_DIM
    low = lax.broadcasted_iota(jnp.int32, (1, LANES), 1) < hd
    for pair in range(FOX_HEADS // 2):
        qp = q_ref[:, pair * LANES:(pair + 1) * LANES].astype(F32)
        qm_ref[2 * pair] = jnp.where(low, qp, 0.0).astype(BF16)
        qm_ref[2 * pair + 1] = jnp.where(low, 0.0, qp).astype(BF16)
    m_ref[...] = jnp.full_like(m_ref, NEG_BIG)
    acc_ref[...] = jnp.zeros_like(acc_ref)
    key = lax.broadcasted_iota(jnp.int32, (tq, tq), 0)
    qry = lax.broadcasted_iota(jnp.int32, (tq, tq), 1)
    causal = key <= qry
    ones = jnp.ones((acc_ref.shape[1] - hd, tq), BF16)

    def step(j, masked):
        start = pl.multiple_of(j * tq, tq)
        cq_all = cq_ref[i]
        qk = []
        for h in range(FOX_HEADS):
            pair = h // 2
            kj = k_ref[pl.ds(start, tq), pair * LANES:(pair + 1) * LANES]
            qk.append(_dot_nt(kj, qm_ref[h]))
        for h in range(FOX_HEADS):
            t = qk[h] - ck_ref[pl.ds(start, tq), h:h + 1]
            if masked:
                t = jnp.where(causal, t, NEG_BIG)
            cq = cq_all[h:h + 1, :]
            m_prev = m_ref[h:h + 1, :]
            m_new = jnp.maximum(m_prev, jnp.max(t, axis=0, keepdims=True) + cq)
            p = jnp.exp2(t + (cq - m_new)).astype(BF16)
            alpha = jnp.exp2(m_prev - m_new)
            v_ones = jnp.concatenate([vt_ref[j][h * hd:(h + 1) * hd, :], ones], axis=0)
            acc_ref[h] = alpha * acc_ref[h] + jnp.dot(v_ones, p, preferred_element_type=F32)
            m_ref[h:h + 1, :] = m_new

    def body(j, carry):
        step(j, False)
        return carry

    lax.fori_loop(0, i, body, 0)
    step(i, True)
    for pair in range(FOX_HEADS // 2):
        halves = []
        for h in (2 * pair, 2 * pair + 1):
            halves.append(acc_ref[h, 0:hd, :] * (1.0 / acc_ref[h, hd:hd + 1, :]))
        o_ref[:, pair * LANES:(pair + 1) * LANES] = jnp.concatenate(halves, axis=0).T.astype(o_ref.dtype)


def _fox_prompt(q, kb, vt, cum, cumt, *, batch, seq_len, tq):
    m, c = q.shape
    nq = seq_len // tq
    return pl.pallas_call(
        functools.partial(_fox_prompt_kernel, tq=tq),
        grid=(batch, nq),
        in_specs=[pl.BlockSpec((tq, c), lambda b, i: (b * nq + i, 0)),
                  pl.BlockSpec((seq_len, c), lambda b, i: (b, 0)),
                  pl.BlockSpec((nq, c, tq), lambda b, i: (b, 0, 0)),
                  pl.BlockSpec((seq_len, FOX_HEADS), lambda b, i: (b, 0)),
                  pl.BlockSpec((nq, FOX_HEADS, tq), lambda b, i: (b, 0, 0))],
        out_specs=pl.BlockSpec((tq, c), lambda b, i: (b * nq + i, 0)),
        out_shape=jax.ShapeDtypeStruct((m, c), BF16),
        scratch_shapes=[pltpu.VMEM((FOX_HEADS, tq, LANES), BF16), pltpu.VMEM((FOX_HEADS, tq), F32),
                        pltpu.VMEM((FOX_HEADS, FOX_HEAD_DIM + 16, tq), F32)],
        compiler_params=_cparams(("arbitrary", "arbitrary")),
        name="fox_prompt",
    )(q, kb, vt, cum, cumt)


def _group_matrices(channels, groups):
    gsz = channels // groups
    ch = lax.broadcasted_iota(jnp.int32, (channels, LANES), 0)
    gr = lax.broadcasted_iota(jnp.int32, (channels, LANES), 1)
    gather = (ch // gsz == gr).astype(BF16)
    gr_t = lax.broadcasted_iota(jnp.int32, (LANES, channels), 0)
    ch_t = lax.broadcasted_iota(jnp.int32, (LANES, channels), 1)
    spread = (ch_t // gsz == gr_t).astype(BF16)
    return gather, spread, 1.0 / gsz


def _group_norm_rows(y, groups):
    gather, spread, inv = _group_matrices(y.shape[-1], groups)
    mu = _dot3(y, gather) * inv
    d = y - _dot3(mu, spread)
    var = _dot3(d * d, gather) * inv
    return d * _dot3(lax.rsqrt(var + EPS), spread)


def _conv_finish(acc, cb, gn_g, gn_b):
    cv = _group_norm_rows(acc + cb, CONV_GROUPS) * gn_g + gn_b
    return _silu(cv)


def _conv_prompt_kernel(u_ref, halo_ref, cw_ref, cb_ref, gg_ref, gb_ref, o_ref, xp_ref, sh_ref, y_ref, *, tl):
    i = pl.program_id(1)
    halo = halo_ref[...]
    xp_ref[0:CONV_HALO, :] = jnp.where(i == 0, jnp.zeros_like(halo), halo)
    xp_ref[CONV_HALO:CONV_HALO + tl, :] = u_ref[...]
    span = sh_ref.shape[1]
    for r in range(1, SUBLANES):
        sh_ref[r - 1] = xp_ref[r:r + span, :]
    base = CONV_HALO - (CONV_WIDTH - 1)
    rows = 64
    for r0 in range(0, tl, rows):
        for c0 in range(0, u_ref.shape[1], LANES):
            acc = jnp.zeros((rows, LANES), F32)
            for w in range(CONV_WIDTH):
                shift = (base + w) % SUBLANES
                start = r0 + base + w - shift
                if shift == 0:
                    tap = xp_ref[start:start + rows, c0:c0 + LANES]
                else:
                    tap = sh_ref[shift - 1, start:start + rows, c0:c0 + LANES]
                acc = acc + tap * cw_ref[w:w + 1, c0:c0 + LANES]
            y_ref[r0:r0 + rows, c0:c0 + LANES] = acc
    y = _conv_finish(y_ref[...], cb_ref[...], gg_ref[...], gb_ref[...])
    o_ref[...] = y.astype(o_ref.dtype)


def _conv_prompt(u, cw, cb, gn_g, gn_b, *, batch, seq_len, tl):
    m, c = u.shape
    nl = seq_len // tl
    hb = tl // CONV_HALO
    vec = pl.BlockSpec((1, c), lambda b, i: (0, 0))
    return pl.pallas_call(
        functools.partial(_conv_prompt_kernel, tl=tl),
        grid=(batch, nl),
        in_specs=[pl.BlockSpec((tl, c), lambda b, i: (b * nl + i, 0)),
                  pl.BlockSpec((CONV_HALO, c), lambda b, i: (jnp.maximum((b * nl + i) * hb - 1, 0), 0)),
                  pl.BlockSpec((CONV_HALO, c), lambda b, i: (0, 0)), vec, vec, vec],
        out_specs=pl.BlockSpec((tl, c), lambda b, i: (b * nl + i, 0)),
        out_shape=jax.ShapeDtypeStruct((m, c), BF16),
        scratch_shapes=[pltpu.VMEM((CONV_HALO + tl, c), F32),
                        pltpu.VMEM((SUBLANES - 1, CONV_HALO - SUBLANES + tl, c), F32), pltpu.VMEM((tl, c), F32)],
        compiler_params=_cparams(("arbitrary", "arbitrary")),
        name="conv_prompt",
    )(u, u, cw, cb, gn_g, gn_b)


def _conv_step_kernel(st_ref, u_ref, cw_ref, cb_ref, gg_ref, gb_ref, o_ref):
    acc = u_ref[...] * cw_ref[CONV_WIDTH - 1:CONV_WIDTH, :]
    for w in range(CONV_WIDTH - 1):
        acc = acc + st_ref[w] * cw_ref[w:w + 1, :]
    o_ref[...] = _conv_finish(acc, cb_ref[...], gg_ref[...], gb_ref[...]).astype(o_ref.dtype)


def _conv_step(state_t, u, cw, cb, gn_g, gn_b):
    bd, c = u.shape
    full = lambda shape: pl.BlockSpec(shape, lambda i: (0,) * len(shape))
    return pl.pallas_call(
        _conv_step_kernel,
        grid=(1,),
        in_specs=[full(state_t.shape), full((bd, c)), full(cw.shape), full((1, c)), full((1, c)), full((1, c))],
        out_specs=full((bd, c)),
        out_shape=jax.ShapeDtypeStruct((bd, c), BF16),
        compiler_params=_cparams(("arbitrary",)),
        name="conv_step",
    )(state_t, u, cw, cb, gn_g, gn_b)


def _proj_res_kernel(*refs, n_in):
    a_refs = refs[:n_in]
    w_refs = refs[n_in:2 * n_in]
    x_ref, gate_ref, o_ref = refs[2 * n_in:]
    acc = jnp.dot(a_refs[0][...], w_refs[0][...], preferred_element_type=F32)
    for a_ref, w_ref in zip(a_refs[1:], w_refs[1:]):
        acc = acc + jnp.dot(a_ref[...], w_ref[...], preferred_element_type=F32)
    o_ref[...] = x_ref[...] + gate_ref[...] * acc


def _proj_res(acts, weights, x, gate, *, tm):
    m, d = x.shape
    nmod, rows, _ = gate.shape
    tiles_per_mod = (m // nmod) // tm
    n_in = len(acts)
    in_specs = ([pl.BlockSpec((tm, a.shape[1]), lambda i: (i, 0)) for a in acts]
                + [pl.BlockSpec(w.shape, lambda i: (0, 0)) for w in weights]
                + [pl.BlockSpec((tm, d), lambda i: (i, 0)), _mod_spec(rows, d, tiles_per_mod)])
    return pl.pallas_call(
        functools.partial(_proj_res_kernel, n_in=n_in),
        grid=(m // tm,),
        in_specs=in_specs,
        out_specs=pl.BlockSpec((tm, d), lambda i: (i, 0)),
        out_shape=jax.ShapeDtypeStruct((m, d), F32),
        compiler_params=_cparams(("arbitrary",)),
        name="proj_res",
    )(*acts, *weights, x, gate)


def _ffn_kernel(x_ref, g_ref, sc_ref, sh_ref, gate_ref, wg_ref, wu_ref, wd_ref, fg_ref, o_ref, hf_ref, acc_ref,
                *, final):
    j = pl.program_id(1)

    @pl.when(j == 0)
    def _():
        hf_ref[...] = _norm_mod(x_ref[...], g_ref[...], sc_ref[...], sh_ref[...]).astype(BF16)
        acc_ref[...] = jnp.zeros_like(acc_ref)

    hf = hf_ref[...]
    gt = jnp.dot(hf, wg_ref[...], preferred_element_type=F32)
    up = jnp.dot(hf, wu_ref[...], preferred_element_type=F32)
    acc_ref[...] += jnp.dot((_silu(gt) * up).astype(BF16), wd_ref[...], preferred_element_type=F32)

    @pl.when(j == pl.num_programs(1) - 1)
    def _():
        y = x_ref[...] + gate_ref[...] * acc_ref[...]
        if final:
            ms = jnp.mean(y * y, axis=-1, keepdims=True)
            y = y * lax.rsqrt(ms + EPS) * fg_ref[...]
        o_ref[...] = y


def _ffn(x, g, sc, sh, gate, wg, wu, wd, fg, *, tm, th, final):
    m, d = x.shape
    nmod, rows, _ = sc.shape
    hdim = wg.shape[1]
    tiles_per_mod = (m // nmod) // tm
    vec = pl.BlockSpec((1, d), lambda i, j: (0, 0))
    mod = _mod_spec(rows, d, tiles_per_mod)
    return pl.pallas_call(
        functools.partial(_ffn_kernel, final=final),
        grid=(m // tm, hdim // th),
        in_specs=[pl.BlockSpec((tm, d), lambda i, j: (i, 0)), vec, mod, mod, mod,
                  pl.BlockSpec((d, th), lambda i, j: (0, j)), pl.BlockSpec((d, th), lambda i, j: (0, j)),
                  pl.BlockSpec((th, d), lambda i, j: (j, 0)), vec],
        out_specs=pl.BlockSpec((tm, d), lambda i, j: (i, 0)),
        out_shape=jax.ShapeDtypeStruct((m, d), F32),
        scratch_shapes=[pltpu.VMEM((tm, d), BF16), pltpu.VMEM((tm, d), F32)],
        compiler_params=_cparams(("arbitrary", "arbitrary")),
        name="ffn",
    )(x, g, sc, sh, gate, wg, wu, wd, fg)


def _odd_proj_kernel(x_ref, g_ref, sc_ref, sh_ref, w_ref, cos_ref, sin_ref, o_ref, hm_ref, *, epilogue):
    j = pl.program_id(1)

    @pl.when(j == 0)
    def _():
        hm_ref[...] = _norm_mod(x_ref[...], g_ref[...], sc_ref[...], sh_ref[...]).astype(BF16)

    z = jnp.dot(hm_ref[...], w_ref[...], preferred_element_type=F32)
    if epilogue == 'rope':
        cos = cos_ref[...]
        sin = sin_ref[...]
        for c0 in range(0, z.shape[1], 2 * LANES):
            x1 = z[:, c0:c0 + LANES]
            x2 = z[:, c0 + LANES:c0 + 2 * LANES]
            o_ref[:, c0:c0 + LANES] = (x1 * cos - x2 * sin).astype(o_ref.dtype)
            o_ref[:, c0 + LANES:c0 + 2 * LANES] = (x1 * sin + x2 * cos).astype(o_ref.dtype)
    elif epilogue == 'silu':
        o_ref[...] = _silu(z).astype(o_ref.dtype)
    else:
        o_ref[...] = z.astype(o_ref.dtype)


def _odd_proj(x, g, sc, sh, w, cos, sin, *, tm, tn, epilogue):
    m, d = x.shape
    nmod, rows, _ = sc.shape
    n = w.shape[1]
    tiles_per_mod = (m // nmod) // tm
    pos_tiles = cos.shape[0] // tm
    mod = _mod_spec(rows, d, tiles_per_mod)
    return pl.pallas_call(
        functools.partial(_odd_proj_kernel, epilogue=epilogue),
        grid=(m // tm, n // tn),
        in_specs=[pl.BlockSpec((tm, d), lambda i, j: (i, 0)), pl.BlockSpec((1, d), lambda i, j: (0, 0)), mod, mod,
                  pl.BlockSpec((d, tn), lambda i, j: (0, j)),
                  pl.BlockSpec((tm, LANES), lambda i, j: (i % pos_tiles, 0)),
                  pl.BlockSpec((tm, LANES), lambda i, j: (i % pos_tiles, 0))],
        out_specs=pl.BlockSpec((tm, tn), lambda i, j: (i, j)),
        out_shape=jax.ShapeDtypeStruct((m, n), BF16),
        scratch_shapes=[pltpu.VMEM((tm, d), BF16)],
        compiler_params=_cparams(("arbitrary", "arbitrary")),
        name="odd_proj",
    )(x, g, sc, sh, w, cos, sin)


def _ret_log_gamma(h):
    return float(np.log(np.float64(1.0) - np.float64(2.0) ** (-5.0 - h)))


def _ret_finish(o, sg, gain):
    mu = jnp.mean(o, axis=-1, keepdims=True)
    d = o - mu
    var = jnp.mean(d * d, axis=-1, keepdims=True)
    r = d * lax.rsqrt(var + EPS) * gain
    return sg.astype(F32) * r


def _ret_prompt_kernel(q_ref, k_ref, v_ref, sg_ref, gain_ref, y_ref, s_out_ref, s_ref, *, chunk, dk, dv):
    c = pl.program_id(1)

    @pl.when(c == 0)
    def _():
        s_ref[...] = jnp.zeros_like(s_ref)

    ri = lax.broadcasted_iota(jnp.int32, (chunk, chunk), 0)
    ci = lax.broadcasted_iota(jnp.int32, (chunk, chunk), 1)
    diff = (ri - ci).astype(F32)
    pos = lax.broadcasted_iota(jnp.int32, (chunk, 1), 0).astype(F32)
    for h in range(RET_HEADS):
        lg = _ret_log_gamma(h)
        dmask = jnp.where(ri >= ci, jnp.exp(jnp.maximum(diff, 0.0) * lg), 0.0)
        q_dec = jnp.exp((pos + 1.0) * lg)
        k_dec = jnp.exp((chunk - 1.0 - pos) * lg)
        c_dec = math.exp(chunk * lg)
        q = q_ref[:, h * dk:(h + 1) * dk]
        k = k_ref[:, h * dk:(h + 1) * dk]
        v = v_ref[:, h * dv:(h + 1) * dv]
        s_prev = s_ref[h]
        a = _dot_nt(q, k) * dmask
        o = (jnp.dot(a.astype(BF16), v, preferred_element_type=F32)
             + jnp.dot(q, s_prev.astype(BF16), preferred_element_type=F32) * q_dec)
        kd = (k.astype(F32) * k_dec).T.astype(BF16)
        s_ref[h] = s_prev * c_dec + jnp.dot(kd, v, preferred_element_type=F32)
        cols = slice(h * dv, (h + 1) * dv)
        y_ref[:, cols] = _ret_finish(o, sg_ref[:, cols], gain_ref[:, cols]).astype(y_ref.dtype)

    @pl.when(c == pl.num_programs(1) - 1)
    def _():
        s_out_ref[...] = s_ref[...]


def _ret_prompt(zqk, zv, zg, gain, *, batch, seq_len, chunk, dk, dv):
    m = zqk.shape[0]
    nc = seq_len // chunk
    nk = RET_HEADS * dk
    nv = RET_HEADS * dv
    rowblk = lambda b, c: b * nc + c
    return pl.pallas_call(
        functools.partial(_ret_prompt_kernel, chunk=chunk, dk=dk, dv=dv),
        grid=(batch, nc),
        in_specs=[pl.BlockSpec((chunk, nk), lambda b, c: (rowblk(b, c), 0)),
                  pl.BlockSpec((chunk, nk), lambda b, c: (rowblk(b, c), 1)),
                  pl.BlockSpec((chunk, nv), lambda b, c: (rowblk(b, c), 0)),
                  pl.BlockSpec((chunk, nv), lambda b, c: (rowblk(b, c), 0)),
                  pl.BlockSpec((1, nv), lambda b, c: (0, 0))],
        out_specs=[pl.BlockSpec((chunk, nv), lambda b, c: (rowblk(b, c), 0)),
                   pl.BlockSpec((None, RET_HEADS, dk, dv), lambda b, c: (b, 0, 0, 0))],
        out_shape=[jax.ShapeDtypeStruct((m, nv), BF16), jax.ShapeDtypeStruct((batch, RET_HEADS, dk, dv), F32)],
        scratch_shapes=[pltpu.VMEM((RET_HEADS, dk, dv), F32)],
        compiler_params=_cparams(("arbitrary", "arbitrary")),
        name="ret_prompt",
    )(zqk, zqk, zv, zg, gain)


def _ret_step_kernel(qk_ref, v_ref, sg_ref, s_ref, gain_ref, y_ref, s_out_ref, *, dk, dv):
    nk = RET_HEADS * dk
    row0 = lax.broadcasted_iota(jnp.int32, (LANES, 1), 0) == 0
    for h in range(RET_HEADS):
        gamma = math.exp(_ret_log_gamma(h))
        q = qk_ref[:, h * dk:(h + 1) * dk]
        k = qk_ref[:, nk + h * dk:nk + (h + 1) * dk]
        v = v_ref[:, h * dv:(h + 1) * dv]
        sg = sg_ref[:, h * dv:(h + 1) * dv]
        s_prev = s_ref[h]
        q_rows = jnp.broadcast_to(q.astype(F32), (16, dk)).astype(BF16)
        a = jnp.sum(q.astype(F32) * k.astype(F32), axis=-1, keepdims=True)
        o = a * v.astype(F32) + jnp.dot(q_rows, s_prev.astype(BF16), preferred_element_type=F32)[0:1, :] * gamma
        k_rows = jnp.where(row0, jnp.broadcast_to(k.astype(F32), (LANES, dk)), 0.0)
        v_rows = jnp.where(row0, jnp.broadcast_to(v.astype(F32), (LANES, dv)), 0.0)
        s_out_ref[h] = s_prev * gamma + jnp.dot(k_rows.T.astype(BF16), v_rows.astype(BF16),
                                                preferred_element_type=F32)
        cols = slice(h * dv, (h + 1) * dv)
        y_ref[:, cols] = _ret_finish(o, sg, gain_ref[:, cols]).astype(y_ref.dtype)


def _ret_step(zqk, zv, zg, states, layer, gain, *, dk, dv):
    bd = zqk.shape[0]
    nv = RET_HEADS * dv
    row = lambda t: pl.BlockSpec((None, 1, t.shape[1]), lambda b: (b, 0, 0))
    blk = pl.BlockSpec((None, RET_HEADS, dk, dv), lambda b: (b, 0, 0, 0))
    y, s_new = pl.pallas_call(
        functools.partial(_ret_step_kernel, dk=dk, dv=dv),
        grid=(bd,),
        in_specs=[row(zqk), row(zv), row(zg),
                  pl.BlockSpec((None, None, RET_HEADS, dk, dv), lambda b: (layer, b, 0, 0, 0)),
                  pl.BlockSpec((1, nv), lambda b: (0, 0))],
        out_specs=[pl.BlockSpec((None, 1, nv), lambda b: (b, 0, 0)), blk],
        out_shape=[jax.ShapeDtypeStruct((bd, 1, nv), BF16), jax.ShapeDtypeStruct(states.shape[1:], F32)],
        compiler_params=_cparams(("arbitrary",)),
        name="ret_step",
    )(*[t.reshape(bd, 1, t.shape[1]) for t in (zqk, zv, zg)], states, gain)
    return y.reshape(bd, nv), s_new


def _fox_decode_kernel(pt_ref, q_ref, qb_ref, kn_ref, vn_ref, lfn_ref, *refs, group):
    del pt_ref
    k_refs = refs[0:group]
    v_refs = refs[group:2 * group]
    lp_refs = refs[2 * group:3 * group]
    o_ref, m_ref, l_ref, w_ref, r_ref, acc_ref = refs[3 * group:]
    c = pl.program_id(1)
    head = lax.broadcasted_iota(jnp.int32, (FOX_HEADS, FOX_DIM), 0)
    chan = lax.broadcasted_iota(jnp.int32, (FOX_HEADS, FOX_DIM), 1)
    own = chan // FOX_HEAD_DIM == head

    def per_head_to_channels(col):
        return jnp.sum(jnp.where(own, jnp.broadcast_to(col, own.shape), 0.0), axis=0, keepdims=True)

    @pl.when(c == 0)
    def _():
        prod = jnp.broadcast_to(q_ref[...] * kn_ref[...], own.shape)
        m_ref[...] = jnp.sum(jnp.where(own, prod, 0.0), axis=-1, keepdims=True)
        l_ref[...] = jnp.ones_like(l_ref)
        w_ref[...] = jnp.ones_like(w_ref)
        r_ref[...] = jnp.zeros_like(r_ref)
        acc_ref[...] = jnp.zeros_like(acc_ref)

    lane = lax.broadcasted_iota(jnp.int32, (1, PAGE_SIZE), 1)
    head_row = lax.broadcasted_iota(jnp.int32, (FOX_HEADS, 1), 0)
    lfn = lfn_ref[...]
    later = r_ref[...]
    scores = []
    for g in range(group):
        lp = lp_refs[g][...]
        suffix = lp
        stride = 1
        while stride < PAGE_SIZE:
            ahead = pltpu.roll(suffix, PAGE_SIZE - stride, axis=1)
            suffix = suffix + jnp.where(lane < PAGE_SIZE - stride, ahead, 0.0)
            stride *= 2
        s = lfn + later + (suffix - lp)
        later = later + jnp.sum(lp, axis=-1, keepdims=True)
        for h in range(FOX_HEADS):
            qk = jnp.sum(k_refs[g][h] * qb_ref[h], axis=0, keepdims=True)
            s = s + jnp.where(head_row == h, jnp.broadcast_to(qk, s.shape), 0.0)
        scores.append(s)
    r_ref[...] = later
    m_prev = m_ref[...]
    m_new = m_prev
    for s in scores:
        m_new = jnp.maximum(m_new, jnp.max(s, axis=-1, keepdims=True))
    alpha = jnp.exp(m_prev - m_new)
    l_new = alpha * l_ref[...]
    probs = []
    for s in scores:
        p = jnp.exp(s - m_new)
        l_new = l_new + jnp.sum(p, axis=-1, keepdims=True)
        probs.append(p)
    for h in range(FOX_HEADS):
        acc = acc_ref[h] * alpha[h:h + 1, :]
        for g in range(group):
            acc = acc + v_refs[g][h] * probs[g][h:h + 1, :]
        acc_ref[h] = acc
    m_ref[...] = m_new
    l_ref[...] = l_new
    w_ref[...] = alpha * w_ref[...]

    @pl.when(c == pl.num_programs(1) - 1)
    def _():
        acc_t = acc_ref[...].reshape(FOX_DIM, PAGE_SIZE).T
        pv = jnp.sum(acc_t, axis=0, keepdims=True)
        o = (pv + per_head_to_channels(w_ref[...]) * vn_ref[...]) / per_head_to_channels(l_ref[...])
        o_ref[...] = o.astype(o_ref.dtype)


def _fox_decode(q, k_new, v_new, logf_new, cache_kt, cache_vt, cache_lpt, page_table, layer, *, group):
    bd, c = q.shape
    n_pages = page_table.shape[1]
    group = min(group, n_pages)
    assert n_pages % group == 0
    pt_flat = page_table.reshape(-1)
    q = q.astype(F32)
    q_lanes = jnp.broadcast_to(q.reshape(bd, FOX_HEADS, FOX_HEAD_DIM, 1), (bd, FOX_HEADS, FOX_HEAD_DIM, PAGE_SIZE))

    def page_spec(g, shape):
        def index(b, s, pt):
            return (layer, pt[b * n_pages + (n_pages - 1 - (s * group + g))]) + (0,) * len(shape)
        return pl.BlockSpec((None, None) + shape, index)

    row = pl.BlockSpec((None, 1, c), lambda b, s, pt: (b, 0, 0))
    slab = (FOX_HEADS, FOX_HEAD_DIM, PAGE_SIZE)
    in_specs = ([row, pl.BlockSpec((None,) + slab, lambda b, s, pt: (b, 0, 0, 0)), row, row,
                 pl.BlockSpec((None, FOX_HEADS, 1), lambda b, s, pt: (b, 0, 0))]
                + [page_spec(g, slab) for g in range(group)]
                + [page_spec(g, slab) for g in range(group)]
                + [page_spec(g, (FOX_HEADS, PAGE_SIZE)) for g in range(group)])
    col = pltpu.VMEM((FOX_HEADS, 1), F32)
    out = pl.pallas_call(
        functools.partial(_fox_decode_kernel, group=group),
        grid_spec=pltpu.PrefetchScalarGridSpec(
            num_scalar_prefetch=1,
            grid=(bd, n_pages // group),
            in_specs=in_specs,
            out_specs=row,
            scratch_shapes=[col, col, col, col, pltpu.VMEM(slab, F32)]),
        out_shape=jax.ShapeDtypeStruct((bd, 1, c), BF16),
        compiler_params=_cparams(("arbitrary", "arbitrary")),
        name="fox_decode",
    )(pt_flat, q.reshape(bd, 1, c), q_lanes, k_new.reshape(bd, 1, c), v_new.reshape(bd, 1, c),
      logf_new.reshape(bd, FOX_HEADS, 1), *([cache_kt] * group), *([cache_vt] * group), *([cache_lpt] * group))
    return out.reshape(bd, c)


def _rope_tables(pos, half):
    inv = 1.0 / (ROPE_BASE ** jnp.linspace(0.0, 1.0, half, dtype=F32))
    ang = pos.astype(F32)[:, None] * inv[None, :]
    return jnp.cos(ang), jnp.sin(ang)


def _prep_weights(w_in_even, b_forget, conv_w, w_out_even, w_in_odd, w_ffn_gate, w_ffn_up, w_ffn_down, dk):
    c = FOX_DIM
    n_even, d, _ = w_in_even.shape
    qkv_scale = jnp.where(jnp.arange(3 * c) < c, FOX_HEAD_DIM ** -0.5, 1.0).astype(F32)
    w_qkv = (w_in_even[:, :, :3 * c] * qkv_scale).astype(BF16)
    w_ab = w_in_even[:, :, 3 * c + FOX_HEADS:].astype(BF16)
    w_f = jnp.pad(w_in_even[:, :, 3 * c:3 * c + FOX_HEADS], ((0, 0), (0, 0), (0, LANES - FOX_HEADS))).astype(BF16)
    bf_pad = jnp.pad(b_forget, ((0, 0), (0, LANES - FOX_HEADS)))[:, None, :]
    cw_pad = jnp.pad(conv_w, ((0, 0), (0, CONV_HALO - CONV_WIDTH), (0, 0)))
    nk = RET_HEADS * dk
    col = jnp.arange(w_in_odd.shape[-1])
    odd_scale = jnp.where((col >= nk) & (col < 2 * nk), dk ** -0.5, 1.0).astype(F32)
    nv = (w_in_odd.shape[-1] - 2 * nk) // 2
    w_odd = ((w_in_odd[:, :, :2 * nk] * odd_scale[:2 * nk]).astype(BF16),
             w_in_odd[:, :, 2 * nk:2 * nk + nv].astype(BF16),
             w_in_odd[:, :, 2 * nk + nv:].astype(BF16))
    return dict(w_even=(w_qkv, w_ab, w_f), bf_pad=bf_pad, cw_pad=cw_pad,
                w_out_fox=w_out_even[:, :c].astype(BF16), w_out_conv=w_out_even[:, c:].astype(BF16),
                w_odd=w_odd, wg=w_ffn_gate.astype(BF16), wu=w_ffn_up.astype(BF16), wd=w_ffn_down.astype(BF16))


def _tiles(rows, seq_len):
    if seq_len == 1:
        return dict(tm=rows, tm_ffn=rows, th=1408, tm_odd=rows, tn_odd=1024)
    return dict(tm=min(512, seq_len), tm_ffn=min(512, seq_len), th=1408, tm_odd=min(1024, seq_len), tn_odd=1024)


def _trunk(x, mod, pos, P, W, *, seq_len, even_mixer, odd_mixer):
    depth = mod.shape[0]
    d = x.shape[1]
    t = _tiles(x.shape[0], seq_len)
    tm = t['tm']
    cos, sin = _rope_tables(pos, LANES)
    ks, vs, lfs, cbs, rs = [], [], [], [], []
    for l in range(depth):
        sh1, sc1, g1, sh2, sc2, g2 = [mod[l][..., s * d:(s + 1) * d] for s in range(6)]
        gm = W['norm_mix_g'][l][None, :]
        if l % 2 == 0:
            e = l // 2
            out = _even_proj(x, gm, sc1, sh1, [w[e] for w in P['w_even']], P['bf_pad'][e], tm=tm, tq=min(FOX_TQ, tm),
                             seq_len=seq_len, with_cum=even_mixer.needs_cum)
            q, k, v, logf, u = out[:5]
            o_fox, cv, new_buf = even_mixer(e, q, k, v, logf, u, out[5:])
            ks.append(k); vs.append(v); lfs.append(logf); cbs.append(new_buf)
            x = _proj_res([o_fox, cv], [P['w_out_fox'][e], P['w_out_conv'][e]], x, g1, tm=tm)
        else:
            o = l // 2
            zs = [_odd_proj(x, gm, sc1, sh1, w[o], cos, sin, tm=t['tm_odd'], tn=t['tn_odd'], epilogue=ep)
                  for w, ep in zip(P['w_odd'], ('rope', 'plain', 'silu'))]
            y, s_new = odd_mixer(o, *zs)
            rs.append(s_new)
            x = _proj_res([y], [P['w_out_odd'][o]], x, g1, tm=tm)
        x = _ffn(x, W['norm_ffn_g'][l][None, :], sc2, sh2, g2, P['wg'][l], P['wu'][l], P['wd'][l],
                 W['final_norm_g'][None, :], tm=t['tm_ffn'], th=t['th'], final=(l == depth - 1))
    return x, (ks, vs, lfs, cbs, rs)


class _PromptEven:
    needs_cum = True

    def __init__(self, batch, seq_len, W, P):
        self.batch, self.seq_len, self.W, self.P = batch, seq_len, W, P

    def __call__(self, e, q, k, v, logf, u, extra):
        kb, vt, cum, cumt = extra
        W, P = self.W, self.P
        o_fox = _fox_prompt(q, kb, vt, cum, cumt, batch=self.batch, seq_len=self.seq_len, tq=FOX_TQ)
        cv = _conv_prompt(u, P['cw_pad'][e], W['conv_b'][e][None, :], W['conv_norm_g'][e][None, :],
                          W['conv_norm_b'][e][None, :], batch=self.batch, seq_len=self.seq_len, tl=256)
        c = u.shape[1]
        new_buf = u.reshape(self.batch, self.seq_len, c)[:, self.seq_len - (CONV_WIDTH - 1):]
        return o_fox, cv, new_buf


class _SampleEven:
    needs_cum = False

    def __init__(self, W, P, cache_k, cache_v, cache_lp, state_conv, page_table):
        self.W, self.P = W, P
        self.cache_k, self.cache_v, self.cache_lp = cache_k, cache_v, cache_lp
        self.state_conv, self.page_table = state_conv, page_table

    def __call__(self, e, q, k, v, logf, u, extra):
        W, P = self.W, self.P
        o_fox = _fox_decode(q, k, v, logf, self.cache_k, self.cache_v, self.cache_lp, self.page_table, e, group=16)
        st = self.state_conv[e]
        cv = _conv_step(jnp.transpose(st, (1, 0, 2)), u, P['cw_pad'][e], W['conv_b'][e][None, :],
                        W['conv_norm_g'][e][None, :], W['conv_norm_b'][e][None, :])
        new_buf = jnp.concatenate([st[:, 1:], u[:, None, :]], axis=1)
        return o_fox, cv, new_buf


def kernel(x_prompt, x_sample, cache_k, cache_v, cache_logf, state_conv, state_ret, page_table, c_prompt, c_sample,
           ada_w, ada_b, norm_mix_g, norm_ffn_g, w_in_even, b_forget, conv_w, conv_b, conv_norm_g, conv_norm_b,
           w_out_even, w_in_odd, ret_norm_g, w_out_odd, w_ffn_gate, w_ffn_up, w_ffn_down, final_norm_g):
    bp, lp, d = x_prompt.shape
    bs, ls, _ = x_sample.shape
    assert ls == 1, "the decode path handles one new token per sequence"
    depth = ada_w.shape[0]
    n_even, n_phys, page, heads, hd = cache_k.shape
    assert (page, heads, hd) == (PAGE_SIZE, FOX_HEADS, FOX_HEAD_DIM)
    dk, dv = state_ret.shape[-2:]
    past_len = page_table.shape[1] * PAGE_SIZE

    W = dict(norm_mix_g=norm_mix_g, norm_ffn_g=norm_ffn_g, conv_b=conv_b, conv_norm_g=conv_norm_g,
             conv_norm_b=conv_norm_b, final_norm_g=final_norm_g)
    P = _prep_weights(w_in_even, b_forget, conv_w, w_out_even, w_in_odd, w_ffn_gate, w_ffn_up, w_ffn_down, dk)
    P['w_out_odd'] = w_out_odd.astype(BF16)
    gain = ret_norm_g[:, None, :]

    mod = _adaln(jnp.concatenate([c_prompt, c_sample], axis=0), ada_w, ada_b)
    mod_p = mod[:, :bp].reshape(depth, bp, 1, 6 * d)
    mod_s = mod[:, bp:].reshape(depth, 1, bs, 6 * d)

    def odd_prompt(o, zqk, zv, zg):
        return _ret_prompt(zqk, zv, zg, gain[o], batch=bp, seq_len=lp, chunk=min(256, lp), dk=dk, dv=dv)

    y_p, (k_p, v_p, lf_p, cb_p, r_p) = _trunk(
        x_prompt.reshape(bp * lp, d), mod_p, jnp.arange(lp), P, W, seq_len=lp,
        even_mixer=_PromptEven(bp, lp, W, P), odd_mixer=odd_prompt)

    ckt = jnp.transpose(cache_k, (0, 1, 3, 4, 2))
    cvt = jnp.transpose(cache_v, (0, 1, 3, 4, 2))
    clp = jnp.transpose(cache_logf, (0, 1, 3, 2))

    def odd_sample(o, zqk, zv, zg):
        return _ret_step(zqk, zv, zg, state_ret, o, gain[o], dk=dk, dv=dv)

    y_s, (k_s, v_s, lf_s, cb_s, r_s) = _trunk(
        x_sample.reshape(bs, d), mod_s, jnp.full((bs,), past_len, jnp.int32), P, W, seq_len=1,
        even_mixer=_SampleEven(W, P, ckt, cvt, clp, state_conv, page_table), odd_mixer=odd_sample)

    hshape = (FOX_HEADS, FOX_HEAD_DIM)

    def seq_last_to_heads(ts):
        return jnp.transpose(jnp.stack(ts).reshape((n_even, bp) + hshape + (lp,)), (0, 1, 4, 2, 3))

    return (y_p.reshape(bp, lp, d), y_s.reshape(bs, ls, d),
            seq_last_to_heads(k_p), seq_last_to_heads(v_p),
            jnp.transpose(jnp.stack(lf_p), (0, 1, 3, 2)), jnp.stack(cb_p), jnp.stack(r_p),
            jnp.stack(k_s).reshape((n_even, bs, ls) + hshape), jnp.stack(v_s).reshape((n_even, bs, ls) + hshape),
            jnp.stack(lf_s).reshape(n_even, bs, ls, FOX_HEADS), jnp.stack(cb_s),
            jnp.stack(r_s).astype(state_ret.dtype))
```

```python
import functools
import math

import jax
import jax.numpy as jnp
import numpy as np
from jax import lax
from jax.experimental import pallas as pl
from jax.experimental.pallas import tpu as pltpu

F32 = jnp.float32
BF16 = jnp.bfloat16

EPS = 1e-6
ROPE_BASE = 10000.0
FOX_HEADS = 8
FOX_HEAD_DIM = 64
FOX_DIM = FOX_HEADS * FOX_HEAD_DIM
CONV_GROUPS = 8
CONV_WIDTH = 31
RET_HEADS = 4
PAGE_SIZE = 128
LANES = 128
SUBLANES = 8
CONV_HALO = 32
VMEM_LIMIT = 56 * 1024 * 1024
NEG_BIG = -1e30
LOG2E = math.log2(math.e)
FOX_TQ = 256


def _cparams(sem):
    return pltpu.CompilerParams(dimension_semantics=sem, vmem_limit_bytes=VMEM_LIMIT)


def _silu(x):
    return x * jax.nn.sigmoid(x)


def _log_sigmoid(x):
    return jnp.minimum(x, 0.0) - jnp.log1p(jnp.exp(-jnp.abs(x)))


def _norm_mod(x, g, sc, sh):
    ms = jnp.mean(x * x, axis=-1, keepdims=True)
    y = x * lax.rsqrt(ms + EPS) * g
    return y * (1.0 + sc) + sh


def _split3(x):
    hi = x.astype(BF16)
    r1 = x - hi.astype(F32)
    mid = r1.astype(BF16)
    lo = (r1 - mid.astype(F32)).astype(BF16)
    return hi, mid, lo


def _dot3(x, w_bf16):
    hi, mid, lo = _split3(x)
    d = functools.partial(jnp.dot, preferred_element_type=F32)
    return d(hi, w_bf16) + d(mid, w_bf16) + d(lo, w_bf16)


def _dot_nt(a, b):
    return lax.dot_general(a, b, (((1,), (1,)), ((), ())), preferred_element_type=F32)


def _mod_spec(rows, d, tiles_per_mod):
    return pl.BlockSpec((None, rows, d), lambda i, *_: (i // tiles_per_mod, 0, 0))


def _adaln_kernel(c_ref, w_ref, b_ref, o_ref):
    cm = _silu(c_ref[...]).astype(BF16)
    o_ref[...] = jnp.dot(cm, w_ref[...].astype(BF16), preferred_element_type=F32) + b_ref[...]


def _adaln(c_all, ada_w, ada_b, tn=1536):
    depth, d, n = ada_w.shape
    r = c_all.shape[0]
    return pl.pallas_call(
        _adaln_kernel,
        grid=(depth, n // tn),
        in_specs=[pl.BlockSpec((r, d), lambda l, j: (0, 0)),
                  pl.BlockSpec((None, d, tn), lambda l, j: (l, 0, j)),
                  pl.BlockSpec((None, 1, tn), lambda l, j: (l, 0, j))],
        out_specs=pl.BlockSpec((None, r, tn), lambda l, j: (l, 0, j)),
        out_shape=jax.ShapeDtypeStruct((depth, r, n), F32),
        compiler_params=_cparams(("arbitrary", "arbitrary")),
        name="adaln",
    )(c_all, ada_w, ada_b.reshape(depth, 1, n))


def _even_proj_kernel(x_ref, g_ref, sc_ref, sh_ref, wqkv_ref, wab_ref, wf_ref, bf_ref, *refs,
                      tm, tq, tiles_per_seq, with_cum):
    if with_cum:
        q_ref, k_ref, v_ref, lf_ref, u_ref, kb_ref, vt_ref, cum_ref, cumt_ref, carry_ref = refs
    else:
        q_ref, k_ref, v_ref, lf_ref, u_ref = refs
    hm = _norm_mod(x_ref[...], g_ref[...], sc_ref[...], sh_ref[...]).astype(BF16)
    c = FOX_DIM
    qkv = jnp.dot(hm, wqkv_ref[...], preferred_element_type=F32)
    q_ref[...] = qkv[:, 0:c].astype(BF16)
    k = qkv[:, c:2 * c]
    v = qkv[:, 2 * c:3 * c]
    ab = jnp.dot(hm, wab_ref[...], preferred_element_type=F32)
    u_ref[...] = ab[:, 0:c] * jax.nn.sigmoid(ab[:, c:2 * c])
    fl = jnp.dot(hm, wf_ref[...], preferred_element_type=F32)
    logf = _log_sigmoid(fl + bf_ref[...])
    if not with_cum:
        k_ref[...] = k
        v_ref[...] = v
        lf_ref[...] = logf[:, 0:FOX_HEADS]
    else:
        v_t = v.T
        k_ref[...] = k.T
        v_ref[...] = v_t
        lf_ref[...] = logf.T[0:FOX_HEADS, :]
        kb_ref[...] = (k * LOG2E).astype(BF16)
        for r in range(tm // tq):
            vt_ref[r] = v_t[:, r * tq:(r + 1) * tq].astype(BF16)
        i = pl.program_id(0)

        @pl.when(i % tiles_per_seq == 0)
        def _():
            carry_ref[...] = jnp.zeros_like(carry_ref)

        row = lax.broadcasted_iota(jnp.int32, (tm, tm), 0)
        col = lax.broadcasted_iota(jnp.int32, (tm, tm), 1)
        tri = (col <= row).astype(BF16)
        cum = _dot3_left(tri, logf) + carry_ref[...]
        carry_ref[...] = cum[tm - 1:tm, :]
        cum = cum * LOG2E
        cum_ref[...] = cum[:, 0:FOX_HEADS]
        cum_t = cum.T
        for r in range(tm // tq):
            cumt_ref[r] = cum_t[0:FOX_HEADS, r * tq:(r + 1) * tq]


def _dot3_left(w_bf16, x):
    hi, mid, lo = _split3(x)
    d = functools.partial(jnp.dot, preferred_element_type=F32)
    return d(w_bf16, hi) + d(w_bf16, mid) + d(w_bf16, lo)


def _even_proj(x, g, sc, sh, weights, bf_pad, *, tm, tq, seq_len, with_cum):
    m, d = x.shape
    nmod, rows, _ = sc.shape
    tiles_per_mod = (m // nmod) // tm
    c = FOX_DIM
    row_spec = lambda width: pl.BlockSpec((tm, width), lambda i: (i, 0))
    tiles_per_seq = max(seq_len // tm, 1)
    if with_cum:
        nseq = m // seq_len
        seq_major = lambda width: (jax.ShapeDtypeStruct((nseq, width, seq_len), F32),
                                   pl.BlockSpec((None, width, tm), lambda i: (i // tiles_per_seq, 0, i % tiles_per_seq)))
        kvl = [seq_major(c), seq_major(c), seq_major(FOX_HEADS)]
    else:
        kvl = [(jax.ShapeDtypeStruct((m, w_), F32), row_spec(w_)) for w_ in (c, c, FOX_HEADS)]
    out_shape = [jax.ShapeDtypeStruct((m, c), BF16)] + [s for s, _ in kvl] + [jax.ShapeDtypeStruct((m, c), F32)]
    out_specs = [row_spec(c)] + [s for _, s in kvl] + [row_spec(c)]
    scratch = []
    if with_cum:
        out_shape += [jax.ShapeDtypeStruct((m, c), BF16), jax.ShapeDtypeStruct((m // tq, c, tq), BF16),
                      jax.ShapeDtypeStruct((m, FOX_HEADS), F32), jax.ShapeDtypeStruct((m // tq, FOX_HEADS, tq), F32)]
        out_specs += [row_spec(c), pl.BlockSpec((tm // tq, c, tq), lambda i: (i, 0, 0)),
                      row_spec(FOX_HEADS), pl.BlockSpec((tm // tq, FOX_HEADS, tq), lambda i: (i, 0, 0))]
        scratch = [pltpu.VMEM((1, LANES), F32)]
    return pl.pallas_call(
        functools.partial(_even_proj_kernel, tm=tm, tq=tq, tiles_per_seq=tiles_per_seq, with_cum=with_cum),
        grid=(m // tm,),
        in_specs=[row_spec(d), pl.BlockSpec((1, d), lambda i: (0, 0)),
                  _mod_spec(rows, d, tiles_per_mod), _mod_spec(rows, d, tiles_per_mod),
                  *[pl.BlockSpec(w.shape, lambda i: (0, 0)) for w in weights],
                  pl.BlockSpec((1, LANES), lambda i: (0, 0))],
        out_specs=out_specs,
        out_shape=out_shape,
        scratch_shapes=scratch,
        compiler_params=_cparams(("arbitrary",)),
        name="even_proj",
    )(x, g, sc, sh, *weights, bf_pad)


def _fox_prompt_kernel(q_ref, k_ref, vt_ref, ck_ref, cq_ref, o_ref, qm_ref, m_ref, acc_ref, *, tq):
    i = pl.program_id(1)
    hd = FOX_HEAD_DIM
    low = lax.broadcasted_iota(jnp.int32, (1, LANES), 1) < hd
    for pair in range(FOX_HEADS // 2):
        qp = q_ref[:, pair * LANES:(pair + 1) * LANES].astype(F32)
        qm_ref[2 * pair] = jnp.where(low, qp, 0.0).astype(BF16)
        qm_ref[2 * pair + 1] = jnp.where(low, 0.0, qp).astype(BF16)
    m_ref[...] = jnp.full_like(m_ref, NEG_BIG)
    acc_ref[...] = jnp.zeros_like(acc_ref)
    key = lax.broadcasted_iota(jnp.int32, (tq, tq), 0)
    qry = lax.broadcasted_iota(jnp.int32, (tq, tq), 1)
    causal = key <= qry
    ones = jnp.ones((acc_ref.shape[1] - hd, tq), BF16)

    def step(j, masked):
        start = pl.multiple_of(j * tq, tq)
        cq_all = cq_ref[i]
        qk = []
        for h in range(FOX_HEADS):
            pair = h // 2
            kj = k_ref[pl.ds(start, tq), pair * LANES:(pair + 1) * LANES]
            qk.append(_dot_nt(kj, qm_ref[h]))
        for h in range(FOX_HEADS):
            t = qk[h] - ck_ref[pl.ds(start, tq), h:h + 1]
            if masked:
                t = jnp.where(causal, t, NEG_BIG)
            cq = cq_all[h:h + 1, :]
            m_prev = m_ref[h:h + 1, :]
            m_new = jnp.maximum(m_prev, jnp.max(t, axis=0, keepdims=True) + cq)
            p = jnp.exp2(t + (cq - m_new)).astype(BF16)
            alpha = jnp.exp2(m_prev - m_new)
            v_ones = jnp.concatenate([vt_ref[j][h * hd:(h + 1) * hd, :], ones], axis=0)
            acc_ref[h] = alpha * acc_ref[h] + jnp.dot(v_ones, p, preferred_element_type=F32)
            m_ref[h:h + 1, :] = m_new

    def body(j, carry):
        step(j, False)
        return carry

    lax.fori_loop(0, i, body, 0)
    step(i, True)
    for pair in range(FOX_HEADS // 2):
        halves = []
        for h in (2 * pair, 2 * pair + 1):
            halves.append(acc_ref[h, 0:hd, :] * (1.0 / acc_ref[h, hd:hd + 1, :]))
        o_ref[:, pair * LANES:(pair + 1) * LANES] = jnp.concatenate(halves, axis=0).T.astype(o_ref.dtype)


def _fox_prompt(q, kb, vt, cum, cumt, *, batch, seq_len, tq):
    m, c = q.shape
    nq = seq_len // tq
    return pl.pallas_call(
        functools.partial(_fox_prompt_kernel, tq=tq),
        grid=(batch, nq),
        in_specs=[pl.BlockSpec((tq, c), lambda b, i: (b * nq + i, 0)),
                  pl.BlockSpec((seq_len, c), lambda b, i: (b, 0)),
                  pl.BlockSpec((nq, c, tq), lambda b, i: (b, 0, 0)),
                  pl.BlockSpec((seq_len, FOX_HEADS), lambda b, i: (b, 0)),
                  pl.BlockSpec((nq, FOX_HEADS, tq), lambda b, i: (b, 0, 0))],
        out_specs=pl.BlockSpec((tq, c), lambda b, i: (b * nq + i, 0)),
        out_shape=jax.ShapeDtypeStruct((m, c), BF16),
        scratch_shapes=[pltpu.VMEM((FOX_HEADS, tq, LANES), BF16), pltpu.VMEM((FOX_HEADS, tq), F32),
                        pltpu.VMEM((FOX_HEADS, FOX_HEAD_DIM + 16, tq), F32)],
        compiler_params=_cparams(("arbitrary", "arbitrary")),
        name="fox_prompt",
    )(q, kb, vt, cum, cumt)


def _group_matrices(channels, groups):
    gsz = channels // groups
    ch = lax.broadcasted_iota(jnp.int32, (channels, LANES), 0)
    gr = lax.broadcasted_iota(jnp.int32, (channels, LANES), 1)
    gather = (ch // gsz == gr).astype(BF16)
    gr_t = lax.broadcasted_iota(jnp.int32, (LANES, channels), 0)
    ch_t = lax.broadcasted_iota(jnp.int32, (LANES, channels), 1)
    spread = (ch_t // gsz == gr_t).astype(BF16)
    return gather, spread, 1.0 / gsz


def _group_norm_rows(y, groups):
    gather, spread, inv = _group_matrices(y.shape[-1], groups)
    mu = _dot3(y, gather) * inv
    d = y - _dot3(mu, spread)
    var = _dot3(d * d, gather) * inv
    return d * _dot3(lax.rsqrt(var + EPS), spread)


def _conv_finish(acc, cb, gn_g, gn_b):
    cv = _group_norm_rows(acc + cb, CONV_GROUPS) * gn_g + gn_b
    return _silu(cv)


def _conv_prompt_kernel(u_ref, halo_ref, cw_ref, cb_ref, gg_ref, gb_ref, o_ref, xp_ref, sh_ref, y_ref, *, tl):
    i = pl.program_id(1)
    halo = halo_ref[...]
    xp_ref[0:CONV_HALO, :] = jnp.where(i == 0, jnp.zeros_like(halo), halo)
    xp_ref[CONV_HALO:CONV_HALO + tl, :] = u_ref[...]
    span = sh_ref.shape[1]
    for r in range(1, SUBLANES):
        sh_ref[r - 1] = xp_ref[r:r + span, :]
    base = CONV_HALO - (CONV_WIDTH - 1)
    rows = 64
    for r0 in range(0, tl, rows):
        for c0 in range(0, u_ref.shape[1], LANES):
            acc = jnp.zeros((rows, LANES), F32)
            for w in range(CONV_WIDTH):
                shift = (base + w) % SUBLANES
                start = r0 + base + w - shift
                if shift == 0:
                    tap = xp_ref[start:start + rows, c0:c0 + LANES]
                else:
                    tap = sh_ref[shift - 1, start:start + rows, c0:c0 + LANES]
                acc = acc + tap * cw_ref[w:w + 1, c0:c0 + LANES]
            y_ref[r0:r0 + rows, c0:c0 + LANES] = acc
    y = _conv_finish(y_ref[...], cb_ref[...], gg_ref[...], gb_ref[...])
    o_ref[...] = y.astype(o_ref.dtype)


def _conv_prompt(u, cw, cb, gn_g, gn_b, *, batch, seq_len, tl):
    m, c = u.shape
    nl = seq_len // tl
    hb = tl // CONV_HALO
    vec = pl.BlockSpec((1, c), lambda b, i: (0, 0))
    return pl.pallas_call(
        functools.partial(_conv_prompt_kernel, tl=tl),
        grid=(batch, nl),
        in_specs=[pl.BlockSpec((tl, c), lambda b, i: (b * nl + i, 0)),
                  pl.BlockSpec((CONV_HALO, c), lambda b, i: (jnp.maximum((b * nl + i) * hb - 1, 0), 0)),
                  pl.BlockSpec((CONV_HALO, c), lambda b, i: (0, 0)), vec, vec, vec],
        out_specs=pl.BlockSpec((tl, c), lambda b, i: (b * nl + i, 0)),
        out_shape=jax.ShapeDtypeStruct((m, c), BF16),
        scratch_shapes=[pltpu.VMEM((CONV_HALO + tl, c), F32),
                        pltpu.VMEM((SUBLANES - 1, CONV_HALO - SUBLANES + tl, c), F32), pltpu.VMEM((tl, c), F32)],
        compiler_params=_cparams(("arbitrary", "arbitrary")),
        name="conv_prompt",
    )(u, u, cw, cb, gn_g, gn_b)


def _conv_step_kernel(st_ref, u_ref, cw_ref, cb_ref, gg_ref, gb_ref, o_ref):
    acc = u_ref[...] * cw_ref[CONV_WIDTH - 1:CONV_WIDTH, :]
    for w in range(CONV_WIDTH - 1):
        acc = acc + st_ref[w] * cw_ref[w:w + 1, :]
    o_ref[...] = _conv_finish(acc, cb_ref[...], gg_ref[...], gb_ref[...]).astype(o_ref.dtype)


def _conv_step(state_t, u, cw, cb, gn_g, gn_b):
    bd, c = u.shape
    full = lambda shape: pl.BlockSpec(shape, lambda i: (0,) * len(shape))
    return pl.pallas_call(
        _conv_step_kernel,
        grid=(1,),
        in_specs=[full(state_t.shape), full((bd, c)), full(cw.shape), full((1, c)), full((1, c)), full((1, c))],
        out_specs=full((bd, c)),
        out_shape=jax.ShapeDtypeStruct((bd, c), BF16),
        compiler_params=_cparams(("arbitrary",)),
        name="conv_step",
    )(state_t, u, cw, cb, gn_g, gn_b)


def _mix_ffn_kernel(*refs, n_in, final):
    a_refs = refs[:n_in]
    w_refs = refs[n_in:2 * n_in]
    (x_ref, g1_ref, g_ref, sc_ref, sh_ref, g2_ref, wg_ref, wu_ref, wd_ref, fg_ref,
     o_ref, x1_ref, hf_ref, acc_ref) = refs[2 * n_in:]
    j = pl.program_id(1)

    @pl.when(j == 0)
    def _():
        mix = jnp.dot(a_refs[0][...], w_refs[0][...], preferred_element_type=F32)
        for a_ref, w_ref in zip(a_refs[1:], w_refs[1:]):
            mix = mix + jnp.dot(a_ref[...], w_ref[...], preferred_element_type=F32)
        x1 = x_ref[...] + g1_ref[...] * mix
        x1_ref[...] = x1
        hf_ref[...] = _norm_mod(x1, g_ref[...], sc_ref[...], sh_ref[...]).astype(BF16)
        acc_ref[...] = jnp.zeros_like(acc_ref)

    hf = hf_ref[...]
    gt = jnp.dot(hf, wg_ref[...], preferred_element_type=F32)
    up = jnp.dot(hf, wu_ref[...], preferred_element_type=F32)
    acc_ref[...] += jnp.dot((_silu(gt) * up).astype(BF16), wd_ref[...], preferred_element_type=F32)

    @pl.when(j == pl.num_programs(1) - 1)
    def _():
        y = x1_ref[...] + g2_ref[...] * acc_ref[...]
        if final:
            ms = jnp.mean(y * y, axis=-1, keepdims=True)
            y = y * lax.rsqrt(ms + EPS) * fg_ref[...]
        o_ref[...] = y


def _mix_ffn(acts, w_outs, x, g1, g, sc, sh, g2, wg, wu, wd, fg, *, tm, th, final):
    m, d = x.shape
    nmod, rows, _ = sc.shape
    hdim = wg.shape[1]
    tiles_per_mod = (m // nmod) // tm
    vec = pl.BlockSpec((1, d), lambda i, j: (0, 0))
    mod = _mod_spec(rows, d, tiles_per_mod)
    in_specs = ([pl.BlockSpec((tm, a.shape[1]), lambda i, j: (i, 0)) for a in acts]
                + [pl.BlockSpec(w.shape, lambda i, j: (0, 0)) for w in w_outs]
                + [pl.BlockSpec((tm, d), lambda i, j: (i, 0)), mod, vec, mod, mod, mod,
                   pl.BlockSpec((d, th), lambda i, j: (0, j)), pl.BlockSpec((d, th), lambda i, j: (0, j)),
                   pl.BlockSpec((th, d), lambda i, j: (j, 0)), vec])
    return pl.pallas_call(
        functools.partial(_mix_ffn_kernel, n_in=len(acts), final=final),
        grid=(m // tm, hdim // th),
        in_specs=in_specs,
        out_specs=pl.BlockSpec((tm, d), lambda i, j: (i, 0)),
        out_shape=jax.ShapeDtypeStruct((m, d), F32),
        scratch_shapes=[pltpu.VMEM((tm, d), F32), pltpu.VMEM((tm, d), BF16), pltpu.VMEM((tm, d), F32)],
        compiler_params=_cparams(("arbitrary", "arbitrary")),
        name="mix_ffn",
    )(*acts, *w_outs, x, g1, g, sc, sh, g2, wg, wu, wd, fg)


def _odd_proj_kernel(x_ref, g_ref, sc_ref, sh_ref, w_ref, cos_ref, sin_ref, o_ref, hm_ref):
    j = pl.program_id(1)

    @pl.when(j == 0)
    def _():
        hm_ref[...] = _norm_mod(x_ref[...], g_ref[...], sc_ref[...], sh_ref[...]).astype(BF16)

    def slab():
        return jnp.dot(hm_ref[...], w_ref[...], preferred_element_type=F32)

    @pl.when(j == 0)
    def _():
        z = slab()
        cos = cos_ref[...]
        sin = sin_ref[...]
        for c0 in range(0, z.shape[1], 2 * LANES):
            x1 = z[:, c0:c0 + LANES]
            x2 = z[:, c0 + LANES:c0 + 2 * LANES]
            o_ref[:, c0:c0 + LANES] = (x1 * cos - x2 * sin).astype(o_ref.dtype)
            o_ref[:, c0 + LANES:c0 + 2 * LANES] = (x1 * sin + x2 * cos).astype(o_ref.dtype)

    @pl.when(j == 1)
    def _():
        o_ref[...] = slab().astype(o_ref.dtype)

    @pl.when(j == 2)
    def _():
        o_ref[...] = _silu(slab()).astype(o_ref.dtype)


def _odd_proj(x, g, sc, sh, w, cos, sin, *, tm, tn):
    m, d = x.shape
    nmod, rows, _ = sc.shape
    n = w.shape[1]
    tiles_per_mod = (m // nmod) // tm
    assert n == 3 * tn, "q|k, v and gate slabs must have equal widths"
    pos_tiles = cos.shape[0] // tm
    mod = _mod_spec(rows, d, tiles_per_mod)
    return pl.pallas_call(
        _odd_proj_kernel,
        grid=(m // tm, 3),
        in_specs=[pl.BlockSpec((tm, d), lambda i, j: (i, 0)), pl.BlockSpec((1, d), lambda i, j: (0, 0)), mod, mod,
                  pl.BlockSpec((d, tn), lambda i, j: (0, j)),
                  pl.BlockSpec((tm, LANES), lambda i, j: (i % pos_tiles, 0)),
                  pl.BlockSpec((tm, LANES), lambda i, j: (i % pos_tiles, 0))],
        out_specs=pl.BlockSpec((tm, tn), lambda i, j: (i, j)),
        out_shape=jax.ShapeDtypeStruct((m, n), BF16),
        scratch_shapes=[pltpu.VMEM((tm, d), BF16)],
        compiler_params=_cparams(("arbitrary", "arbitrary")),
        name="odd_proj",
    )(x, g, sc, sh, w, cos, sin)


def _ret_log_gamma(h):
    return float(np.log(np.float64(1.0) - np.float64(2.0) ** (-5.0 - h)))


def _ret_finish(o, sg, gain):
    mu = jnp.mean(o, axis=-1, keepdims=True)
    d = o - mu
    var = jnp.mean(d * d, axis=-1, keepdims=True)
    r = d * lax.rsqrt(var + EPS) * gain
    return sg.astype(F32) * r


def _ret_prompt_kernel(q_ref, k_ref, v_ref, sg_ref, gain_ref, y_ref, s_out_ref, s_ref, dmask_ref, *, chunk, dk, dv):
    c = pl.program_id(1)

    @pl.when(c == 0)
    def _():
        s_ref[...] = jnp.zeros_like(s_ref)

    @pl.when(jnp.logical_and(pl.program_id(0) == 0, c == 0))
    def _():
        ri = lax.broadcasted_iota(jnp.int32, (chunk, chunk), 0)
        ci = lax.broadcasted_iota(jnp.int32, (chunk, chunk), 1)
        diff = (ri - ci).astype(F32)
        for h in range(RET_HEADS):
            dmask_ref[h] = jnp.where(ri >= ci, jnp.exp(jnp.maximum(diff, 0.0) * _ret_log_gamma(h)), 0.0)

    pos = lax.broadcasted_iota(jnp.int32, (chunk, 1), 0).astype(F32)
    for h in range(RET_HEADS):
        lg = _ret_log_gamma(h)
        dmask = dmask_ref[h]
        q_dec = jnp.exp((pos + 1.0) * lg)
        k_dec = jnp.exp((chunk - 1.0 - pos) * lg)
        c_dec = math.exp(chunk * lg)
        q = q_ref[:, h * dk:(h + 1) * dk]
        k = k_ref[:, h * dk:(h + 1) * dk]
        v = v_ref[:, h * dv:(h + 1) * dv]
        s_prev = s_ref[h]
        a = _dot_nt(q, k) * dmask
        o = (jnp.dot(a.astype(BF16), v, preferred_element_type=F32)
             + jnp.dot(q, s_prev.astype(BF16), preferred_element_type=F32) * q_dec)
        kd = (k.astype(F32) * k_dec).T.astype(BF16)
        s_ref[h] = s_prev * c_dec + jnp.dot(kd, v, preferred_element_type=F32)
        cols = slice(h * dv, (h + 1) * dv)
        y_ref[:, cols] = _ret_finish(o, sg_ref[:, cols], gain_ref[:, cols]).astype(y_ref.dtype)

    @pl.when(c == pl.num_programs(1) - 1)
    def _():
        s_out_ref[...] = s_ref[...]


def _ret_prompt(z, gain, *, batch, seq_len, chunk, dk, dv):
    m = z.shape[0]
    nc = seq_len // chunk
    nk = RET_HEADS * dk
    nv = RET_HEADS * dv
    rowblk = lambda b, c: b * nc + c
    return pl.pallas_call(
        functools.partial(_ret_prompt_kernel, chunk=chunk, dk=dk, dv=dv),
        grid=(batch, nc),
        in_specs=[pl.BlockSpec((chunk, nk), lambda b, c: (rowblk(b, c), 0)),
                  pl.BlockSpec((chunk, nk), lambda b, c: (rowblk(b, c), 1)),
                  pl.BlockSpec((chunk, nv), lambda b, c: (rowblk(b, c), (2 * nk) // nv)),
                  pl.BlockSpec((chunk, nv), lambda b, c: (rowblk(b, c), (2 * nk) // nv + 1)),
                  pl.BlockSpec((1, nv), lambda b, c: (0, 0))],
        out_specs=[pl.BlockSpec((chunk, nv), lambda b, c: (rowblk(b, c), 0)),
                   pl.BlockSpec((None, RET_HEADS, dk, dv), lambda b, c: (b, 0, 0, 0))],
        out_shape=[jax.ShapeDtypeStruct((m, nv), BF16), jax.ShapeDtypeStruct((batch, RET_HEADS, dk, dv), F32)],
        scratch_shapes=[pltpu.VMEM((RET_HEADS, dk, dv), F32), pltpu.VMEM((RET_HEADS, chunk, chunk), F32)],
        compiler_params=_cparams(("arbitrary", "arbitrary")),
        name="ret_prompt",
    )(z, z, z, z, gain)


def _ret_step_kernel(z_ref, s_ref, gain_ref, y_ref, s_out_ref, *, dk, dv):
    nk = RET_HEADS * dk
    nv = RET_HEADS * dv
    row0 = lax.broadcasted_iota(jnp.int32, (LANES, 1), 0) == 0
    for h in range(RET_HEADS):
        gamma = math.exp(_ret_log_gamma(h))
        q = z_ref[:, h * dk:(h + 1) * dk]
        k = z_ref[:, nk + h * dk:nk + (h + 1) * dk]
        v = z_ref[:, 2 * nk + h * dv:2 * nk + (h + 1) * dv]
        sg = z_ref[:, 2 * nk + nv + h * dv:2 * nk + nv + (h + 1) * dv]
        s_prev = s_ref[h]
        q_rows = jnp.broadcast_to(q.astype(F32), (16, dk)).astype(BF16)
        a = jnp.sum(q.astype(F32) * k.astype(F32), axis=-1, keepdims=True)
        o = a * v.astype(F32) + jnp.dot(q_rows, s_prev.astype(BF16), preferred_element_type=F32)[0:1, :] * gamma
        k_rows = jnp.where(row0, jnp.broadcast_to(k.astype(F32), (LANES, dk)), 0.0)
        v_rows = jnp.where(row0, jnp.broadcast_to(v.astype(F32), (LANES, dv)), 0.0)
        s_out_ref[h] = s_prev * gamma + jnp.dot(k_rows.T.astype(BF16), v_rows.astype(BF16),
                                                preferred_element_type=F32)
        cols = slice(h * dv, (h + 1) * dv)
        y_ref[:, cols] = _ret_finish(o, sg, gain_ref[:, cols]).astype(y_ref.dtype)


def _ret_step(z, states, layer, gain, *, dk, dv):
    bd, n = z.shape
    nv = RET_HEADS * dv
    blk = pl.BlockSpec((None, RET_HEADS, dk, dv), lambda b: (b, 0, 0, 0))
    y, s_new = pl.pallas_call(
        functools.partial(_ret_step_kernel, dk=dk, dv=dv),
        grid=(bd,),
        in_specs=[pl.BlockSpec((None, 1, n), lambda b: (b, 0, 0)),
                  pl.BlockSpec((None, None, RET_HEADS, dk, dv), lambda b: (layer, b, 0, 0, 0)),
                  pl.BlockSpec((1, nv), lambda b: (0, 0))],
        out_specs=[pl.BlockSpec((None, 1, nv), lambda b: (b, 0, 0)), blk],
        out_shape=[jax.ShapeDtypeStruct((bd, 1, nv), BF16), jax.ShapeDtypeStruct(states.shape[1:], F32)],
        compiler_params=_cparams(("arbitrary",)),
        name="ret_step",
    )(z.reshape(bd, 1, n), states, gain)
    return y.reshape(bd, nv), s_new


def _fox_decode_kernel(pt_ref, q_ref, qb_ref, kn_ref, vn_ref, lfn_ref, *refs, group):
    del pt_ref
    k_refs = refs[0:group]
    v_refs = refs[group:2 * group]
    lp_refs = refs[2 * group:3 * group]
    o_ref, m_ref, l_ref, w_ref, r_ref, acc_ref = refs[3 * group:]
    c = pl.program_id(1)
    head = lax.broadcasted_iota(jnp.int32, (FOX_HEADS, FOX_DIM), 0)
    chan = lax.broadcasted_iota(jnp.int32, (FOX_HEADS, FOX_DIM), 1)
    own = chan // FOX_HEAD_DIM == head

    def per_head_to_channels(col):
        return jnp.sum(jnp.where(own, jnp.broadcast_to(col, own.shape), 0.0), axis=0, keepdims=True)

    @pl.when(c == 0)
    def _():
        prod = jnp.broadcast_to(q_ref[...] * kn_ref[...], own.shape)
        m_ref[...] = jnp.sum(jnp.where(own, prod, 0.0), axis=-1, keepdims=True)
        l_ref[...] = jnp.ones_like(l_ref)
        w_ref[...] = jnp.ones_like(w_ref)
        r_ref[...] = jnp.zeros_like(r_ref)
        acc_ref[...] = jnp.zeros_like(acc_ref)

    lane = lax.broadcasted_iota(jnp.int32, (1, PAGE_SIZE), 1)
    head_row = lax.broadcasted_iota(jnp.int32, (FOX_HEADS, 1), 0)
    lfn = lfn_ref[...]
    later = r_ref[...]
    scores = []
    for g in range(group):
        lp = lp_refs[g][...]
        suffix = lp
        stride = 1
        while stride < PAGE_SIZE:
            ahead = pltpu.roll(suffix, PAGE_SIZE - stride, axis=1)
            suffix = suffix + jnp.where(lane < PAGE_SIZE - stride, ahead, 0.0)
            stride *= 2
        s = lfn + later + (suffix - lp)
        later = later + jnp.sum(lp, axis=-1, keepdims=True)
        for h in range(FOX_HEADS):
            qk = jnp.sum(k_refs[g][h] * qb_ref[h], axis=0, keepdims=True)
            s = s + jnp.where(head_row == h, jnp.broadcast_to(qk, s.shape), 0.0)
        scores.append(s)
    r_ref[...] = later
    m_prev = m_ref[...]
    m_new = m_prev
    for s in scores:
        m_new = jnp.maximum(m_new, jnp.max(s, axis=-1, keepdims=True))
    alpha = jnp.exp(m_prev - m_new)
    l_new = alpha * l_ref[...]
    probs = []
    for s in scores:
        p = jnp.exp(s - m_new)
        l_new = l_new + jnp.sum(p, axis=-1, keepdims=True)
        probs.append(p)
    for h in range(FOX_HEADS):
        acc = acc_ref[h] * alpha[h:h + 1, :]
        for g in range(group):
            acc = acc + v_refs[g][h] * probs[g][h:h + 1, :]
        acc_ref[h] = acc
    m_ref[...] = m_new
    l_ref[...] = l_new
    w_ref[...] = alpha * w_ref[...]

    @pl.when(c == pl.num_programs(1) - 1)
    def _():
        acc_t = acc_ref[...].reshape(FOX_DIM, PAGE_SIZE).T
        pv = jnp.sum(acc_t, axis=0, keepdims=True)
        o = (pv + per_head_to_channels(w_ref[...]) * vn_ref[...]) / per_head_to_channels(l_ref[...])
        o_ref[...] = o.astype(o_ref.dtype)


def _fox_decode(q, k_new, v_new, logf_new, cache_kt, cache_vt, cache_lpt, page_table, layer, *, group):
    bd, c = q.shape
    n_pages = page_table.shape[1]
    group = min(group, n_pages)
    assert n_pages % group == 0
    pt_flat = page_table.reshape(-1)
    q = q.astype(F32)
    q_lanes = jnp.broadcast_to(q.reshape(bd, FOX_HEADS, FOX_HEAD_DIM, 1), (bd, FOX_HEADS, FOX_HEAD_DIM, PAGE_SIZE))

    def page_spec(g, shape):
        def index(b, s, pt):
            return (layer, pt[b * n_pages + (n_pages - 1 - (s * group + g))]) + (0,) * len(shape)
        return pl.BlockSpec((None, None) + shape, index)

    row = pl.BlockSpec((None, 1, c), lambda b, s, pt: (b, 0, 0))
    slab = (FOX_HEADS, FOX_HEAD_DIM, PAGE_SIZE)
    in_specs = ([row, pl.BlockSpec((None,) + slab, lambda b, s, pt: (b, 0, 0, 0)), row, row,
                 pl.BlockSpec((None, FOX_HEADS, 1), lambda b, s, pt: (b, 0, 0))]
                + [page_spec(g, slab) for g in range(group)]
                + [page_spec(g, slab) for g in range(group)]
                + [page_spec(g, (FOX_HEADS, PAGE_SIZE)) for g in range(group)])
    col = pltpu.VMEM((FOX_HEADS, 1), F32)
    out = pl.pallas_call(
        functools.partial(_fox_decode_kernel, group=group),
        grid_spec=pltpu.PrefetchScalarGridSpec(
            num_scalar_prefetch=1,
            grid=(bd, n_pages // group),
            in_specs=in_specs,
            out_specs=row,
            scratch_shapes=[col, col, col, col, pltpu.VMEM(slab, F32)]),
        out_shape=jax.ShapeDtypeStruct((bd, 1, c), BF16),
        compiler_params=_cparams(("arbitrary", "arbitrary")),
        name="fox_decode",
    )(pt_flat, q.reshape(bd, 1, c), q_lanes, k_new.reshape(bd, 1, c), v_new.reshape(bd, 1, c),
      logf_new.reshape(bd, FOX_HEADS, 1), *([cache_kt] * group), *([cache_vt] * group), *([cache_lpt] * group))
    return out.reshape(bd, c)


def _rope_tables(pos, half):
    inv = 1.0 / (ROPE_BASE ** jnp.linspace(0.0, 1.0, half, dtype=F32))
    ang = pos.astype(F32)[:, None] * inv[None, :]
    return jnp.cos(ang), jnp.sin(ang)


def _prep_weights(w_in_even, b_forget, conv_w, w_out_even, w_in_odd, w_ffn_gate, w_ffn_up, w_ffn_down, dk):
    c = FOX_DIM
    n_even, d, _ = w_in_even.shape
    qkv_scale = jnp.where(jnp.arange(3 * c) < c, FOX_HEAD_DIM ** -0.5, 1.0).astype(F32)
    w_qkv = (w_in_even[:, :, :3 * c] * qkv_scale).astype(BF16)
    w_ab = w_in_even[:, :, 3 * c + FOX_HEADS:].astype(BF16)
    w_f = jnp.pad(w_in_even[:, :, 3 * c:3 * c + FOX_HEADS], ((0, 0), (0, 0), (0, LANES - FOX_HEADS))).astype(BF16)
    bf_pad = jnp.pad(b_forget, ((0, 0), (0, LANES - FOX_HEADS)))[:, None, :]
    cw_pad = jnp.pad(conv_w, ((0, 0), (0, CONV_HALO - CONV_WIDTH), (0, 0)))
    nk = RET_HEADS * dk
    col = jnp.arange(w_in_odd.shape[-1])
    odd_scale = jnp.where((col >= nk) & (col < 2 * nk), dk ** -0.5, 1.0).astype(F32)
    w_odd = (w_in_odd * odd_scale).astype(BF16)
    return dict(w_even=(w_qkv, w_ab, w_f), bf_pad=bf_pad, cw_pad=cw_pad,
                w_out_fox=w_out_even[:, :c].astype(BF16), w_out_conv=w_out_even[:, c:].astype(BF16),
                w_odd=w_odd, wg=w_ffn_gate.astype(BF16), wu=w_ffn_up.astype(BF16), wd=w_ffn_down.astype(BF16))


def _tiles(rows, seq_len):
    if seq_len == 1:
        return dict(tm=rows, tm_ffn=rows, th=1408, tm_odd=rows)
    return dict(tm=min(512, seq_len), tm_ffn=min(512, seq_len), th=1408, tm_odd=min(1024, seq_len))


def _trunk(x, mod, pos, P, W, *, seq_len, even_mixer, odd_mixer):
    depth = mod.shape[0]
    d = x.shape[1]
    t = _tiles(x.shape[0], seq_len)
    tm = t['tm']
    cos, sin = _rope_tables(pos, LANES)
    ks, vs, lfs, cbs, rs = [], [], [], [], []
    for l in range(depth):
        sh1, sc1, g1, sh2, sc2, g2 = [mod[l][..., s * d:(s + 1) * d] for s in range(6)]
        gm = W['norm_mix_g'][l][None, :]
        if l % 2 == 0:
            e = l // 2
            out = _even_proj(x, gm, sc1, sh1, [w[e] for w in P['w_even']], P['bf_pad'][e], tm=tm, tq=min(FOX_TQ, tm),
                             seq_len=seq_len, with_cum=even_mixer.needs_cum)
            q, k, v, logf, u = out[:5]
            o_fox, cv, new_buf = even_mixer(e, q, k, v, logf, u, out[5:])
            ks.append(k); vs.append(v); lfs.append(logf); cbs.append(new_buf)
            acts, w_outs = [o_fox, cv], [P['w_out_fox'][e], P['w_out_conv'][e]]
        else:
            o = l // 2
            z = _odd_proj(x, gm, sc1, sh1, P['w_odd'][o], cos, sin, tm=t['tm_odd'], tn=P['w_odd'].shape[-1] // 3)
            y, s_new = odd_mixer(o, z)
            rs.append(s_new)
            acts, w_outs = [y], [P['w_out_odd'][o]]
        x = _mix_ffn(acts, w_outs, x, g1, W['norm_ffn_g'][l][None, :], sc2, sh2, g2, P['wg'][l], P['wu'][l],
                     P['wd'][l], W['final_norm_g'][None, :], tm=t['tm_ffn'], th=t['th'], final=(l == depth - 1))
    return x, (ks, vs, lfs, cbs, rs)


class _PromptEven:
    needs_cum = True

    def __init__(self, batch, seq_len, W, P):
        self.batch, self.seq_len, self.W, self.P = batch, seq_len, W, P

    def __call__(self, e, q, k, v, logf, u, extra):
        kb, vt, cum, cumt = extra
        W, P = self.W, self.P
        o_fox = _fox_prompt(q, kb, vt, cum, cumt, batch=self.batch, seq_len=self.seq_len, tq=FOX_TQ)
        cv = _conv_prompt(u, P['cw_pad'][e], W['conv_b'][e][None, :], W['conv_norm_g'][e][None, :],
                          W['conv_norm_b'][e][None, :], batch=self.batch, seq_len=self.seq_len, tl=256)
        c = u.shape[1]
        new_buf = u.reshape(self.batch, self.seq_len, c)[:, self.seq_len - (CONV_WIDTH - 1):]
        return o_fox, cv, new_buf


class _SampleEven:
    needs_cum = False

    def __init__(self, W, P, cache_k, cache_v, cache_lp, state_conv, page_table):
        self.W, self.P = W, P
        self.cache_k, self.cache_v, self.cache_lp = cache_k, cache_v, cache_lp
        self.state_conv, self.page_table = state_conv, page_table

    def __call__(self, e, q, k, v, logf, u, extra):
        W, P = self.W, self.P
        o_fox = _fox_decode(q, k, v, logf, self.cache_k, self.cache_v, self.cache_lp, self.page_table, e, group=16)
        st = self.state_conv[e]
        cv = _conv_step(jnp.transpose(st, (1, 0, 2)), u, P['cw_pad'][e], W['conv_b'][e][None, :],
                        W['conv_norm_g'][e][None, :], W['conv_norm_b'][e][None, :])
        new_buf = jnp.concatenate([st[:, 1:], u[:, None, :]], axis=1)
        return o_fox, cv, new_buf


def kernel(x_prompt, x_sample, cache_k, cache_v, cache_logf, state_conv, state_ret, page_table, c_prompt, c_sample,
           ada_w, ada_b, norm_mix_g, norm_ffn_g, w_in_even, b_forget, conv_w, conv_b, conv_norm_g, conv_norm_b,
           w_out_even, w_in_odd, ret_norm_g, w_out_odd, w_ffn_gate, w_ffn_up, w_ffn_down, final_norm_g):
    bp, lp, d = x_prompt.shape
    bs, ls, _ = x_sample.shape
    assert ls == 1, "the decode path handles one new token per sequence"
    depth = ada_w.shape[0]
    n_even, n_phys, page, heads, hd = cache_k.shape
    assert (page, heads, hd) == (PAGE_SIZE, FOX_HEADS, FOX_HEAD_DIM)
    dk, dv = state_ret.shape[-2:]
    past_len = page_table.shape[1] * PAGE_SIZE

    W = dict(norm_mix_g=norm_mix_g, norm_ffn_g=norm_ffn_g, conv_b=conv_b, conv_norm_g=conv_norm_g,
             conv_norm_b=conv_norm_b, final_norm_g=final_norm_g)
    P = _prep_weights(w_in_even, b_forget, conv_w, w_out_even, w_in_odd, w_ffn_gate, w_ffn_up, w_ffn_down, dk)
    P['w_out_odd'] = w_out_odd.astype(BF16)
    gain = ret_norm_g[:, None, :]

    mod = _adaln(jnp.concatenate([c_prompt, c_sample], axis=0), ada_w, ada_b)
    mod_p = mod[:, :bp].reshape(depth, bp, 1, 6 * d)
    mod_s = mod[:, bp:].reshape(depth, 1, bs, 6 * d)

    def odd_prompt(o, z):
        return _ret_prompt(z, gain[o], batch=bp, seq_len=lp, chunk=min(256, lp), dk=dk, dv=dv)

    y_p, (k_p, v_p, lf_p, cb_p, r_p) = _trunk(
        x_prompt.reshape(bp * lp, d), mod_p, jnp.arange(lp), P, W, seq_len=lp,
        even_mixer=_PromptEven(bp, lp, W, P), odd_mixer=odd_prompt)

    ckt = jnp.transpose(cache_k, (0, 1, 3, 4, 2))
    cvt = jnp.transpose(cache_v, (0, 1, 3, 4, 2))
    clp = jnp.transpose(cache_logf, (0, 1, 3, 2))

    def odd_sample(o, z):
        return _ret_step(z, state_ret, o, gain[o], dk=dk, dv=dv)

    y_s, (k_s, v_s, lf_s, cb_s, r_s) = _trunk(
        x_sample.reshape(bs, d), mod_s, jnp.full((bs,), past_len, jnp.int32), P, W, seq_len=1,
        even_mixer=_SampleEven(W, P, ckt, cvt, clp, state_conv, page_table), odd_mixer=odd_sample)

    hshape = (FOX_HEADS, FOX_HEAD_DIM)

    def seq_last_to_heads(ts):
        return jnp.transpose(jnp.stack(ts).reshape((n_even, bp) + hshape + (lp,)), (0, 1, 4, 2, 3))

    return (y_p.reshape(bp, lp, d), y_s.reshape(bs, ls, d),
            seq_last_to_heads(k_p), seq_last_to_heads(v_p),
            jnp.transpose(jnp.stack(lf_p), (0, 1, 3, 2)), jnp.stack(cb_p), jnp.stack(r_p),
            jnp.stack(k_s).reshape((n_even, bs, ls) + hshape), jnp.stack(v_s).reshape((n_even, bs, ls) + hshape),
            jnp.stack(lf_s).reshape(n_even, bs, ls, FOX_HEADS), jnp.stack(cb_s),
            jnp.stack(r_s).astype(state_ret.dtype))
```

```python
import functools
import math

import jax
import jax.numpy as jnp
import numpy as np
from jax import lax
from jax.experimental import pallas as pl
from jax.experimental.pallas import tpu as pltpu

F32 = jnp.float32
BF16 = jnp.bfloat16

EPS = 1e-6
ROPE_BASE = 10000.0
FOX_HEADS = 8
FOX_HEAD_DIM = 64
FOX_DIM = FOX_HEADS * FOX_HEAD_DIM
CONV_GROUPS = 8
CONV_WIDTH = 31
RET_HEADS = 4
PAGE_SIZE = 128
LANES = 128
SUBLANES = 8
CONV_HALO = 32
VMEM_LIMIT = 56 * 1024 * 1024
NEG_BIG = -1e30
LOG2E = math.log2(math.e)
FOX_TQ = 256


def _cparams(sem):
    return pltpu.CompilerParams(dimension_semantics=sem, vmem_limit_bytes=VMEM_LIMIT)


def _silu(x):
    return x * jax.nn.sigmoid(x)


def _log_sigmoid(x):
    return jnp.minimum(x, 0.0) - jnp.log1p(jnp.exp(-jnp.abs(x)))


def _norm_mod(x, g, sc, sh):
    ms = jnp.mean(x * x, axis=-1, keepdims=True)
    y = x * lax.rsqrt(ms + EPS) * g
    return y * (1.0 + sc) + sh


def _split3(x):
    hi = x.astype(BF16)
    r1 = x - hi.astype(F32)
    mid = r1.astype(BF16)
    lo = (r1 - mid.astype(F32)).astype(BF16)
    return hi, mid, lo


def _dot3(x, w_bf16):
    hi, mid, lo = _split3(x)
    d = functools.partial(jnp.dot, preferred_element_type=F32)
    return d(hi, w_bf16) + d(mid, w_bf16) + d(lo, w_bf16)


def _dot_nt(a, b):
    return lax.dot_general(a, b, (((1,), (1,)), ((), ())), preferred_element_type=F32)


def _mod_spec(rows, d, tiles_per_mod):
    return pl.BlockSpec((None, rows, d), lambda i, *_: (i // tiles_per_mod, 0, 0))


def _adaln_kernel(c_ref, w_ref, b_ref, o_ref):
    cm = _silu(c_ref[...]).astype(BF16)
    o_ref[...] = jnp.dot(cm, w_ref[...].astype(BF16), preferred_element_type=F32) + b_ref[...]


def _adaln(c_all, ada_w, ada_b, tn=1536):
    depth, d, n = ada_w.shape
    r = c_all.shape[0]
    return pl.pallas_call(
        _adaln_kernel,
        grid=(depth, n // tn),
        in_specs=[pl.BlockSpec((r, d), lambda l, j: (0, 0)),
                  pl.BlockSpec((None, d, tn), lambda l, j: (l, 0, j)),
                  pl.BlockSpec((None, 1, tn), lambda l, j: (l, 0, j))],
        out_specs=pl.BlockSpec((None, r, tn), lambda l, j: (l, 0, j)),
        out_shape=jax.ShapeDtypeStruct((depth, r, n), F32),
        compiler_params=_cparams(("arbitrary", "arbitrary")),
        name="adaln",
    )(c_all, ada_w, ada_b.reshape(depth, 1, n))


def _even_proj_kernel(x_ref, g_ref, sc_ref, sh_ref, wqkv_ref, wab_ref, wf_ref, bf_ref, *refs,
                      tm, tq, tiles_per_seq, with_cum, n_carried):
    refs = refs[n_carried:]
    if with_cum:
        q_ref, k_ref, v_ref, lf_ref, u_ref, kb_ref, vt_ref, cum_ref, cumt_ref, carry_ref = refs
    else:
        q_ref, k_ref, v_ref, lf_ref, u_ref = refs
    hm = _norm_mod(x_ref[...], g_ref[...], sc_ref[...], sh_ref[...]).astype(BF16)
    c = FOX_DIM
    qkv = jnp.dot(hm, wqkv_ref[...], preferred_element_type=F32)
    q_ref[...] = qkv[:, 0:c].astype(BF16)
    k = qkv[:, c:2 * c]
    v = qkv[:, 2 * c:3 * c]
    ab = jnp.dot(hm, wab_ref[...], preferred_element_type=F32)
    u_ref[...] = ab[:, 0:c] * jax.nn.sigmoid(ab[:, c:2 * c])
    fl = jnp.dot(hm, wf_ref[...], preferred_element_type=F32)
    logf = _log_sigmoid(fl + bf_ref[...])
    if not with_cum:
        k_ref[...] = k
        v_ref[...] = v
        lf_ref[...] = logf[:, 0:FOX_HEADS]
    else:
        v_t = v.T
        k_ref[...] = k.T
        v_ref[...] = v_t
        lf_ref[...] = logf.T[0:FOX_HEADS, :]
        kb_ref[...] = (k * LOG2E).astype(BF16)
        for r in range(tm // tq):
            vt_ref[r] = v_t[:, r * tq:(r + 1) * tq].astype(BF16)
        i = pl.program_id(0)

        @pl.when(i % tiles_per_seq == 0)
        def _():
            carry_ref[...] = jnp.zeros_like(carry_ref)

        row = lax.broadcasted_iota(jnp.int32, (tm, tm), 0)
        col = lax.broadcasted_iota(jnp.int32, (tm, tm), 1)
        tri = (col <= row).astype(BF16)
        cum = _dot3_left(tri, logf) + carry_ref[...]
        carry_ref[...] = cum[tm - 1:tm, :]
        cum = cum * LOG2E
        cum_ref[...] = cum[:, 0:FOX_HEADS]
        cum_t = cum.T
        for r in range(tm // tq):
            cumt_ref[r] = cum_t[0:FOX_HEADS, r * tq:(r + 1) * tq]


def _dot3_left(w_bf16, x):
    hi, mid, lo = _split3(x)
    d = functools.partial(jnp.dot, preferred_element_type=F32)
    return d(w_bf16, hi) + d(w_bf16, mid) + d(w_bf16, lo)


def _even_proj(x, g, sc, sh, weights, bf_pad, *, tm, tq, seq_len, with_cum, layer=0, n_layers=1, carried=()):
    m, d = x.shape
    nmod, rows, _ = sc.shape
    tiles_per_mod = (m // nmod) // tm
    c = FOX_DIM
    row_spec = lambda width: pl.BlockSpec((tm, width), lambda i: (i, 0))
    tiles_per_seq = max(seq_len // tm, 1)
    if with_cum:
        nseq = m // seq_len
        stacked = lambda width: (
            jax.ShapeDtypeStruct((n_layers, nseq, width, seq_len), F32),
            pl.BlockSpec((None, None, width, tm), lambda i: (layer, i // tiles_per_seq, 0, i % tiles_per_seq)))
        kvl = [stacked(c), stacked(c), stacked(FOX_HEADS)]
    else:
        kvl = [(jax.ShapeDtypeStruct((m, w_), F32), row_spec(w_)) for w_ in (c, c, FOX_HEADS)]
    out_shape = [jax.ShapeDtypeStruct((m, c), BF16)] + [s for s, _ in kvl] + [jax.ShapeDtypeStruct((m, c), F32)]
    out_specs = [row_spec(c)] + [s for _, s in kvl] + [row_spec(c)]
    scratch = []
    if with_cum:
        out_shape += [jax.ShapeDtypeStruct((m, c), BF16), jax.ShapeDtypeStruct((m // tq, c, tq), BF16),
                      jax.ShapeDtypeStruct((m, FOX_HEADS), F32), jax.ShapeDtypeStruct((m // tq, FOX_HEADS, tq), F32)]
        out_specs += [row_spec(c), pl.BlockSpec((tm // tq, c, tq), lambda i: (i, 0, 0)),
                      row_spec(FOX_HEADS), pl.BlockSpec((tm // tq, FOX_HEADS, tq), lambda i: (i, 0, 0))]
        scratch = [pltpu.VMEM((1, LANES), F32)]
    n_fixed = 5 + len(weights)
    return pl.pallas_call(
        functools.partial(_even_proj_kernel, tm=tm, tq=tq, tiles_per_seq=tiles_per_seq, with_cum=with_cum,
                          n_carried=len(carried)),
        grid=(m // tm,),
        in_specs=[row_spec(d), pl.BlockSpec((1, d), lambda i: (0, 0)),
                  _mod_spec(rows, d, tiles_per_mod), _mod_spec(rows, d, tiles_per_mod),
                  *[pl.BlockSpec(w.shape, lambda i: (0, 0)) for w in weights],
                  pl.BlockSpec((1, LANES), lambda i: (0, 0)),
                  *[pl.BlockSpec(memory_space=pl.ANY) for _ in carried]],
        out_specs=out_specs,
        out_shape=out_shape,
        scratch_shapes=scratch,
        input_output_aliases={n_fixed + t: 1 + t for t in range(len(carried))},
        compiler_params=_cparams(("arbitrary",)),
        name="even_proj",
    )(x, g, sc, sh, *weights, bf_pad, *carried)


def _fox_prompt_kernel(q_ref, k_ref, vt_ref, ck_ref, cq_ref, o_ref, qm_ref, m_ref, acc_ref, *, tq):
    i = pl.program_id(1)
    hd = FOX_HEAD_DIM
    low = lax.broadcasted_iota(jnp.int32, (1, LANES), 1) < hd
    for pair in range(FOX_HEADS // 2):
        qp = q_ref[:, pair * LANES:(pair + 1) * LANES].astype(F32)
        qm_ref[2 * pair] = jnp.where(low, qp, 0.0).astype(BF16)
        qm_ref[2 * pair + 1] = jnp.where(low, 0.0, qp).astype(BF16)
    m_ref[...] = jnp.full_like(m_ref, NEG_BIG)
    acc_ref[...] = jnp.zeros_like(acc_ref)
    key = lax.broadcasted_iota(jnp.int32, (tq, tq), 0)
    qry = lax.broadcasted_iota(jnp.int32, (tq, tq), 1)
    causal = key <= qry
    ones = jnp.ones((acc_ref.shape[1] - hd, tq), BF16)

    def scores(j):
        start = pl.multiple_of(j * tq, tq)
        qk = []
        for h in range(FOX_HEADS):
            pair = h // 2
            kj = k_ref[pl.ds(start, tq), pair * LANES:(pair + 1) * LANES]
            qk.append(_dot_nt(kj, qm_ref[h]))
        return qk

    def softmax_pv(j, qk, masked):
        start = pl.multiple_of(j * tq, tq)
        cq_all = cq_ref[i]
        for h in range(FOX_HEADS):
            t = qk[h] - ck_ref[pl.ds(start, tq), h:h + 1]
            if masked:
                t = jnp.where(causal, t, NEG_BIG)
            cq = cq_all[h:h + 1, :]
            m_prev = m_ref[h:h + 1, :]
            m_new = jnp.maximum(m_prev, jnp.max(t, axis=0, keepdims=True) + cq)
            p = jnp.exp2(t + (cq - m_new)).astype(BF16)
            alpha = jnp.exp2(m_prev - m_new)
            v_ones = jnp.concatenate([vt_ref[j][h * hd:(h + 1) * hd, :], ones], axis=0)
            acc_ref[h] = alpha * acc_ref[h] + jnp.dot(v_ones, p, preferred_element_type=F32)
            m_ref[h:h + 1, :] = m_new

    def steps(blocks):
        qks = [scores(j) for j, _ in blocks]
        for (j, masked), qk in zip(blocks, qks):
            softmax_pv(j, qk, masked)

    def body(jj, carry):
        steps([(2 * jj, False), (2 * jj + 1, False)])
        return carry

    lax.fori_loop(0, i // 2, body, 0)

    @pl.when(i % 2 == 1)
    def _():
        steps([(i - 1, False), (i, True)])

    @pl.when(i % 2 == 0)
    def _():
        steps([(i, True)])
    for pair in range(FOX_HEADS // 2):
        halves = []
        for h in (2 * pair, 2 * pair + 1):
            halves.append(acc_ref[h, 0:hd, :] * (1.0 / acc_ref[h, hd:hd + 1, :]))
        o_ref[:, pair * LANES:(pair + 1) * LANES] = jnp.concatenate(halves, axis=0).T.astype(o_ref.dtype)


def _fox_prompt(q, kb, vt, cum, cumt, *, batch, seq_len, tq):
    m, c = q.shape
    nq = seq_len // tq
    return pl.pallas_call(
        functools.partial(_fox_prompt_kernel, tq=tq),
        grid=(batch, nq),
        in_specs=[pl.BlockSpec((tq, c), lambda b, i: (b * nq + i, 0)),
                  pl.BlockSpec((seq_len, c), lambda b, i: (b, 0)),
                  pl.BlockSpec((nq, c, tq), lambda b, i: (b, 0, 0)),
                  pl.BlockSpec((seq_len, FOX_HEADS), lambda b, i: (b, 0)),
                  pl.BlockSpec((nq, FOX_HEADS, tq), lambda b, i: (b, 0, 0))],
        out_specs=pl.BlockSpec((tq, c), lambda b, i: (b * nq + i, 0)),
        out_shape=jax.ShapeDtypeStruct((m, c), BF16),
        scratch_shapes=[pltpu.VMEM((FOX_HEADS, tq, LANES), BF16), pltpu.VMEM((FOX_HEADS, tq), F32),
                        pltpu.VMEM((FOX_HEADS, FOX_HEAD_DIM + 16, tq), F32)],
        compiler_params=_cparams(("arbitrary", "arbitrary")),
        name="fox_prompt",
    )(q, kb, vt, cum, cumt)


def _group_matrices(channels, groups):
    gsz = channels // groups
    ch = lax.broadcasted_iota(jnp.int32, (channels, LANES), 0)
    gr = lax.broadcasted_iota(jnp.int32, (channels, LANES), 1)
    gather = (ch // gsz == gr).astype(BF16)
    gr_t = lax.broadcasted_iota(jnp.int32, (LANES, channels), 0)
    ch_t = lax.broadcasted_iota(jnp.int32, (LANES, channels), 1)
    spread = (ch_t // gsz == gr_t).astype(BF16)
    return gather, spread, 1.0 / gsz


def _group_norm_rows(y, groups):
    gather, spread, inv = _group_matrices(y.shape[-1], groups)
    mu = _dot3(y, gather) * inv
    d = y - _dot3(mu, spread)
    var = _dot3(d * d, gather) * inv
    return d * _dot3(lax.rsqrt(var + EPS), spread)


def _conv_finish(acc, cb, gn_g, gn_b):
    cv = _group_norm_rows(acc + cb, CONV_GROUPS) * gn_g + gn_b
    return _silu(cv)


def _conv_prompt_kernel(u_ref, halo_ref, cw_ref, cb_ref, gg_ref, gb_ref, o_ref, xp_ref, sh_ref, y_ref, *, tl):
    i = pl.program_id(1)
    halo = halo_ref[...]
    xp_ref[0:CONV_HALO, :] = jnp.where(i == 0, jnp.zeros_like(halo), halo)
    xp_ref[CONV_HALO:CONV_HALO + tl, :] = u_ref[...]
    span = sh_ref.shape[1]
    for r in range(1, SUBLANES):
        sh_ref[r - 1] = xp_ref[r:r + span, :]
    base = CONV_HALO - (CONV_WIDTH - 1)
    rows = 64
    for r0 in range(0, tl, rows):
        for c0 in range(0, u_ref.shape[1], LANES):
            acc = jnp.zeros((rows, LANES), F32)
            for w in range(CONV_WIDTH):
                shift = (base + w) % SUBLANES
                start = r0 + base + w - shift
                if shift == 0:
                    tap = xp_ref[start:start + rows, c0:c0 + LANES]
                else:
                    tap = sh_ref[shift - 1, start:start + rows, c0:c0 + LANES]
                acc = acc + tap * cw_ref[w:w + 1, c0:c0 + LANES]
            y_ref[r0:r0 + rows, c0:c0 + LANES] = acc
    y = _conv_finish(y_ref[...], cb_ref[...], gg_ref[...], gb_ref[...])
    o_ref[...] = y.astype(o_ref.dtype)


def _conv_prompt(u, cw, cb, gn_g, gn_b, *, batch, seq_len, tl):
    m, c = u.shape
    nl = seq_len // tl
    hb = tl // CONV_HALO
    vec = pl.BlockSpec((1, c), lambda b, i: (0, 0))
    return pl.pallas_call(
        functools.partial(_conv_prompt_kernel, tl=tl),
        grid=(batch, nl),
        in_specs=[pl.BlockSpec((tl, c), lambda b, i: (b * nl + i, 0)),
                  pl.BlockSpec((CONV_HALO, c), lambda b, i: (jnp.maximum((b * nl + i) * hb - 1, 0), 0)),
                  pl.BlockSpec((CONV_HALO, c), lambda b, i: (0, 0)), vec, vec, vec],
        out_specs=pl.BlockSpec((tl, c), lambda b, i: (b * nl + i, 0)),
        out_shape=jax.ShapeDtypeStruct((m, c), BF16),
        scratch_shapes=[pltpu.VMEM((CONV_HALO + tl, c), F32),
                        pltpu.VMEM((SUBLANES - 1, CONV_HALO - SUBLANES + tl, c), F32), pltpu.VMEM((tl, c), F32)],
        compiler_params=_cparams(("arbitrary", "arbitrary")),
        name="conv_prompt",
    )(u, u, cw, cb, gn_g, gn_b)


def _conv_step_kernel(st_ref, u_ref, cw_ref, cb_ref, gg_ref, gb_ref, o_ref):
    acc = u_ref[...] * cw_ref[CONV_WIDTH - 1:CONV_WIDTH, :]
    for w in range(CONV_WIDTH - 1):
        acc = acc + st_ref[w] * cw_ref[w:w + 1, :]
    o_ref[...] = _conv_finish(acc, cb_ref[...], gg_ref[...], gb_ref[...]).astype(o_ref.dtype)


def _conv_step(state_t, u, cw, cb, gn_g, gn_b):
    bd, c = u.shape
    full = lambda shape: pl.BlockSpec(shape, lambda i: (0,) * len(shape))
    return pl.pallas_call(
        _conv_step_kernel,
        grid=(1,),
        in_specs=[full(state_t.shape), full((bd, c)), full(cw.shape), full((1, c)), full((1, c)), full((1, c))],
        out_specs=full((bd, c)),
        out_shape=jax.ShapeDtypeStruct((bd, c), BF16),
        compiler_params=_cparams(("arbitrary",)),
        name="conv_step",
    )(state_t, u, cw, cb, gn_g, gn_b)


def _mix_ffn_kernel(*refs, n_in, final):
    a_refs = refs[:n_in]
    w_refs = refs[n_in:2 * n_in]
    (x_ref, g1_ref, g_ref, sc_ref, sh_ref, g2_ref, wg_ref, wu_ref, wd_ref, fg_ref,
     o_ref, x1_ref, hf_ref, acc_ref) = refs[2 * n_in:]
    j = pl.program_id(1)

    @pl.when(j == 0)
    def _():
        mix = jnp.dot(a_refs[0][...], w_refs[0][...], preferred_element_type=F32)
        for a_ref, w_ref in zip(a_refs[1:], w_refs[1:]):
            mix = mix + jnp.dot(a_ref[...], w_ref[...], preferred_element_type=F32)
        x1 = x_ref[...] + g1_ref[...] * mix
        x1_ref[...] = x1
        hf_ref[...] = _norm_mod(x1, g_ref[...], sc_ref[...], sh_ref[...]).astype(BF16)
        acc_ref[...] = jnp.zeros_like(acc_ref)

    hf = hf_ref[...]
    gt = jnp.dot(hf, wg_ref[...], preferred_element_type=F32)
    up = jnp.dot(hf, wu_ref[...], preferred_element_type=F32)
    acc_ref[...] += jnp.dot((_silu(gt) * up).astype(BF16), wd_ref[...], preferred_element_type=F32)

    @pl.when(j == pl.num_programs(1) - 1)
    def _():
        y = x1_ref[...] + g2_ref[...] * acc_ref[...]
        if final:
            ms = jnp.mean(y * y, axis=-1, keepdims=True)
            y = y * lax.rsqrt(ms + EPS) * fg_ref[...]
        o_ref[...] = y


def _mix_ffn(acts, w_outs, x, g1, g, sc, sh, g2, wg, wu, wd, fg, *, tm, th, final):
    m, d = x.shape
    nmod, rows, _ = sc.shape
    hdim = wg.shape[1]
    tiles_per_mod = (m // nmod) // tm
    vec = pl.BlockSpec((1, d), lambda i, j: (0, 0))
    mod = _mod_spec(rows, d, tiles_per_mod)
    in_specs = ([pl.BlockSpec((tm, a.shape[1]), lambda i, j: (i, 0)) for a in acts]
                + [pl.BlockSpec(w.shape, lambda i, j: (0, 0)) for w in w_outs]
                + [pl.BlockSpec((tm, d), lambda i, j: (i, 0)), mod, vec, mod, mod, mod,
                   pl.BlockSpec((d, th), lambda i, j: (0, j)), pl.BlockSpec((d, th), lambda i, j: (0, j)),
                   pl.BlockSpec((th, d), lambda i, j: (j, 0)), vec])
    return pl.pallas_call(
        functools.partial(_mix_ffn_kernel, n_in=len(acts), final=final),
        grid=(m // tm, hdim // th),
        in_specs=in_specs,
        out_specs=pl.BlockSpec((tm, d), lambda i, j: (i, 0)),
        out_shape=jax.ShapeDtypeStruct((m, d), F32),
        scratch_shapes=[pltpu.VMEM((tm, d), F32), pltpu.VMEM((tm, d), BF16), pltpu.VMEM((tm, d), F32)],
        compiler_params=_cparams(("arbitrary", "arbitrary")),
        name="mix_ffn",
    )(*acts, *w_outs, x, g1, g, sc, sh, g2, wg, wu, wd, fg)


def _odd_proj_kernel(x_ref, g_ref, sc_ref, sh_ref, w_ref, cos_ref, sin_ref, o_ref, hm_ref):
    j = pl.program_id(1)

    @pl.when(j == 0)
    def _():
        hm_ref[...] = _norm_mod(x_ref[...], g_ref[...], sc_ref[...], sh_ref[...]).astype(BF16)

    def slab():
        return jnp.dot(hm_ref[...], w_ref[...], preferred_element_type=F32)

    @pl.when(j == 0)
    def _():
        z = slab()
        cos = cos_ref[...]
        sin = sin_ref[...]
        for c0 in range(0, z.shape[1], 2 * LANES):
            x1 = z[:, c0:c0 + LANES]
            x2 = z[:, c0 + LANES:c0 + 2 * LANES]
            o_ref[:, c0:c0 + LANES] = (x1 * cos - x2 * sin).astype(o_ref.dtype)
            o_ref[:, c0 + LANES:c0 + 2 * LANES] = (x1 * sin + x2 * cos).astype(o_ref.dtype)

    @pl.when(j == 1)
    def _():
        o_ref[...] = slab().astype(o_ref.dtype)

    @pl.when(j == 2)
    def _():
        o_ref[...] = _silu(slab()).astype(o_ref.dtype)


def _odd_proj(x, g, sc, sh, w, cos, sin, *, tm, tn):
    m, d = x.shape
    nmod, rows, _ = sc.shape
    n = w.shape[1]
    tiles_per_mod = (m // nmod) // tm
    assert n == 3 * tn, "q|k, v and gate slabs must have equal widths"
    pos_tiles = cos.shape[0] // tm
    mod = _mod_spec(rows, d, tiles_per_mod)
    return pl.pallas_call(
        _odd_proj_kernel,
        grid=(m // tm, 3),
        in_specs=[pl.BlockSpec((tm, d), lambda i, j: (i, 0)), pl.BlockSpec((1, d), lambda i, j: (0, 0)), mod, mod,
                  pl.BlockSpec((d, tn), lambda i, j: (0, j)),
                  pl.BlockSpec((tm, LANES), lambda i, j: (i % pos_tiles, 0)),
                  pl.BlockSpec((tm, LANES), lambda i, j: (i % pos_tiles, 0))],
        out_specs=pl.BlockSpec((tm, tn), lambda i, j: (i, j)),
        out_shape=jax.ShapeDtypeStruct((m, n), BF16),
        scratch_shapes=[pltpu.VMEM((tm, d), BF16)],
        compiler_params=_cparams(("arbitrary", "arbitrary")),
        name="odd_proj",
    )(x, g, sc, sh, w, cos, sin)


def _ret_log_gamma(h):
    return float(np.log(np.float64(1.0) - np.float64(2.0) ** (-5.0 - h)))


def _ret_finish(o, sg, gain):
    mu = jnp.mean(o, axis=-1, keepdims=True)
    d = o - mu
    var = jnp.mean(d * d, axis=-1, keepdims=True)
    r = d * lax.rsqrt(var + EPS) * gain
    return sg.astype(F32) * r


def _ret_prompt_kernel(q_ref, k_ref, v_ref, sg_ref, gain_ref, y_ref, s_out_ref, s_ref, dmask_ref, *, chunk, dk, dv):
    c = pl.program_id(1)

    @pl.when(c == 0)
    def _():
        s_ref[...] = jnp.zeros_like(s_ref)

    @pl.when(jnp.logical_and(pl.program_id(0) == 0, c == 0))
    def _():
        ri = lax.broadcasted_iota(jnp.int32, (chunk, chunk), 0)
        ci = lax.broadcasted_iota(jnp.int32, (chunk, chunk), 1)
        diff = (ri - ci).astype(F32)
        for h in range(RET_HEADS):
            dmask_ref[h] = jnp.where(ri >= ci, jnp.exp(jnp.maximum(diff, 0.0) * _ret_log_gamma(h)), 0.0)

    pos = lax.broadcasted_iota(jnp.int32, (chunk, 1), 0).astype(F32)
    for h in range(RET_HEADS):
        lg = _ret_log_gamma(h)
        dmask = dmask_ref[h]
        q_dec = jnp.exp((pos + 1.0) * lg)
        k_dec = jnp.exp((chunk - 1.0 - pos) * lg)
        c_dec = math.exp(chunk * lg)
        q = q_ref[:, h * dk:(h + 1) * dk]
        k = k_ref[:, h * dk:(h + 1) * dk]
        v = v_ref[:, h * dv:(h + 1) * dv]
        s_prev = s_ref[h]
        a = _dot_nt(q, k) * dmask
        o = (jnp.dot(a.astype(BF16), v, preferred_element_type=F32)
             + jnp.dot(q, s_prev.astype(BF16), preferred_element_type=F32) * q_dec)
        kd = (k.astype(F32) * k_dec).T.astype(BF16)
        s_ref[h] = s_prev * c_dec + jnp.dot(kd, v, preferred_element_type=F32)
        cols = slice(h * dv, (h + 1) * dv)
        y_ref[:, cols] = _ret_finish(o, sg_ref[:, cols], gain_ref[:, cols]).astype(y_ref.dtype)

    @pl.when(c == pl.num_programs(1) - 1)
    def _():
        s_out_ref[...] = s_ref[...]


def _ret_prompt(z, gain, *, batch, seq_len, chunk, dk, dv):
    m = z.shape[0]
    nc = seq_len // chunk
    nk = RET_HEADS * dk
    nv = RET_HEADS * dv
    rowblk = lambda b, c: b * nc + c
    return pl.pallas_call(
        functools.partial(_ret_prompt_kernel, chunk=chunk, dk=dk, dv=dv),
        grid=(batch, nc),
        in_specs=[pl.BlockSpec((chunk, nk), lambda b, c: (rowblk(b, c), 0)),
                  pl.BlockSpec((chunk, nk), lambda b, c: (rowblk(b, c), 1)),
                  pl.BlockSpec((chunk, nv), lambda b, c: (rowblk(b, c), (2 * nk) // nv)),
                  pl.BlockSpec((chunk, nv), lambda b, c: (rowblk(b, c), (2 * nk) // nv + 1)),
                  pl.BlockSpec((1, nv), lambda b, c: (0, 0))],
        out_specs=[pl.BlockSpec((chunk, nv), lambda b, c: (rowblk(b, c), 0)),
                   pl.BlockSpec((None, RET_HEADS, dk, dv), lambda b, c: (b, 0, 0, 0))],
        out_shape=[jax.ShapeDtypeStruct((m, nv), BF16), jax.ShapeDtypeStruct((batch, RET_HEADS, dk, dv), F32)],
        scratch_shapes=[pltpu.VMEM((RET_HEADS, dk, dv), F32), pltpu.VMEM((RET_HEADS, chunk, chunk), F32)],
        compiler_params=_cparams(("arbitrary", "arbitrary")),
        name="ret_prompt",
    )(z, z, z, z, gain)


def _ret_step_kernel(z_ref, s_ref, gain_ref, *refs, dk, dv):
    y_ref, s_out_ref = refs[-2:]
    nk = RET_HEADS * dk
    nv = RET_HEADS * dv
    row0 = lax.broadcasted_iota(jnp.int32, (LANES, 1), 0) == 0
    for h in range(RET_HEADS):
        gamma = math.exp(_ret_log_gamma(h))
        q = z_ref[:, h * dk:(h + 1) * dk]
        k = z_ref[:, nk + h * dk:nk + (h + 1) * dk]
        v = z_ref[:, 2 * nk + h * dv:2 * nk + (h + 1) * dv]
        sg = z_ref[:, 2 * nk + nv + h * dv:2 * nk + nv + (h + 1) * dv]
        s_prev = s_ref[h]
        q_rows = jnp.broadcast_to(q.astype(F32), (16, dk)).astype(BF16)
        a = jnp.sum(q.astype(F32) * k.astype(F32), axis=-1, keepdims=True)
        o = a * v.astype(F32) + jnp.dot(q_rows, s_prev.astype(BF16), preferred_element_type=F32)[0:1, :] * gamma
        k_rows = jnp.where(row0, jnp.broadcast_to(k.astype(F32), (LANES, dk)), 0.0)
        v_rows = jnp.where(row0, jnp.broadcast_to(v.astype(F32), (LANES, dv)), 0.0)
        s_out_ref[h] = s_prev * gamma + jnp.dot(k_rows.T.astype(BF16), v_rows.astype(BF16),
                                                preferred_element_type=F32)
        cols = slice(h * dv, (h + 1) * dv)
        y_ref[:, cols] = _ret_finish(o, sg, gain_ref[:, cols]).astype(y_ref.dtype)


def _ret_step(z, states, layer, gain, carried, *, dk, dv):
    bd, n = z.shape
    nv = RET_HEADS * dv
    slab = pl.BlockSpec((None, None, RET_HEADS, dk, dv), lambda b: (layer, b, 0, 0, 0))
    operands = [z.reshape(bd, 1, n), states, gain]
    in_specs = [pl.BlockSpec((None, 1, n), lambda b: (b, 0, 0)), slab, pl.BlockSpec((1, nv), lambda b: (0, 0))]
    aliases = {}
    if carried is not None:
        operands.append(carried)
        in_specs.append(pl.BlockSpec(memory_space=pl.ANY))
        aliases = {3: 1}
    y, s_all = pl.pallas_call(
        functools.partial(_ret_step_kernel, dk=dk, dv=dv),
        grid=(bd,),
        in_specs=in_specs,
        out_specs=[pl.BlockSpec((None, 1, nv), lambda b: (b, 0, 0)), slab],
        out_shape=[jax.ShapeDtypeStruct((bd, 1, nv), BF16), jax.ShapeDtypeStruct(states.shape, F32)],
        input_output_aliases=aliases,
        compiler_params=_cparams(("arbitrary",)),
        name="ret_step",
    )(*operands)
    return y.reshape(bd, nv), s_all


def _fox_decode_kernel(pt_ref, q_ref, qb_ref, kn_ref, vn_ref, lfn_ref, *refs, group):
    del pt_ref
    k_refs = refs[0:group]
    v_refs = refs[group:2 * group]
    lp_refs = refs[2 * group:3 * group]
    o_ref, m_ref, l_ref, w_ref, r_ref, acc_ref = refs[3 * group:]
    c = pl.program_id(1)
    head = lax.broadcasted_iota(jnp.int32, (FOX_HEADS, FOX_DIM), 0)
    chan = lax.broadcasted_iota(jnp.int32, (FOX_HEADS, FOX_DIM), 1)
    own = chan // FOX_HEAD_DIM == head

    def per_head_to_channels(col):
        return jnp.sum(jnp.where(own, jnp.broadcast_to(col, own.shape), 0.0), axis=0, keepdims=True)

    @pl.when(c == 0)
    def _():
        prod = jnp.broadcast_to(q_ref[...] * kn_ref[...], own.shape)
        m_ref[...] = jnp.sum(jnp.where(own, prod, 0.0), axis=-1, keepdims=True)
        l_ref[...] = jnp.ones_like(l_ref)
        w_ref[...] = jnp.ones_like(w_ref)
        r_ref[...] = jnp.zeros_like(r_ref)
        acc_ref[...] = jnp.zeros_like(acc_ref)

    lane = lax.broadcasted_iota(jnp.int32, (1, PAGE_SIZE), 1)
    head_row = lax.broadcasted_iota(jnp.int32, (FOX_HEADS, 1), 0)
    lfn = lfn_ref[...]
    later = r_ref[...]
    scores = []
    for g in range(group):
        lp = lp_refs[g][...]
        suffix = lp
        stride = 1
        while stride < PAGE_SIZE:
            ahead = pltpu.roll(suffix, PAGE_SIZE - stride, axis=1)
            suffix = suffix + jnp.where(lane < PAGE_SIZE - stride, ahead, 0.0)
            stride *= 2
        s = lfn + later + (suffix - lp)
        later = later + jnp.sum(lp, axis=-1, keepdims=True)
        for h in range(FOX_HEADS):
            qk = jnp.sum(k_refs[g][h] * qb_ref[h], axis=0, keepdims=True)
            s = s + jnp.where(head_row == h, jnp.broadcast_to(qk, s.shape), 0.0)
        scores.append(s)
    r_ref[...] = later
    m_prev = m_ref[...]
    m_new = m_prev
    for s in scores:
        m_new = jnp.maximum(m_new, jnp.max(s, axis=-1, keepdims=True))
    alpha = jnp.exp(m_prev - m_new)
    l_new = alpha * l_ref[...]
    probs = []
    for s in scores:
        p = jnp.exp(s - m_new)
        l_new = l_new + jnp.sum(p, axis=-1, keepdims=True)
        probs.append(p)
    for h in range(FOX_HEADS):
        acc = acc_ref[h] * alpha[h:h + 1, :]
        for g in range(group):
            acc = acc + v_refs[g][h] * probs[g][h:h + 1, :]
        acc_ref[h] = acc
    m_ref[...] = m_new
    l_ref[...] = l_new
    w_ref[...] = alpha * w_ref[...]

    @pl.when(c == pl.num_programs(1) - 1)
    def _():
        acc_t = acc_ref[...].reshape(FOX_DIM, PAGE_SIZE).T
        pv = jnp.sum(acc_t, axis=0, keepdims=True)
        o = (pv + per_head_to_channels(w_ref[...]) * vn_ref[...]) / per_head_to_channels(l_ref[...])
        o_ref[...] = o.astype(o_ref.dtype)


def _fox_decode(q, k_new, v_new, logf_new, cache_kt, cache_vt, cache_lpt, page_table, layer, *, group):
    bd, c = q.shape
    n_pages = page_table.shape[1]
    group = min(group, n_pages)
    assert n_pages % group == 0
    pt_flat = page_table.reshape(-1)
    q = q.astype(F32)
    q_lanes = jnp.broadcast_to(q.reshape(bd, FOX_HEADS, FOX_HEAD_DIM, 1), (bd, FOX_HEADS, FOX_HEAD_DIM, PAGE_SIZE))

    def page_spec(g, shape):
        def index(b, s, pt):
            return (layer, pt[b * n_pages + (n_pages - 1 - (s * group + g))]) + (0,) * len(shape)
        return pl.BlockSpec((None, None) + shape, index)

    row = pl.BlockSpec((None, 1, c), lambda b, s, pt: (b, 0, 0))
    slab = (FOX_HEADS, FOX_HEAD_DIM, PAGE_SIZE)
    in_specs = ([row, pl.BlockSpec((None,) + slab, lambda b, s, pt: (b, 0, 0, 0)), row, row,
                 pl.BlockSpec((None, FOX_HEADS, 1), lambda b, s, pt: (b, 0, 0))]
                + [page_spec(g, slab) for g in range(group)]
                + [page_spec(g, slab) for g in range(group)]
                + [page_spec(g, (FOX_HEADS, PAGE_SIZE)) for g in range(group)])
    col = pltpu.VMEM((FOX_HEADS, 1), F32)
    out = pl.pallas_call(
        functools.partial(_fox_decode_kernel, group=group),
        grid_spec=pltpu.PrefetchScalarGridSpec(
            num_scalar_prefetch=1,
            grid=(bd, n_pages // group),
            in_specs=in_specs,
            out_specs=row,
            scratch_shapes=[col, col, col, col, pltpu.VMEM(slab, F32)]),
        out_shape=jax.ShapeDtypeStruct((bd, 1, c), BF16),
        compiler_params=_cparams(("arbitrary", "arbitrary")),
        name="fox_decode",
    )(pt_flat, q.reshape(bd, 1, c), q_lanes, k_new.reshape(bd, 1, c), v_new.reshape(bd, 1, c),
      logf_new.reshape(bd, FOX_HEADS, 1), *([cache_kt] * group), *([cache_vt] * group), *([cache_lpt] * group))
    return out.reshape(bd, c)


def _rope_tables(pos, half):
    inv = 1.0 / (ROPE_BASE ** jnp.linspace(0.0, 1.0, half, dtype=F32))
    ang = pos.astype(F32)[:, None] * inv[None, :]
    return jnp.cos(ang), jnp.sin(ang)


def _prep_weights(w_in_even, b_forget, conv_w, w_out_even, w_in_odd, w_ffn_gate, w_ffn_up, w_ffn_down, dk):
    c = FOX_DIM
    n_even, d, _ = w_in_even.shape
    qkv_scale = jnp.where(jnp.arange(3 * c) < c, FOX_HEAD_DIM ** -0.5, 1.0).astype(F32)
    w_qkv = (w_in_even[:, :, :3 * c] * qkv_scale).astype(BF16)
    w_ab = w_in_even[:, :, 3 * c + FOX_HEADS:].astype(BF16)
    w_f = jnp.pad(w_in_even[:, :, 3 * c:3 * c + FOX_HEADS], ((0, 0), (0, 0), (0, LANES - FOX_HEADS))).astype(BF16)
    bf_pad = jnp.pad(b_forget, ((0, 0), (0, LANES - FOX_HEADS)))[:, None, :]
    cw_pad = jnp.pad(conv_w, ((0, 0), (0, CONV_HALO - CONV_WIDTH), (0, 0)))
    nk = RET_HEADS * dk
    col = jnp.arange(w_in_odd.shape[-1])
    odd_scale = jnp.where((col >= nk) & (col < 2 * nk), dk ** -0.5, 1.0).astype(F32)
    w_odd = (w_in_odd * odd_scale).astype(BF16)
    return dict(w_even=(w_qkv, w_ab, w_f), bf_pad=bf_pad, cw_pad=cw_pad,
                w_out_fox=w_out_even[:, :c].astype(BF16), w_out_conv=w_out_even[:, c:].astype(BF16),
                w_odd=w_odd, wg=w_ffn_gate.astype(BF16), wu=w_ffn_up.astype(BF16), wd=w_ffn_down.astype(BF16))


def _tiles(rows, seq_len):
    if seq_len == 1:
        return dict(tm=rows, tm_ffn=rows, th=1408, tm_odd=rows)
    return dict(tm=min(512, seq_len), tm_ffn=min(512, seq_len), th=1408, tm_odd=min(1024, seq_len))


def _trunk(x, mod, pos, P, W, *, seq_len, even_mixer, odd_mixer):
    depth = mod.shape[0]
    d = x.shape[1]
    t = _tiles(x.shape[0], seq_len)
    tm = t['tm']
    cos, sin = _rope_tables(pos, LANES)
    ks, vs, lfs, cbs, rs = [], [], [], [], []
    for l in range(depth):
        sh1, sc1, g1, sh2, sc2, g2 = [mod[l][..., s * d:(s + 1) * d] for s in range(6)]
        gm = W['norm_mix_g'][l][None, :]
        if l % 2 == 0:
            e = l // 2
            stacked = dict(layer=e, n_layers=P['bf_pad'].shape[0],
                           carried=(ks[-1], vs[-1], lfs[-1]) if ks else ()) if even_mixer.needs_cum else {}
            out = _even_proj(x, gm, sc1, sh1, [w[e] for w in P['w_even']], P['bf_pad'][e], tm=tm, tq=min(FOX_TQ, tm),
                             seq_len=seq_len, with_cum=even_mixer.needs_cum, **stacked)
            q, k, v, logf, u = out[:5]
            o_fox, cv, new_buf = even_mixer(e, q, k, v, logf, u, out[5:])
            ks.append(k); vs.append(v); lfs.append(logf); cbs.append(new_buf)
            acts, w_outs = [o_fox, cv], [P['w_out_fox'][e], P['w_out_conv'][e]]
        else:
            o = l // 2
            z = _odd_proj(x, gm, sc1, sh1, P['w_odd'][o], cos, sin, tm=t['tm_odd'], tn=P['w_odd'].shape[-1] // 3)
            y, s_new = odd_mixer(o, z)
            rs.append(s_new)
            acts, w_outs = [y], [P['w_out_odd'][o]]
        x = _mix_ffn(acts, w_outs, x, g1, W['norm_ffn_g'][l][None, :], sc2, sh2, g2, P['wg'][l], P['wu'][l],
                     P['wd'][l], W['final_norm_g'][None, :], tm=t['tm_ffn'], th=t['th'], final=(l == depth - 1))
    return x, (ks, vs, lfs, cbs, rs)


class _PromptEven:
    needs_cum = True

    def __init__(self, batch, seq_len, W, P):
        self.batch, self.seq_len, self.W, self.P = batch, seq_len, W, P

    def __call__(self, e, q, k, v, logf, u, extra):
        kb, vt, cum, cumt = extra
        W, P = self.W, self.P
        o_fox = _fox_prompt(q, kb, vt, cum, cumt, batch=self.batch, seq_len=self.seq_len, tq=FOX_TQ)
        cv = _conv_prompt(u, P['cw_pad'][e], W['conv_b'][e][None, :], W['conv_norm_g'][e][None, :],
                          W['conv_norm_b'][e][None, :], batch=self.batch, seq_len=self.seq_len, tl=256)
        c = u.shape[1]
        new_buf = u.reshape(self.batch, self.seq_len, c)[:, self.seq_len - (CONV_WIDTH - 1):]
        return o_fox, cv, new_buf


class _SampleEven:
    needs_cum = False

    def __init__(self, W, P, cache_k, cache_v, cache_lp, state_conv, page_table):
        self.W, self.P = W, P
        self.cache_k, self.cache_v, self.cache_lp = cache_k, cache_v, cache_lp
        self.state_conv, self.page_table = state_conv, page_table

    def __call__(self, e, q, k, v, logf, u, extra):
        W, P = self.W, self.P
        o_fox = _fox_decode(q, k, v, logf, self.cache_k, self.cache_v, self.cache_lp, self.page_table, e, group=16)
        st = self.state_conv[e]
        cv = _conv_step(jnp.transpose(st, (1, 0, 2)), u, P['cw_pad'][e], W['conv_b'][e][None, :],
                        W['conv_norm_g'][e][None, :], W['conv_norm_b'][e][None, :])
        new_buf = jnp.concatenate([st[:, 1:], u[:, None, :]], axis=1)
        return o_fox, cv, new_buf


def kernel(x_prompt, x_sample, cache_k, cache_v, cache_logf, state_conv, state_ret, page_table, c_prompt, c_sample,
           ada_w, ada_b, norm_mix_g, norm_ffn_g, w_in_even, b_forget, conv_w, conv_b, conv_norm_g, conv_norm_b,
           w_out_even, w_in_odd, ret_norm_g, w_out_odd, w_ffn_gate, w_ffn_up, w_ffn_down, final_norm_g):
    bp, lp, d = x_prompt.shape
    bs, ls, _ = x_sample.shape
    assert ls == 1, "the decode path handles one new token per sequence"
    depth = ada_w.shape[0]
    n_even, n_phys, page, heads, hd = cache_k.shape
    assert (page, heads, hd) == (PAGE_SIZE, FOX_HEADS, FOX_HEAD_DIM)
    dk, dv = state_ret.shape[-2:]
    past_len = page_table.shape[1] * PAGE_SIZE

    W = dict(norm_mix_g=norm_mix_g, norm_ffn_g=norm_ffn_g, conv_b=conv_b, conv_norm_g=conv_norm_g,
             conv_norm_b=conv_norm_b, final_norm_g=final_norm_g)
    P = _prep_weights(w_in_even, b_forget, conv_w, w_out_even, w_in_odd, w_ffn_gate, w_ffn_up, w_ffn_down, dk)
    P['w_out_odd'] = w_out_odd.astype(BF16)
    gain = ret_norm_g[:, None, :]

    mod = _adaln(jnp.concatenate([c_prompt, c_sample], axis=0), ada_w, ada_b)
    mod_p = mod[:, :bp].reshape(depth, bp, 1, 6 * d)
    mod_s = mod[:, bp:].reshape(depth, 1, bs, 6 * d)

    def odd_prompt(o, z):
        return _ret_prompt(z, gain[o], batch=bp, seq_len=lp, chunk=min(256, lp), dk=dk, dv=dv)

    y_p, (k_p, v_p, lf_p, cb_p, r_p) = _trunk(
        x_prompt.reshape(bp * lp, d), mod_p, jnp.arange(lp), P, W, seq_len=lp,
        even_mixer=_PromptEven(bp, lp, W, P), odd_mixer=odd_prompt)

    ckt = jnp.transpose(cache_k, (0, 1, 3, 4, 2))
    cvt = jnp.transpose(cache_v, (0, 1, 3, 4, 2))
    clp = jnp.transpose(cache_logf, (0, 1, 3, 2))

    new_states = []

    def odd_sample(o, z):
        y, s_all = _ret_step(z, state_ret, o, gain[o], new_states[-1] if new_states else None, dk=dk, dv=dv)
        new_states.append(s_all)
        return y, s_all

    y_s, (k_s, v_s, lf_s, cb_s, r_s) = _trunk(
        x_sample.reshape(bs, d), mod_s, jnp.full((bs,), past_len, jnp.int32), P, W, seq_len=1,
        even_mixer=_SampleEven(W, P, ckt, cvt, clp, state_conv, page_table), odd_mixer=odd_sample)

    hshape = (FOX_HEADS, FOX_HEAD_DIM)

    def seq_last_to_heads(t):
        return jnp.transpose(t.reshape((n_even, bp) + hshape + (lp,)), (0, 1, 4, 2, 3))

    return (y_p.reshape(bp, lp, d), y_s.reshape(bs, ls, d),
            seq_last_to_heads(k_p[-1]), seq_last_to_heads(v_p[-1]),
            jnp.transpose(lf_p[-1], (0, 1, 3, 2)), jnp.stack(cb_p), jnp.stack(r_p),
            jnp.stack(k_s).reshape((n_even, bs, ls) + hshape), jnp.stack(v_s).reshape((n_even, bs, ls) + hshape),
            jnp.stack(lf_s).reshape(n_even, bs, ls, FOX_HEADS), jnp.stack(cb_s),
            r_s[-1].astype(state_ret.dtype))
```

```python
import functools
import math

import jax
import jax.numpy as jnp
import numpy as np
from jax import lax
from jax.experimental import pallas as pl
from jax.experimental.pallas import tpu as pltpu

F32 = jnp.float32
BF16 = jnp.bfloat16

EPS = 1e-6
ROPE_BASE = 10000.0
FOX_HEADS = 8
FOX_HEAD_DIM = 64
FOX_DIM = FOX_HEADS * FOX_HEAD_DIM
CONV_GROUPS = 8
CONV_WIDTH = 31
RET_HEADS = 4
PAGE_SIZE = 128
LANES = 128
SUBLANES = 8
CONV_HALO = 32
VMEM_LIMIT = 56 * 1024 * 1024
NEG_BIG = -1e30
LOG2E = math.log2(math.e)
FOX_TQ = 256


def _cparams(sem):
    return pltpu.CompilerParams(dimension_semantics=sem, vmem_limit_bytes=VMEM_LIMIT)


def _silu(x):
    return x * jax.nn.sigmoid(x)


def _log_sigmoid(x):
    return jnp.minimum(x, 0.0) - jnp.log1p(jnp.exp(-jnp.abs(x)))


def _norm_mod(x, g, sc, sh):
    ms = jnp.mean(x * x, axis=-1, keepdims=True)
    y = x * lax.rsqrt(ms + EPS) * g
    return y * (1.0 + sc) + sh


def _split3(x):
    hi = x.astype(BF16)
    r1 = x - hi.astype(F32)
    mid = r1.astype(BF16)
    lo = (r1 - mid.astype(F32)).astype(BF16)
    return hi, mid, lo


def _dot3(x, w_bf16):
    hi, mid, lo = _split3(x)
    d = functools.partial(jnp.dot, preferred_element_type=F32)
    return d(hi, w_bf16) + d(mid, w_bf16) + d(lo, w_bf16)


def _dot_nt(a, b):
    return lax.dot_general(a, b, (((1,), (1,)), ((), ())), preferred_element_type=F32)


def _mod_spec(rows, d, tiles_per_mod):
    return pl.BlockSpec((None, rows, d), lambda i, *_: (i // tiles_per_mod, 0, 0))


def _adaln_kernel(c_ref, w_ref, b_ref, o_ref):
    cm = _silu(c_ref[...]).astype(BF16)
    o_ref[...] = jnp.dot(cm, w_ref[...].astype(BF16), preferred_element_type=F32) + b_ref[...]


def _adaln(c_all, ada_w, ada_b, tn=1536):
    depth, d, n = ada_w.shape
    r = c_all.shape[0]
    return pl.pallas_call(
        _adaln_kernel,
        grid=(depth, n // tn),
        in_specs=[pl.BlockSpec((r, d), lambda l, j: (0, 0)),
                  pl.BlockSpec((None, d, tn), lambda l, j: (l, 0, j)),
                  pl.BlockSpec((None, 1, tn), lambda l, j: (l, 0, j))],
        out_specs=pl.BlockSpec((None, r, tn), lambda l, j: (l, 0, j)),
        out_shape=jax.ShapeDtypeStruct((depth, r, n), F32),
        compiler_params=_cparams(("arbitrary", "arbitrary")),
        name="adaln",
    )(c_all, ada_w, ada_b.reshape(depth, 1, n))


def _even_proj_kernel(x_ref, g_ref, sc_ref, sh_ref, wqkv_ref, wab_ref, wf_ref, bf_ref, *refs,
                      tm, tq, tiles_per_seq, with_cum, n_carried):
    refs = refs[n_carried:]
    if with_cum:
        q_ref, k_ref, v_ref, lf_ref, u_ref, kb_ref, vt_ref, cum_ref, cumt_ref, carry_ref = refs
    else:
        q_ref, k_ref, v_ref, lf_ref, u_ref = refs
    hm = _norm_mod(x_ref[...], g_ref[...], sc_ref[...], sh_ref[...]).astype(BF16)
    c = FOX_DIM
    qkv = jnp.dot(hm, wqkv_ref[...], preferred_element_type=F32)
    q_ref[...] = qkv[:, 0:c].astype(BF16)
    k = qkv[:, c:2 * c]
    v = qkv[:, 2 * c:3 * c]
    ab = jnp.dot(hm, wab_ref[...], preferred_element_type=F32)
    u_ref[...] = ab[:, 0:c] * jax.nn.sigmoid(ab[:, c:2 * c])
    fl = jnp.dot(hm, wf_ref[...], preferred_element_type=F32)
    logf = _log_sigmoid(fl + bf_ref[...])
    if not with_cum:
        k_ref[...] = k
        v_ref[...] = v
        lf_ref[...] = logf[:, 0:FOX_HEADS]
    else:
        v_t = v.T
        k_ref[...] = k.T
        v_ref[...] = v_t
        lf_ref[...] = logf.T[0:FOX_HEADS, :]
        kb_ref[...] = (k * LOG2E).astype(BF16)
        for r in range(tm // tq):
            vt_ref[r] = v_t[:, r * tq:(r + 1) * tq].astype(BF16)
        i = pl.program_id(0)

        @pl.when(i % tiles_per_seq == 0)
        def _():
            carry_ref[...] = jnp.zeros_like(carry_ref)

        row = lax.broadcasted_iota(jnp.int32, (tm, tm), 0)
        col = lax.broadcasted_iota(jnp.int32, (tm, tm), 1)
        tri = (col <= row).astype(BF16)
        cum = _dot3_left(tri, logf) + carry_ref[...]
        carry_ref[...] = cum[tm - 1:tm, :]
        cum = cum * LOG2E
        cum_ref[...] = cum[:, 0:FOX_HEADS]
        cum_t = cum.T
        for r in range(tm // tq):
            cumt_ref[r] = cum_t[0:FOX_HEADS, r * tq:(r + 1) * tq]


def _dot3_left(w_bf16, x):
    hi, mid, lo = _split3(x)
    d = functools.partial(jnp.dot, preferred_element_type=F32)
    return d(w_bf16, hi) + d(w_bf16, mid) + d(w_bf16, lo)


def _even_proj(x, g, sc, sh, weights, bf_pad, *, tm, tq, seq_len, with_cum, layer=0, n_layers=1, carried=()):
    m, d = x.shape
    nmod, rows, _ = sc.shape
    tiles_per_mod = (m // nmod) // tm
    c = FOX_DIM
    row_spec = lambda width: pl.BlockSpec((tm, width), lambda i: (i, 0))
    tiles_per_seq = max(seq_len // tm, 1)
    if with_cum:
        nseq = m // seq_len
        stacked = lambda width: (
            jax.ShapeDtypeStruct((n_layers, nseq, width, seq_len), F32),
            pl.BlockSpec((None, None, width, tm), lambda i: (layer, i // tiles_per_seq, 0, i % tiles_per_seq)))
        kvl = [stacked(c), stacked(c), stacked(FOX_HEADS)]
    else:
        kvl = [(jax.ShapeDtypeStruct((m, w_), F32), row_spec(w_)) for w_ in (c, c, FOX_HEADS)]
    out_shape = [jax.ShapeDtypeStruct((m, c), BF16)] + [s for s, _ in kvl] + [jax.ShapeDtypeStruct((m, c), F32)]
    out_specs = [row_spec(c)] + [s for _, s in kvl] + [row_spec(c)]
    scratch = []
    if with_cum:
        out_shape += [jax.ShapeDtypeStruct((m, c), BF16), jax.ShapeDtypeStruct((m // tq, c, tq), BF16),
                      jax.ShapeDtypeStruct((m, FOX_HEADS), F32), jax.ShapeDtypeStruct((m // tq, FOX_HEADS, tq), F32)]
        out_specs += [row_spec(c), pl.BlockSpec((tm // tq, c, tq), lambda i: (i, 0, 0)),
                      row_spec(FOX_HEADS), pl.BlockSpec((tm // tq, FOX_HEADS, tq), lambda i: (i, 0, 0))]
        scratch = [pltpu.VMEM((1, LANES), F32)]
    n_fixed = 5 + len(weights)
    return pl.pallas_call(
        functools.partial(_even_proj_kernel, tm=tm, tq=tq, tiles_per_seq=tiles_per_seq, with_cum=with_cum,
                          n_carried=len(carried)),
        grid=(m // tm,),
        in_specs=[row_spec(d), pl.BlockSpec((1, d), lambda i: (0, 0)),
                  _mod_spec(rows, d, tiles_per_mod), _mod_spec(rows, d, tiles_per_mod),
                  *[pl.BlockSpec(w.shape, lambda i: (0, 0)) for w in weights],
                  pl.BlockSpec((1, LANES), lambda i: (0, 0)),
                  *[pl.BlockSpec(memory_space=pl.ANY) for _ in carried]],
        out_specs=out_specs,
        out_shape=out_shape,
        scratch_shapes=scratch,
        input_output_aliases={n_fixed + t: 1 + t for t in range(len(carried))},
        compiler_params=_cparams(("arbitrary",)),
        name="even_proj",
    )(x, g, sc, sh, *weights, bf_pad, *carried)


def _fox_prompt_kernel(q_ref, k_ref, vt_ref, ck_ref, cq_ref, o_ref, qm_ref, m_ref, acc_ref, *, tq):
    i = pl.program_id(1)
    hd = FOX_HEAD_DIM
    low = lax.broadcasted_iota(jnp.int32, (1, LANES), 1) < hd
    for pair in range(FOX_HEADS // 2):
        qp = q_ref[:, pair * LANES:(pair + 1) * LANES].astype(F32)
        qm_ref[2 * pair] = jnp.where(low, qp, 0.0).astype(BF16)
        qm_ref[2 * pair + 1] = jnp.where(low, 0.0, qp).astype(BF16)
    m_ref[...] = jnp.full_like(m_ref, NEG_BIG)
    acc_ref[...] = jnp.zeros_like(acc_ref)
    key = lax.broadcasted_iota(jnp.int32, (tq, tq), 0)
    qry = lax.broadcasted_iota(jnp.int32, (tq, tq), 1)
    causal = key <= qry
    ones = jnp.ones((acc_ref.shape[1] - hd, tq), BF16)

    def scores(j, masked):
        start = pl.multiple_of(j * tq, tq)
        out = []
        for h in range(FOX_HEADS):
            pair = h // 2
            kj = k_ref[pl.ds(start, tq), pair * LANES:(pair + 1) * LANES]
            t = _dot_nt(kj, qm_ref[h]) - ck_ref[pl.ds(start, tq), h:h + 1]
            if masked:
                t = jnp.where(causal, t, NEG_BIG)
            out.append((t, jnp.max(t, axis=0, keepdims=True)))
        return out

    def softmax_pv(j, scored):
        cq_all = cq_ref[i]
        for h in range(FOX_HEADS):
            t, t_max = scored[h]
            cq = cq_all[h:h + 1, :]
            m_prev = m_ref[h:h + 1, :]
            m_new = jnp.maximum(m_prev, t_max + cq)
            p = jnp.exp2(t + (cq - m_new)).astype(BF16)
            alpha = jnp.exp2(m_prev - m_new)
            v_ones = jnp.concatenate([vt_ref[j][h * hd:(h + 1) * hd, :], ones], axis=0)
            acc_ref[h] = alpha * acc_ref[h] + jnp.dot(v_ones, p, preferred_element_type=F32)
            m_ref[h:h + 1, :] = m_new

    def steps(blocks):
        scored = [scores(j, masked) for j, masked in blocks]
        for (j, _), sc in zip(blocks, scored):
            softmax_pv(j, sc)

    def body(jj, carry):
        steps([(2 * jj, False), (2 * jj + 1, False)])
        return carry

    lax.fori_loop(0, i // 2, body, 0)

    @pl.when(i % 2 == 1)
    def _():
        steps([(i - 1, False), (i, True)])

    @pl.when(i % 2 == 0)
    def _():
        steps([(i, True)])
    for pair in range(FOX_HEADS // 2):
        halves = []
        for h in (2 * pair, 2 * pair + 1):
            halves.append(acc_ref[h, 0:hd, :] * (1.0 / acc_ref[h, hd:hd + 1, :]))
        o_ref[:, pair * LANES:(pair + 1) * LANES] = jnp.concatenate(halves, axis=0).T.astype(o_ref.dtype)


def _fox_prompt(q, kb, vt, cum, cumt, *, batch, seq_len, tq):
    m, c = q.shape
    nq = seq_len // tq
    return pl.pallas_call(
        functools.partial(_fox_prompt_kernel, tq=tq),
        grid=(batch, nq),
        in_specs=[pl.BlockSpec((tq, c), lambda b, i: (b * nq + i, 0)),
                  pl.BlockSpec((seq_len, c), lambda b, i: (b, 0)),
                  pl.BlockSpec((nq, c, tq), lambda b, i: (b, 0, 0)),
                  pl.BlockSpec((seq_len, FOX_HEADS), lambda b, i: (b, 0)),
                  pl.BlockSpec((nq, FOX_HEADS, tq), lambda b, i: (b, 0, 0))],
        out_specs=pl.BlockSpec((tq, c), lambda b, i: (b * nq + i, 0)),
        out_shape=jax.ShapeDtypeStruct((m, c), BF16),
        scratch_shapes=[pltpu.VMEM((FOX_HEADS, tq, LANES), BF16), pltpu.VMEM((FOX_HEADS, tq), F32),
                        pltpu.VMEM((FOX_HEADS, FOX_HEAD_DIM + 16, tq), F32)],
        compiler_params=_cparams(("arbitrary", "arbitrary")),
        name="fox_prompt",
    )(q, kb, vt, cum, cumt)


def _group_matrices(channels, groups):
    gsz = channels // groups
    ch = lax.broadcasted_iota(jnp.int32, (channels, LANES), 0)
    gr = lax.broadcasted_iota(jnp.int32, (channels, LANES), 1)
    gather = (ch // gsz == gr).astype(BF16)
    gr_t = lax.broadcasted_iota(jnp.int32, (LANES, channels), 0)
    ch_t = lax.broadcasted_iota(jnp.int32, (LANES, channels), 1)
    spread = (ch_t // gsz == gr_t).astype(BF16)
    return gather, spread, 1.0 / gsz


def _group_norm_rows(y, groups):
    gather, spread, inv = _group_matrices(y.shape[-1], groups)
    mu = _dot3(y, gather) * inv
    d = y - _dot3(mu, spread)
    var = _dot3(d * d, gather) * inv
    return d * _dot3(lax.rsqrt(var + EPS), spread)


def _conv_finish(acc, cb, gn_g, gn_b):
    cv = _group_norm_rows(acc + cb, CONV_GROUPS) * gn_g + gn_b
    return _silu(cv)


def _conv_prompt_kernel(u_ref, halo_ref, cw_ref, cb_ref, gg_ref, gb_ref, o_ref, xp_ref, sh_ref, y_ref, *, tl):
    i = pl.program_id(1)
    halo = halo_ref[...]
    xp_ref[0:CONV_HALO, :] = jnp.where(i == 0, jnp.zeros_like(halo), halo)
    xp_ref[CONV_HALO:CONV_HALO + tl, :] = u_ref[...]
    span = sh_ref.shape[1]
    for r in range(1, SUBLANES):
        sh_ref[r - 1] = xp_ref[r:r + span, :]
    base = CONV_HALO - (CONV_WIDTH - 1)
    rows = 64
    for r0 in range(0, tl, rows):
        for c0 in range(0, u_ref.shape[1], LANES):
            acc = jnp.zeros((rows, LANES), F32)
            for w in range(CONV_WIDTH):
                shift = (base + w) % SUBLANES
                start = r0 + base + w - shift
                if shift == 0:
                    tap = xp_ref[start:start + rows, c0:c0 + LANES]
                else:
                    tap = sh_ref[shift - 1, start:start + rows, c0:c0 + LANES]
                acc = acc + tap * cw_ref[w:w + 1, c0:c0 + LANES]
            y_ref[r0:r0 + rows, c0:c0 + LANES] = acc
    y = _conv_finish(y_ref[...], cb_ref[...], gg_ref[...], gb_ref[...])
    o_ref[...] = y.astype(o_ref.dtype)


def _conv_prompt(u, cw, cb, gn_g, gn_b, *, batch, seq_len, tl):
    m, c = u.shape
    nl = seq_len // tl
    hb = tl // CONV_HALO
    vec = pl.BlockSpec((1, c), lambda b, i: (0, 0))
    return pl.pallas_call(
        functools.partial(_conv_prompt_kernel, tl=tl),
        grid=(batch, nl),
        in_specs=[pl.BlockSpec((tl, c), lambda b, i: (b * nl + i, 0)),
                  pl.BlockSpec((CONV_HALO, c), lambda b, i: (jnp.maximum((b * nl + i) * hb - 1, 0), 0)),
                  pl.BlockSpec((CONV_HALO, c), lambda b, i: (0, 0)), vec, vec, vec],
        out_specs=pl.BlockSpec((tl, c), lambda b, i: (b * nl + i, 0)),
        out_shape=jax.ShapeDtypeStruct((m, c), BF16),
        scratch_shapes=[pltpu.VMEM((CONV_HALO + tl, c), F32),
                        pltpu.VMEM((SUBLANES - 1, CONV_HALO - SUBLANES + tl, c), F32), pltpu.VMEM((tl, c), F32)],
        compiler_params=_cparams(("arbitrary", "arbitrary")),
        name="conv_prompt",
    )(u, u, cw, cb, gn_g, gn_b)


def _conv_step_kernel(st_ref, u_ref, cw_ref, cb_ref, gg_ref, gb_ref, o_ref):
    acc = u_ref[...] * cw_ref[CONV_WIDTH - 1:CONV_WIDTH, :]
    for w in range(CONV_WIDTH - 1):
        acc = acc + st_ref[w] * cw_ref[w:w + 1, :]
    o_ref[...] = _conv_finish(acc, cb_ref[...], gg_ref[...], gb_ref[...]).astype(o_ref.dtype)


def _conv_step(state_t, u, cw, cb, gn_g, gn_b):
    bd, c = u.shape
    full = lambda shape: pl.BlockSpec(shape, lambda i: (0,) * len(shape))
    return pl.pallas_call(
        _conv_step_kernel,
        grid=(1,),
        in_specs=[full(state_t.shape), full((bd, c)), full(cw.shape), full((1, c)), full((1, c)), full((1, c))],
        out_specs=full((bd, c)),
        out_shape=jax.ShapeDtypeStruct((bd, c), BF16),
        compiler_params=_cparams(("arbitrary",)),
        name="conv_step",
    )(state_t, u, cw, cb, gn_g, gn_b)


def _mix_ffn_kernel(*refs, n_in, final):
    a_refs = refs[:n_in]
    w_refs = refs[n_in:2 * n_in]
    (x_ref, g1_ref, g_ref, sc_ref, sh_ref, g2_ref, wg_ref, wu_ref, wd_ref, fg_ref,
     o_ref, x1_ref, hf_ref, acc_ref) = refs[2 * n_in:]
    j = pl.program_id(1)

    @pl.when(j == 0)
    def _():
        mix = jnp.dot(a_refs[0][...], w_refs[0][...], preferred_element_type=F32)
        for a_ref, w_ref in zip(a_refs[1:], w_refs[1:]):
            mix = mix + jnp.dot(a_ref[...], w_ref[...], preferred_element_type=F32)
        x1 = x_ref[...] + g1_ref[...] * mix
        x1_ref[...] = x1
        hf_ref[...] = _norm_mod(x1, g_ref[...], sc_ref[...], sh_ref[...]).astype(BF16)
        acc_ref[...] = jnp.zeros_like(acc_ref)

    hf = hf_ref[...]
    gt = jnp.dot(hf, wg_ref[...], preferred_element_type=F32)
    up = jnp.dot(hf, wu_ref[...], preferred_element_type=F32)
    acc_ref[...] += jnp.dot((_silu(gt) * up).astype(BF16), wd_ref[...], preferred_element_type=F32)

    @pl.when(j == pl.num_programs(1) - 1)
    def _():
        y = x1_ref[...] + g2_ref[...] * acc_ref[...]
        if final:
            ms = jnp.mean(y * y, axis=-1, keepdims=True)
            y = y * lax.rsqrt(ms + EPS) * fg_ref[...]
        o_ref[...] = y


def _mix_ffn(acts, w_outs, x, g1, g, sc, sh, g2, wg, wu, wd, fg, *, tm, th, final):
    m, d = x.shape
    nmod, rows, _ = sc.shape
    hdim = wg.shape[1]
    tiles_per_mod = (m // nmod) // tm
    vec = pl.BlockSpec((1, d), lambda i, j: (0, 0))
    mod = _mod_spec(rows, d, tiles_per_mod)
    in_specs = ([pl.BlockSpec((tm, a.shape[1]), lambda i, j: (i, 0)) for a in acts]
                + [pl.BlockSpec(w.shape, lambda i, j: (0, 0)) for w in w_outs]
                + [pl.BlockSpec((tm, d), lambda i, j: (i, 0)), mod, vec, mod, mod, mod,
                   pl.BlockSpec((d, th), lambda i, j: (0, j)), pl.BlockSpec((d, th), lambda i, j: (0, j)),
                   pl.BlockSpec((th, d), lambda i, j: (j, 0)), vec])
    return pl.pallas_call(
        functools.partial(_mix_ffn_kernel, n_in=len(acts), final=final),
        grid=(m // tm, hdim // th),
        in_specs=in_specs,
        out_specs=pl.BlockSpec((tm, d), lambda i, j: (i, 0)),
        out_shape=jax.ShapeDtypeStruct((m, d), F32),
        scratch_shapes=[pltpu.VMEM((tm, d), F32), pltpu.VMEM((tm, d), BF16), pltpu.VMEM((tm, d), F32)],
        compiler_params=_cparams(("arbitrary", "arbitrary")),
        name="mix_ffn",
    )(*acts, *w_outs, x, g1, g, sc, sh, g2, wg, wu, wd, fg)


def _odd_proj_kernel(x_ref, g_ref, sc_ref, sh_ref, w_ref, cos_ref, sin_ref, o_ref, hm_ref):
    j = pl.program_id(1)

    @pl.when(j == 0)
    def _():
        hm_ref[...] = _norm_mod(x_ref[...], g_ref[...], sc_ref[...], sh_ref[...]).astype(BF16)

    def slab():
        return jnp.dot(hm_ref[...], w_ref[...], preferred_element_type=F32)

    @pl.when(j == 0)
    def _():
        z = slab()
        cos = cos_ref[...]
        sin = sin_ref[...]
        for c0 in range(0, z.shape[1], 2 * LANES):
            x1 = z[:, c0:c0 + LANES]
            x2 = z[:, c0 + LANES:c0 + 2 * LANES]
            o_ref[:, c0:c0 + LANES] = (x1 * cos - x2 * sin).astype(o_ref.dtype)
            o_ref[:, c0 + LANES:c0 + 2 * LANES] = (x1 * sin + x2 * cos).astype(o_ref.dtype)

    @pl.when(j == 1)
    def _():
        o_ref[...] = slab().astype(o_ref.dtype)

    @pl.when(j == 2)
    def _():
        o_ref[...] = _silu(slab()).astype(o_ref.dtype)


def _odd_proj(x, g, sc, sh, w, cos, sin, *, tm, tn):
    m, d = x.shape
    nmod, rows, _ = sc.shape
    n = w.shape[1]
    tiles_per_mod = (m // nmod) // tm
    assert n == 3 * tn, "q|k, v and gate slabs must have equal widths"
    pos_tiles = cos.shape[0] // tm
    mod = _mod_spec(rows, d, tiles_per_mod)
    return pl.pallas_call(
        _odd_proj_kernel,
        grid=(m // tm, 3),
        in_specs=[pl.BlockSpec((tm, d), lambda i, j: (i, 0)), pl.BlockSpec((1, d), lambda i, j: (0, 0)), mod, mod,
                  pl.BlockSpec((d, tn), lambda i, j: (0, j)),
                  pl.BlockSpec((tm, LANES), lambda i, j: (i % pos_tiles, 0)),
                  pl.BlockSpec((tm, LANES), lambda i, j: (i % pos_tiles, 0))],
        out_specs=pl.BlockSpec((tm, tn), lambda i, j: (i, j)),
        out_shape=jax.ShapeDtypeStruct((m, n), BF16),
        scratch_shapes=[pltpu.VMEM((tm, d), BF16)],
        compiler_params=_cparams(("arbitrary", "arbitrary")),
        name="odd_proj",
    )(x, g, sc, sh, w, cos, sin)


def _ret_log_gamma(h):
    return float(np.log(np.float64(1.0) - np.float64(2.0) ** (-5.0 - h)))


def _ret_finish(o, sg, gain):
    mu = jnp.mean(o, axis=-1, keepdims=True)
    d = o - mu
    var = jnp.mean(d * d, axis=-1, keepdims=True)
    r = d * lax.rsqrt(var + EPS) * gain
    return sg.astype(F32) * r


def _ret_prompt_kernel(q_ref, k_ref, v_ref, sg_ref, gain_ref, y_ref, s_out_ref, s_ref, dmask_ref, *, chunk, dk, dv):
    c = pl.program_id(1)

    @pl.when(c == 0)
    def _():
        s_ref[...] = jnp.zeros_like(s_ref)

    @pl.when(jnp.logical_and(pl.program_id(0) == 0, c == 0))
    def _():
        ri = lax.broadcasted_iota(jnp.int32, (chunk, chunk), 0)
        ci = lax.broadcasted_iota(jnp.int32, (chunk, chunk), 1)
        diff = (ri - ci).astype(F32)
        for h in range(RET_HEADS):
            dmask_ref[h] = jnp.where(ri >= ci, jnp.exp(jnp.maximum(diff, 0.0) * _ret_log_gamma(h)), 0.0)

    pos = lax.broadcasted_iota(jnp.int32, (chunk, 1), 0).astype(F32)
    for h in range(RET_HEADS):
        lg = _ret_log_gamma(h)
        dmask = dmask_ref[h]
        q_dec = jnp.exp((pos + 1.0) * lg)
        k_dec = jnp.exp((chunk - 1.0 - pos) * lg)
        c_dec = math.exp(chunk * lg)
        q = q_ref[:, h * dk:(h + 1) * dk]
        k = k_ref[:, h * dk:(h + 1) * dk]
        v = v_ref[:, h * dv:(h + 1) * dv]
        s_prev = s_ref[h]
        a = _dot_nt(q, k) * dmask
        o = (jnp.dot(a.astype(BF16), v, preferred_element_type=F32)
             + jnp.dot(q, s_prev.astype(BF16), preferred_element_type=F32) * q_dec)
        kd = (k.astype(F32) * k_dec).T.astype(BF16)
        s_ref[h] = s_prev * c_dec + jnp.dot(kd, v, preferred_element_type=F32)
        cols = slice(h * dv, (h + 1) * dv)
        y_ref[:, cols] = _ret_finish(o, sg_ref[:, cols], gain_ref[:, cols]).astype(y_ref.dtype)

    @pl.when(c == pl.num_programs(1) - 1)
    def _():
        s_out_ref[...] = s_ref[...]


def _ret_prompt(z, gain, *, batch, seq_len, chunk, dk, dv):
    m = z.shape[0]
    nc = seq_len // chunk
    nk = RET_HEADS * dk
    nv = RET_HEADS * dv
    rowblk = lambda b, c: b * nc + c
    return pl.pallas_call(
        functools.partial(_ret_prompt_kernel, chunk=chunk, dk=dk, dv=dv),
        grid=(batch, nc),
        in_specs=[pl.BlockSpec((chunk, nk), lambda b, c: (rowblk(b, c), 0)),
                  pl.BlockSpec((chunk, nk), lambda b, c: (rowblk(b, c), 1)),
                  pl.BlockSpec((chunk, nv), lambda b, c: (rowblk(b, c), (2 * nk) // nv)),
                  pl.BlockSpec((chunk, nv), lambda b, c: (rowblk(b, c), (2 * nk) // nv + 1)),
                  pl.BlockSpec((1, nv), lambda b, c: (0, 0))],
        out_specs=[pl.BlockSpec((chunk, nv), lambda b, c: (rowblk(b, c), 0)),
                   pl.BlockSpec((None, RET_HEADS, dk, dv), lambda b, c: (b, 0, 0, 0))],
        out_shape=[jax.ShapeDtypeStruct((m, nv), BF16), jax.ShapeDtypeStruct((batch, RET_HEADS, dk, dv), F32)],
        scratch_shapes=[pltpu.VMEM((RET_HEADS, dk, dv), F32), pltpu.VMEM((RET_HEADS, chunk, chunk), F32)],
        compiler_params=_cparams(("arbitrary", "arbitrary")),
        name="ret_prompt",
    )(z, z, z, z, gain)


def _ret_step_kernel(z_ref, s_ref, gain_ref, *refs, dk, dv):
    y_ref, s_out_ref = refs[-2:]
    nk = RET_HEADS * dk
    nv = RET_HEADS * dv
    row0 = lax.broadcasted_iota(jnp.int32, (LANES, 1), 0) == 0
    for h in range(RET_HEADS):
        gamma = math.exp(_ret_log_gamma(h))
        q = z_ref[:, h * dk:(h + 1) * dk]
        k = z_ref[:, nk + h * dk:nk + (h + 1) * dk]
        v = z_ref[:, 2 * nk + h * dv:2 * nk + (h + 1) * dv]
        sg = z_ref[:, 2 * nk + nv + h * dv:2 * nk + nv + (h + 1) * dv]
        s_prev = s_ref[h]
        q_rows = jnp.broadcast_to(q.astype(F32), (16, dk)).astype(BF16)
        a = jnp.sum(q.astype(F32) * k.astype(F32), axis=-1, keepdims=True)
        o = a * v.astype(F32) + jnp.dot(q_rows, s_prev.astype(BF16), preferred_element_type=F32)[0:1, :] * gamma
        k_rows = jnp.where(row0, jnp.broadcast_to(k.astype(F32), (LANES, dk)), 0.0)
        v_rows = jnp.where(row0, jnp.broadcast_to(v.astype(F32), (LANES, dv)), 0.0)
        s_out_ref[h] = s_prev * gamma + jnp.dot(k_rows.T.astype(BF16), v_rows.astype(BF16),
                                                preferred_element_type=F32)
        cols = slice(h * dv, (h + 1) * dv)
        y_ref[:, cols] = _ret_finish(o, sg, gain_ref[:, cols]).astype(y_ref.dtype)


def _ret_step(z, states, layer, gain, carried, *, dk, dv):
    bd, n = z.shape
    nv = RET_HEADS * dv
    slab = pl.BlockSpec((None, None, RET_HEADS, dk, dv), lambda b: (layer, b, 0, 0, 0))
    operands = [z.reshape(bd, 1, n), states, gain]
    in_specs = [pl.BlockSpec((None, 1, n), lambda b: (b, 0, 0)), slab, pl.BlockSpec((1, nv), lambda b: (0, 0))]
    aliases = {}
    if carried is not None:
        operands.append(carried)
        in_specs.append(pl.BlockSpec(memory_space=pl.ANY))
        aliases = {3: 1}
    y, s_all = pl.pallas_call(
        functools.partial(_ret_step_kernel, dk=dk, dv=dv),
        grid=(bd,),
        in_specs=in_specs,
        out_specs=[pl.BlockSpec((None, 1, nv), lambda b: (b, 0, 0)), slab],
        out_shape=[jax.ShapeDtypeStruct((bd, 1, nv), BF16), jax.ShapeDtypeStruct(states.shape, F32)],
        input_output_aliases=aliases,
        compiler_params=_cparams(("arbitrary",)),
        name="ret_step",
    )(*operands)
    return y.reshape(bd, nv), s_all


def _fox_decode_kernel(pt_ref, q_ref, qb_ref, kn_ref, vn_ref, lfn_ref, *refs, group, n_pages):
    k_refs = refs[0:group]
    v_refs = refs[group:2 * group]
    lpt_ref = refs[2 * group]
    o_ref, m_ref, l_ref, w_ref, r_ref, acc_ref = refs[2 * group + 1:]
    c = pl.program_id(1)
    first_page = pl.program_id(0) * n_pages + (n_pages - 1) - c * group
    head = lax.broadcasted_iota(jnp.int32, (FOX_HEADS, FOX_DIM), 0)
    chan = lax.broadcasted_iota(jnp.int32, (FOX_HEADS, FOX_DIM), 1)
    own = chan // FOX_HEAD_DIM == head

    def per_head_to_channels(col):
        return jnp.sum(jnp.where(own, jnp.broadcast_to(col, own.shape), 0.0), axis=0, keepdims=True)

    @pl.when(c == 0)
    def _():
        prod = jnp.broadcast_to(q_ref[...] * kn_ref[...], own.shape)
        m_ref[...] = jnp.sum(jnp.where(own, prod, 0.0), axis=-1, keepdims=True)
        l_ref[...] = jnp.ones_like(l_ref)
        w_ref[...] = jnp.ones_like(w_ref)
        r_ref[...] = jnp.zeros_like(r_ref)
        acc_ref[...] = jnp.zeros_like(acc_ref)

    lane = lax.broadcasted_iota(jnp.int32, (1, PAGE_SIZE), 1)
    head_row = lax.broadcasted_iota(jnp.int32, (FOX_HEADS, 1), 0)
    lfn = lfn_ref[...]
    later = r_ref[...]
    scores = []
    for g in range(group):
        lp = lpt_ref[pt_ref[first_page - g]]
        suffix = lp
        stride = 1
        while stride < PAGE_SIZE:
            ahead = pltpu.roll(suffix, PAGE_SIZE - stride, axis=1)
            suffix = suffix + jnp.where(lane < PAGE_SIZE - stride, ahead, 0.0)
            stride *= 2
        s = lfn + later + (suffix - lp)
        later = later + jnp.sum(lp, axis=-1, keepdims=True)
        for h in range(FOX_HEADS):
            qk = jnp.sum(k_refs[g][h] * qb_ref[h], axis=0, keepdims=True)
            s = s + jnp.where(head_row == h, jnp.broadcast_to(qk, s.shape), 0.0)
        scores.append(s)
    r_ref[...] = later
    m_prev = m_ref[...]
    m_new = m_prev
    for s in scores:
        m_new = jnp.maximum(m_new, jnp.max(s, axis=-1, keepdims=True))
    alpha = jnp.exp(m_prev - m_new)
    l_new = alpha * l_ref[...]
    probs = []
    for s in scores:
        p = jnp.exp(s - m_new)
        l_new = l_new + jnp.sum(p, axis=-1, keepdims=True)
        probs.append(p)
    for h in range(FOX_HEADS):
        acc = acc_ref[h] * alpha[h:h + 1, :]
        for g in range(group):
            acc = acc + v_refs[g][h] * probs[g][h:h + 1, :]
        acc_ref[h] = acc
    m_ref[...] = m_new
    l_ref[...] = l_new
    w_ref[...] = alpha * w_ref[...]

    @pl.when(c == pl.num_programs(1) - 1)
    def _():
        acc_t = acc_ref[...].reshape(FOX_DIM, PAGE_SIZE).T
        pv = jnp.sum(acc_t, axis=0, keepdims=True)
        o = (pv + per_head_to_channels(w_ref[...]) * vn_ref[...]) / per_head_to_channels(l_ref[...])
        o_ref[...] = o.astype(o_ref.dtype)


def _fox_decode(q, k_new, v_new, logf_new, cache_kt, cache_vt, cache_lpt, page_table, layer, *, group):
    bd, c = q.shape
    n_pages = page_table.shape[1]
    group = min(group, n_pages)
    assert n_pages % group == 0
    pt_flat = page_table.reshape(-1)
    q = q.astype(F32)
    q_lanes = jnp.broadcast_to(q.reshape(bd, FOX_HEADS, FOX_HEAD_DIM, 1), (bd, FOX_HEADS, FOX_HEAD_DIM, PAGE_SIZE))

    def page_spec(g, shape):
        def index(b, s, pt):
            return (layer, pt[b * n_pages + (n_pages - 1 - (s * group + g))]) + (0,) * len(shape)
        return pl.BlockSpec((None, None) + shape, index)

    row = pl.BlockSpec((None, 1, c), lambda b, s, pt: (b, 0, 0))
    slab = (FOX_HEADS, FOX_HEAD_DIM, PAGE_SIZE)
    in_specs = ([row, pl.BlockSpec((None,) + slab, lambda b, s, pt: (b, 0, 0, 0)), row, row,
                 pl.BlockSpec((None, FOX_HEADS, 1), lambda b, s, pt: (b, 0, 0))]
                + [page_spec(g, slab) for g in range(group)]
                + [page_spec(g, slab) for g in range(group)]
                + [pl.BlockSpec((None,) + cache_lpt.shape[1:], lambda b, s, pt: (layer, 0, 0, 0))])
    col = pltpu.VMEM((FOX_HEADS, 1), F32)
    out = pl.pallas_call(
        functools.partial(_fox_decode_kernel, group=group, n_pages=n_pages),
        grid_spec=pltpu.PrefetchScalarGridSpec(
            num_scalar_prefetch=1,
            grid=(bd, n_pages // group),
            in_specs=in_specs,
            out_specs=row,
            scratch_shapes=[col, col, col, col, pltpu.VMEM(slab, F32)]),
        out_shape=jax.ShapeDtypeStruct((bd, 1, c), BF16),
        compiler_params=_cparams(("arbitrary", "arbitrary")),
        name="fox_decode",
    )(pt_flat, q.reshape(bd, 1, c), q_lanes, k_new.reshape(bd, 1, c), v_new.reshape(bd, 1, c),
      logf_new.reshape(bd, FOX_HEADS, 1), *([cache_kt] * group), *([cache_vt] * group), cache_lpt)
    return out.reshape(bd, c)


def _rope_tables(pos, half):
    inv = 1.0 / (ROPE_BASE ** jnp.linspace(0.0, 1.0, half, dtype=F32))
    ang = pos.astype(F32)[:, None] * inv[None, :]
    return jnp.cos(ang), jnp.sin(ang)


def _prep_weights(w_in_even, b_forget, conv_w, w_out_even, w_in_odd, w_ffn_gate, w_ffn_up, w_ffn_down, dk):
    c = FOX_DIM
    n_even, d, _ = w_in_even.shape
    qkv_scale = jnp.where(jnp.arange(3 * c) < c, FOX_HEAD_DIM ** -0.5, 1.0).astype(F32)
    w_qkv = (w_in_even[:, :, :3 * c] * qkv_scale).astype(BF16)
    w_ab = w_in_even[:, :, 3 * c + FOX_HEADS:].astype(BF16)
    w_f = jnp.pad(w_in_even[:, :, 3 * c:3 * c + FOX_HEADS], ((0, 0), (0, 0), (0, LANES - FOX_HEADS))).astype(BF16)
    bf_pad = jnp.pad(b_forget, ((0, 0), (0, LANES - FOX_HEADS)))[:, None, :]
    cw_pad = jnp.pad(conv_w, ((0, 0), (0, CONV_HALO - CONV_WIDTH), (0, 0)))
    nk = RET_HEADS * dk
    col = jnp.arange(w_in_odd.shape[-1])
    odd_scale = jnp.where((col >= nk) & (col < 2 * nk), dk ** -0.5, 1.0).astype(F32)
    w_odd = (w_in_odd * odd_scale).astype(BF16)
    return dict(w_even=(w_qkv, w_ab, w_f), bf_pad=bf_pad, cw_pad=cw_pad,
                w_out_fox=w_out_even[:, :c].astype(BF16), w_out_conv=w_out_even[:, c:].astype(BF16),
                w_odd=w_odd, wg=w_ffn_gate.astype(BF16), wu=w_ffn_up.astype(BF16), wd=w_ffn_down.astype(BF16))


def _tiles(rows, seq_len):
    if seq_len == 1:
        return dict(tm=rows, tm_ffn=rows, th=1408, tm_odd=rows)
    return dict(tm=min(512, seq_len), tm_ffn=min(512, seq_len), th=1408, tm_odd=min(1024, seq_len))


def _trunk(x, mod, pos, P, W, *, seq_len, even_mixer, odd_mixer):
    depth = mod.shape[0]
    d = x.shape[1]
    t = _tiles(x.shape[0], seq_len)
    tm = t['tm']
    cos, sin = _rope_tables(pos, LANES)
    ks, vs, lfs, cbs, rs = [], [], [], [], []
    for l in range(depth):
        sh1, sc1, g1, sh2, sc2, g2 = [mod[l][..., s * d:(s + 1) * d] for s in range(6)]
        gm = W['norm_mix_g'][l][None, :]
        if l % 2 == 0:
            e = l // 2
            stacked = dict(layer=e, n_layers=P['bf_pad'].shape[0],
                           carried=(ks[-1], vs[-1], lfs[-1]) if ks else ()) if even_mixer.needs_cum else {}
            out = _even_proj(x, gm, sc1, sh1, [w[e] for w in P['w_even']], P['bf_pad'][e], tm=tm, tq=min(FOX_TQ, tm),
                             seq_len=seq_len, with_cum=even_mixer.needs_cum, **stacked)
            q, k, v, logf, u = out[:5]
            o_fox, cv, new_buf = even_mixer(e, q, k, v, logf, u, out[5:])
            ks.append(k); vs.append(v); lfs.append(logf); cbs.append(new_buf)
            acts, w_outs = [o_fox, cv], [P['w_out_fox'][e], P['w_out_conv'][e]]
        else:
            o = l // 2
            z = _odd_proj(x, gm, sc1, sh1, P['w_odd'][o], cos, sin, tm=t['tm_odd'], tn=P['w_odd'].shape[-1] // 3)
            y, s_new = odd_mixer(o, z)
            rs.append(s_new)
            acts, w_outs = [y], [P['w_out_odd'][o]]
        x = _mix_ffn(acts, w_outs, x, g1, W['norm_ffn_g'][l][None, :], sc2, sh2, g2, P['wg'][l], P['wu'][l],
                     P['wd'][l], W['final_norm_g'][None, :], tm=t['tm_ffn'], th=t['th'], final=(l == depth - 1))
    return x, (ks, vs, lfs, cbs, rs)


class _PromptEven:
    needs_cum = True

    def __init__(self, batch, seq_len, W, P):
        self.batch, self.seq_len, self.W, self.P = batch, seq_len, W, P

    def __call__(self, e, q, k, v, logf, u, extra):
        kb, vt, cum, cumt = extra
        W, P = self.W, self.P
        o_fox = _fox_prompt(q, kb, vt, cum, cumt, batch=self.batch, seq_len=self.seq_len, tq=FOX_TQ)
        cv = _conv_prompt(u, P['cw_pad'][e], W['conv_b'][e][None, :], W['conv_norm_g'][e][None, :],
                          W['conv_norm_b'][e][None, :], batch=self.batch, seq_len=self.seq_len, tl=256)
        c = u.shape[1]
        new_buf = u.reshape(self.batch, self.seq_len, c)[:, self.seq_len - (CONV_WIDTH - 1):]
        return o_fox, cv, new_buf


class _SampleEven:
    needs_cum = False

    def __init__(self, W, P, cache_k, cache_v, cache_lp, state_conv, page_table):
        self.W, self.P = W, P
        self.cache_k, self.cache_v, self.cache_lp = cache_k, cache_v, cache_lp
        self.state_conv, self.page_table = state_conv, page_table

    def __call__(self, e, q, k, v, logf, u, extra):
        W, P = self.W, self.P
        o_fox = _fox_decode(q, k, v, logf, self.cache_k, self.cache_v, self.cache_lp, self.page_table, e, group=16)
        st = self.state_conv[e]
        cv = _conv_step(jnp.transpose(st, (1, 0, 2)), u, P['cw_pad'][e], W['conv_b'][e][None, :],
                        W['conv_norm_g'][e][None, :], W['conv_norm_b'][e][None, :])
        new_buf = jnp.concatenate([st[:, 1:], u[:, None, :]], axis=1)
        return o_fox, cv, new_buf


def kernel(x_prompt, x_sample, cache_k, cache_v, cache_logf, state_conv, state_ret, page_table, c_prompt, c_sample,
           ada_w, ada_b, norm_mix_g, norm_ffn_g, w_in_even, b_forget, conv_w, conv_b, conv_norm_g, conv_norm_b,
           w_out_even, w_in_odd, ret_norm_g, w_out_odd, w_ffn_gate, w_ffn_up, w_ffn_down, final_norm_g):
    bp, lp, d = x_prompt.shape
    bs, ls, _ = x_sample.shape
    assert ls == 1, "the decode path handles one new token per sequence"
    depth = ada_w.shape[0]
    n_even, n_phys, page, heads, hd = cache_k.shape
    assert (page, heads, hd) == (PAGE_SIZE, FOX_HEADS, FOX_HEAD_DIM)
    dk, dv = state_ret.shape[-2:]
    past_len = page_table.shape[1] * PAGE_SIZE

    W = dict(norm_mix_g=norm_mix_g, norm_ffn_g=norm_ffn_g, conv_b=conv_b, conv_norm_g=conv_norm_g,
             conv_norm_b=conv_norm_b, final_norm_g=final_norm_g)
    P = _prep_weights(w_in_even, b_forget, conv_w, w_out_even, w_in_odd, w_ffn_gate, w_ffn_up, w_ffn_down, dk)
    P['w_out_odd'] = w_out_odd.astype(BF16)
    gain = ret_norm_g[:, None, :]

    mod = _adaln(jnp.concatenate([c_prompt, c_sample], axis=0), ada_w, ada_b)
    mod_p = mod[:, :bp].reshape(depth, bp, 1, 6 * d)
    mod_s = mod[:, bp:].reshape(depth, 1, bs, 6 * d)

    def odd_prompt(o, z):
        return _ret_prompt(z, gain[o], batch=bp, seq_len=lp, chunk=min(256, lp), dk=dk, dv=dv)

    y_p, (k_p, v_p, lf_p, cb_p, r_p) = _trunk(
        x_prompt.reshape(bp * lp, d), mod_p, jnp.arange(lp), P, W, seq_len=lp,
        even_mixer=_PromptEven(bp, lp, W, P), odd_mixer=odd_prompt)

    ckt = jnp.transpose(cache_k, (0, 1, 3, 4, 2))
    cvt = jnp.transpose(cache_v, (0, 1, 3, 4, 2))
    clp = jnp.transpose(cache_logf, (0, 1, 3, 2))

    new_states = []

    def odd_sample(o, z):
        y, s_all = _ret_step(z, state_ret, o, gain[o], new_states[-1] if new_states else None, dk=dk, dv=dv)
        new_states.append(s_all)
        return y, s_all

    y_s, (k_s, v_s, lf_s, cb_s, r_s) = _trunk(
        x_sample.reshape(bs, d), mod_s, jnp.full((bs,), past_len, jnp.int32), P, W, seq_len=1,
        even_mixer=_SampleEven(W, P, ckt, cvt, clp, state_conv, page_table), odd_mixer=odd_sample)

    hshape = (FOX_HEADS, FOX_HEAD_DIM)

    def seq_last_to_heads(t):
        return jnp.transpose(t.reshape((n_even, bp) + hshape + (lp,)), (0, 1, 4, 2, 3))

    return (y_p.reshape(bp, lp, d), y_s.reshape(bs, ls, d),
            seq_last_to_heads(k_p[-1]), seq_last_to_heads(v_p[-1]),
            jnp.transpose(lf_p[-1], (0, 1, 3, 2)), jnp.stack(cb_p), jnp.stack(r_p),
            jnp.stack(k_s).reshape((n_even, bs, ls) + hshape), jnp.stack(v_s).reshape((n_even, bs, ls) + hshape),
            jnp.stack(lf_s).reshape(n_even, bs, ls, FOX_HEADS), jnp.stack(cb_s),
            r_s[-1].astype(state_ret.dtype))
```

```python
import functools
import math

import jax
import jax.numpy as jnp
import numpy as np
from jax import lax
from jax.experimental import pallas as pl
from jax.experimental.pallas import tpu as pltpu

F32 = jnp.float32
BF16 = jnp.bfloat16

EPS = 1e-6
ROPE_BASE = 10000.0
FOX_HEADS = 8
FOX_HEAD_DIM = 64
FOX_DIM = FOX_HEADS * FOX_HEAD_DIM
CONV_GROUPS = 8
CONV_WIDTH = 31
RET_HEADS = 4
PAGE_SIZE = 128
LANES = 128
SUBLANES = 8
CONV_HALO = 32
VMEM_LIMIT = 56 * 1024 * 1024
NEG_BIG = -1e30
LOG2E = math.log2(math.e)
FOX_TQ = 256


def _cparams(sem):
    return pltpu.CompilerParams(dimension_semantics=sem, vmem_limit_bytes=VMEM_LIMIT)


def _silu(x):
    return x * jax.nn.sigmoid(x)


def _log_sigmoid(x):
    return jnp.minimum(x, 0.0) - jnp.log1p(jnp.exp(-jnp.abs(x)))


def _norm_mod(x, g, sc, sh):
    ms = jnp.mean(x * x, axis=-1, keepdims=True)
    y = x * lax.rsqrt(ms + EPS) * g
    return y * (1.0 + sc) + sh


def _split3(x):
    hi = x.astype(BF16)
    r1 = x - hi.astype(F32)
    mid = r1.astype(BF16)
    lo = (r1 - mid.astype(F32)).astype(BF16)
    return hi, mid, lo


def _dot3(x, w_bf16):
    hi, mid, lo = _split3(x)
    d = functools.partial(jnp.dot, preferred_element_type=F32)
    return d(hi, w_bf16) + d(mid, w_bf16) + d(lo, w_bf16)


def _dot2(x, w_bf16):
    hi = x.astype(BF16)
    lo = (x - hi.astype(F32)).astype(BF16)
    d = functools.partial(jnp.dot, preferred_element_type=F32)
    return d(hi, w_bf16) + d(lo, w_bf16)


def _dot_nt(a, b):
    return lax.dot_general(a, b, (((1,), (1,)), ((), ())), preferred_element_type=F32)


def _mod_spec(rows, d, tiles_per_mod):
    return pl.BlockSpec((None, rows, d), lambda i, *_: (i // tiles_per_mod, 0, 0))


def _adaln_kernel(c_ref, w_ref, b_ref, o_ref):
    cm = _silu(c_ref[...]).astype(BF16)
    o_ref[...] = jnp.dot(cm, w_ref[...].astype(BF16), preferred_element_type=F32) + b_ref[...]


def _adaln(c_all, ada_w, ada_b, tn=1536):
    depth, d, n = ada_w.shape
    r = c_all.shape[0]
    return pl.pallas_call(
        _adaln_kernel,
        grid=(depth, n // tn),
        in_specs=[pl.BlockSpec((r, d), lambda l, j: (0, 0)),
                  pl.BlockSpec((None, d, tn), lambda l, j: (l, 0, j)),
                  pl.BlockSpec((None, 1, tn), lambda l, j: (l, 0, j))],
        out_specs=pl.BlockSpec((None, r, tn), lambda l, j: (l, 0, j)),
        out_shape=jax.ShapeDtypeStruct((depth, r, n), F32),
        compiler_params=_cparams(("arbitrary", "arbitrary")),
        name="adaln",
    )(c_all, ada_w, ada_b.reshape(depth, 1, n))


def _even_proj_kernel(x_ref, g_ref, sc_ref, sh_ref, wqkv_ref, wabf_ref, bf_ref, *refs,
                      tm, tq, tiles_per_seq, with_cum, n_carried):
    refs = refs[n_carried:]
    if with_cum:
        q_ref, k_ref, v_ref, lf_ref, u_ref, kb_ref, vt_ref, cum_ref, cumt_ref, carry_ref = refs
    else:
        q_ref, k_ref, v_ref, lf_ref, u_ref = refs
    hm = _norm_mod(x_ref[...], g_ref[...], sc_ref[...], sh_ref[...]).astype(BF16)
    c = FOX_DIM
    qkv = jnp.dot(hm, wqkv_ref[...], preferred_element_type=F32)
    q_ref[...] = qkv[:, 0:c].astype(BF16)
    k = qkv[:, c:2 * c]
    v = qkv[:, 2 * c:3 * c]
    abf = jnp.dot(hm, wabf_ref[...], preferred_element_type=F32)
    u_ref[...] = abf[:, 0:c] * jax.nn.sigmoid(abf[:, c:2 * c])
    logf = _log_sigmoid(abf[:, 2 * c:2 * c + LANES] + bf_ref[...])
    if not with_cum:
        k_ref[...] = k
        v_ref[...] = v
        lf_ref[...] = logf[:, 0:FOX_HEADS]
    else:
        v_t = v.T
        k_ref[...] = k.T
        v_ref[...] = v_t
        lf_ref[...] = logf.T[0:FOX_HEADS, :]
        kb_ref[...] = (k * LOG2E).astype(BF16)
        for r in range(tm // tq):
            vt_ref[r] = v_t[:, r * tq:(r + 1) * tq].astype(BF16)
        i = pl.program_id(0)

        @pl.when(i % tiles_per_seq == 0)
        def _():
            carry_ref[...] = jnp.zeros_like(carry_ref)

        row = lax.broadcasted_iota(jnp.int32, (tm, tm), 0)
        col = lax.broadcasted_iota(jnp.int32, (tm, tm), 1)
        tri = (col <= row).astype(BF16)
        cum = _dot3_left(tri, logf) + carry_ref[...]
        carry_ref[...] = cum[tm - 1:tm, :]
        cum = cum * LOG2E
        cum_ref[...] = cum[:, 0:FOX_HEADS]
        cum_t = cum.T
        for r in range(tm // tq):
            cumt_ref[r] = cum_t[0:FOX_HEADS, r * tq:(r + 1) * tq]


def _dot3_left(w_bf16, x):
    hi, mid, lo = _split3(x)
    d = functools.partial(jnp.dot, preferred_element_type=F32)
    return d(w_bf16, hi) + d(w_bf16, mid) + d(w_bf16, lo)


def _even_proj(x, g, sc, sh, weights, bf_pad, *, tm, tq, seq_len, with_cum, layer=0, n_layers=1, carried=()):
    m, d = x.shape
    nmod, rows, _ = sc.shape
    tiles_per_mod = (m // nmod) // tm
    c = FOX_DIM
    row_spec = lambda width: pl.BlockSpec((tm, width), lambda i: (i, 0))
    tiles_per_seq = max(seq_len // tm, 1)
    if with_cum:
        nseq = m // seq_len
        stacked = lambda width: (
            jax.ShapeDtypeStruct((n_layers, nseq, width, seq_len), F32),
            pl.BlockSpec((None, None, width, tm), lambda i: (layer, i // tiles_per_seq, 0, i % tiles_per_seq)))
        kvl = [stacked(c), stacked(c), stacked(FOX_HEADS)]
    else:
        kvl = [(jax.ShapeDtypeStruct((m, w_), F32), row_spec(w_)) for w_ in (c, c, FOX_HEADS)]
    out_shape = [jax.ShapeDtypeStruct((m, c), BF16)] + [s for s, _ in kvl] + [jax.ShapeDtypeStruct((m, c), F32)]
    out_specs = [row_spec(c)] + [s for _, s in kvl] + [row_spec(c)]
    scratch = []
    if with_cum:
        out_shape += [jax.ShapeDtypeStruct((m, c), BF16), jax.ShapeDtypeStruct((m // tq, c, tq), BF16),
                      jax.ShapeDtypeStruct((m, FOX_HEADS), F32), jax.ShapeDtypeStruct((m // tq, FOX_HEADS, tq), F32)]
        out_specs += [row_spec(c), pl.BlockSpec((tm // tq, c, tq), lambda i: (i, 0, 0)),
                      row_spec(FOX_HEADS), pl.BlockSpec((tm // tq, FOX_HEADS, tq), lambda i: (i, 0, 0))]
        scratch = [pltpu.VMEM((1, LANES), F32)]
    n_fixed = 5 + len(weights)
    return pl.pallas_call(
        functools.partial(_even_proj_kernel, tm=tm, tq=tq, tiles_per_seq=tiles_per_seq, with_cum=with_cum,
                          n_carried=len(carried)),
        grid=(m // tm,),
        in_specs=[row_spec(d), pl.BlockSpec((1, d), lambda i: (0, 0)),
                  _mod_spec(rows, d, tiles_per_mod), _mod_spec(rows, d, tiles_per_mod),
                  *[pl.BlockSpec(w.shape, lambda i: (0, 0)) for w in weights],
                  pl.BlockSpec((1, LANES), lambda i: (0, 0)),
                  *[pl.BlockSpec(memory_space=pl.ANY) for _ in carried]],
        out_specs=out_specs,
        out_shape=out_shape,
        scratch_shapes=scratch,
        input_output_aliases={n_fixed + t: 1 + t for t in range(len(carried))},
        compiler_params=_cparams(("arbitrary",)),
        name="even_proj",
    )(x, g, sc, sh, *weights, bf_pad, *carried)


def _fox_prompt_kernel(q_ref, k_ref, vt_ref, ck_ref, cq_ref, o_ref, qm_ref, m_ref, acc_ref, *, tq):
    i = pl.program_id(1)
    hd = FOX_HEAD_DIM
    low = lax.broadcasted_iota(jnp.int32, (1, LANES), 1) < hd
    for pair in range(FOX_HEADS // 2):
        qp = q_ref[:, pair * LANES:(pair + 1) * LANES].astype(F32)
        qm_ref[2 * pair] = jnp.where(low, qp, 0.0).astype(BF16)
        qm_ref[2 * pair + 1] = jnp.where(low, 0.0, qp).astype(BF16)
    m_ref[...] = jnp.full_like(m_ref, NEG_BIG)
    acc_ref[...] = jnp.zeros_like(acc_ref)
    key = lax.broadcasted_iota(jnp.int32, (tq, tq), 0)
    qry = lax.broadcasted_iota(jnp.int32, (tq, tq), 1)
    causal = key <= qry
    ones = jnp.ones((acc_ref.shape[1] - hd, tq), BF16)

    def scores(j, masked):
        start = pl.multiple_of(j * tq, tq)
        out = []
        for h in range(FOX_HEADS):
            pair = h // 2
            kj = k_ref[pl.ds(start, tq), pair * LANES:(pair + 1) * LANES]
            t = _dot_nt(kj, qm_ref[h]) - ck_ref[pl.ds(start, tq), h:h + 1]
            if masked:
                t = jnp.where(causal, t, NEG_BIG)
            out.append((t, jnp.max(t, axis=0, keepdims=True)))
        return out

    def softmax_pv(j, scored):
        cq_all = cq_ref[i]
        for h in range(FOX_HEADS):
            t, t_max = scored[h]
            cq = cq_all[h:h + 1, :]
            m_prev = m_ref[h:h + 1, :]
            m_new = jnp.maximum(m_prev, t_max + cq)
            p = jnp.exp2(t + (cq - m_new)).astype(BF16)
            alpha = jnp.exp2(m_prev - m_new)
            v_ones = jnp.concatenate([vt_ref[j][h * hd:(h + 1) * hd, :], ones], axis=0)
            acc_ref[h] = alpha * acc_ref[h] + jnp.dot(v_ones, p, preferred_element_type=F32)
            m_ref[h:h + 1, :] = m_new

    def steps(blocks):
        scored = [scores(j, masked) for j, masked in blocks]
        for (j, _), sc in zip(blocks, scored):
            softmax_pv(j, sc)

    def body(jj, carry):
        steps([(2 * jj, False), (2 * jj + 1, False)])
        return carry

    lax.fori_loop(0, i // 2, body, 0)

    @pl.when(i % 2 == 1)
    def _():
        steps([(i - 1, False), (i, True)])

    @pl.when(i % 2 == 0)
    def _():
        steps([(i, True)])
    for pair in range(FOX_HEADS // 2):
        halves = []
        for h in (2 * pair, 2 * pair + 1):
            halves.append(acc_ref[h, 0:hd, :] * (1.0 / acc_ref[h, hd:hd + 1, :]))
        o_ref[:, pair * LANES:(pair + 1) * LANES] = jnp.concatenate(halves, axis=0).T.astype(o_ref.dtype)


def _fox_prompt(q, kb, vt, cum, cumt, *, batch, seq_len, tq):
    m, c = q.shape
    nq = seq_len // tq
    return pl.pallas_call(
        functools.partial(_fox_prompt_kernel, tq=tq),
        grid=(batch, nq),
        in_specs=[pl.BlockSpec((tq, c), lambda b, i: (b * nq + i, 0)),
                  pl.BlockSpec((seq_len, c), lambda b, i: (b, 0)),
                  pl.BlockSpec((nq, c, tq), lambda b, i: (b, 0, 0)),
                  pl.BlockSpec((seq_len, FOX_HEADS), lambda b, i: (b, 0)),
                  pl.BlockSpec((nq, FOX_HEADS, tq), lambda b, i: (b, 0, 0))],
        out_specs=pl.BlockSpec((tq, c), lambda b, i: (b * nq + i, 0)),
        out_shape=jax.ShapeDtypeStruct((m, c), BF16),
        scratch_shapes=[pltpu.VMEM((FOX_HEADS, tq, LANES), BF16), pltpu.VMEM((FOX_HEADS, tq), F32),
                        pltpu.VMEM((FOX_HEADS, FOX_HEAD_DIM + 16, tq), F32)],
        compiler_params=_cparams(("arbitrary", "arbitrary")),
        name="fox_prompt",
    )(q, kb, vt, cum, cumt)


def _group_matrices(channels, groups):
    gsz = channels // groups
    ch = lax.broadcasted_iota(jnp.int32, (channels, LANES), 0)
    gr = lax.broadcasted_iota(jnp.int32, (channels, LANES), 1)
    gather = (ch // gsz == gr).astype(BF16)
    gr_t = lax.broadcasted_iota(jnp.int32, (LANES, channels), 0)
    ch_t = lax.broadcasted_iota(jnp.int32, (LANES, channels), 1)
    spread = (ch_t // gsz == gr_t).astype(BF16)
    return gather, spread, 1.0 / gsz


def _group_norm_rows(y, groups):
    gather, spread, inv = _group_matrices(y.shape[-1], groups)
    mu = _dot2(y, gather) * inv
    d = y - _dot2(mu, spread)
    var = _dot2(d * d, gather) * inv
    return d * _dot2(lax.rsqrt(var + EPS), spread)


def _conv_finish(acc, cb, gn_g, gn_b):
    cv = _group_norm_rows(acc + cb, CONV_GROUPS) * gn_g + gn_b
    return _silu(cv)


def _conv_prompt_kernel(u_ref, halo_ref, cw_ref, cb_ref, gg_ref, gb_ref, o_ref, xp_ref, sh_ref, y_ref, *, tl):
    i = pl.program_id(1)
    halo = halo_ref[...]
    xp_ref[0:CONV_HALO, :] = jnp.where(i == 0, jnp.zeros_like(halo), halo)
    xp_ref[CONV_HALO:CONV_HALO + tl, :] = u_ref[...]
    span = sh_ref.shape[1]
    for r in range(1, SUBLANES):
        sh_ref[r - 1] = xp_ref[r:r + span, :]
    base = CONV_HALO - (CONV_WIDTH - 1)
    rows = 64
    for r0 in range(0, tl, rows):
        for c0 in range(0, u_ref.shape[1], LANES):
            acc = jnp.zeros((rows, LANES), F32)
            for w in range(CONV_WIDTH):
                shift = (base + w) % SUBLANES
                start = r0 + base + w - shift
                if shift == 0:
                    tap = xp_ref[start:start + rows, c0:c0 + LANES]
                else:
                    tap = sh_ref[shift - 1, start:start + rows, c0:c0 + LANES]
                acc = acc + tap * cw_ref[w:w + 1, c0:c0 + LANES]
            y_ref[r0:r0 + rows, c0:c0 + LANES] = acc
    y = _conv_finish(y_ref[...], cb_ref[...], gg_ref[...], gb_ref[...])
    o_ref[...] = y.astype(o_ref.dtype)


def _conv_prompt(u, cw, cb, gn_g, gn_b, *, batch, seq_len, tl):
    m, c = u.shape
    nl = seq_len // tl
    hb = tl // CONV_HALO
    vec = pl.BlockSpec((1, c), lambda b, i: (0, 0))
    return pl.pallas_call(
        functools.partial(_conv_prompt_kernel, tl=tl),
        grid=(batch, nl),
        in_specs=[pl.BlockSpec((tl, c), lambda b, i: (b * nl + i, 0)),
                  pl.BlockSpec((CONV_HALO, c), lambda b, i: (jnp.maximum((b * nl + i) * hb - 1, 0), 0)),
                  pl.BlockSpec((CONV_HALO, c), lambda b, i: (0, 0)), vec, vec, vec],
        out_specs=pl.BlockSpec((tl, c), lambda b, i: (b * nl + i, 0)),
        out_shape=jax.ShapeDtypeStruct((m, c), BF16),
        scratch_shapes=[pltpu.VMEM((CONV_HALO + tl, c), F32),
                        pltpu.VMEM((SUBLANES - 1, CONV_HALO - SUBLANES + tl, c), F32), pltpu.VMEM((tl, c), F32)],
        compiler_params=_cparams(("arbitrary", "arbitrary")),
        name="conv_prompt",
    )(u, u, cw, cb, gn_g, gn_b)


def _conv_step_kernel(st_ref, u_ref, cw_ref, cb_ref, gg_ref, gb_ref, o_ref):
    acc = u_ref[...] * cw_ref[CONV_WIDTH - 1:CONV_WIDTH, :]
    for w in range(CONV_WIDTH - 1):
        acc = acc + st_ref[w] * cw_ref[w:w + 1, :]
    o_ref[...] = _conv_finish(acc, cb_ref[...], gg_ref[...], gb_ref[...]).astype(o_ref.dtype)


def _conv_step(state_t, u, cw, cb, gn_g, gn_b):
    bd, c = u.shape
    full = lambda shape: pl.BlockSpec(shape, lambda i: (0,) * len(shape))
    return pl.pallas_call(
        _conv_step_kernel,
        grid=(1,),
        in_specs=[full(state_t.shape), full((bd, c)), full(cw.shape), full((1, c)), full((1, c)), full((1, c))],
        out_specs=full((bd, c)),
        out_shape=jax.ShapeDtypeStruct((bd, c), BF16),
        compiler_params=_cparams(("arbitrary",)),
        name="conv_step",
    )(state_t, u, cw, cb, gn_g, gn_b)


def _mix_ffn_kernel(*refs, n_in, final):
    a_refs = refs[:n_in]
    w_refs = refs[n_in:2 * n_in]
    (x_ref, g1_ref, g_ref, sc_ref, sh_ref, g2_ref, wg_ref, wu_ref, wd_ref, fg_ref,
     o_ref, x1_ref, hf_ref, acc_ref) = refs[2 * n_in:]
    j = pl.program_id(1)

    @pl.when(j == 0)
    def _():
        mix = jnp.dot(a_refs[0][...], w_refs[0][...], preferred_element_type=F32)
        for a_ref, w_ref in zip(a_refs[1:], w_refs[1:]):
            mix = mix + jnp.dot(a_ref[...], w_ref[...], preferred_element_type=F32)
        x1 = x_ref[...] + g1_ref[...] * mix
        x1_ref[...] = x1
        hf_ref[...] = _norm_mod(x1, g_ref[...], sc_ref[...], sh_ref[...]).astype(BF16)
        acc_ref[...] = jnp.zeros_like(acc_ref)

    hf = hf_ref[...]
    gt = jnp.dot(hf, wg_ref[...], preferred_element_type=F32)
    up = jnp.dot(hf, wu_ref[...], preferred_element_type=F32)
    acc_ref[...] += jnp.dot((_silu(gt) * up).astype(BF16), wd_ref[...], preferred_element_type=F32)

    @pl.when(j == pl.num_programs(1) - 1)
    def _():
        y = x1_ref[...] + g2_ref[...] * acc_ref[...]
        if final:
            ms = jnp.mean(y * y, axis=-1, keepdims=True)
            y = y * lax.rsqrt(ms + EPS) * fg_ref[...]
        o_ref[...] = y


def _mix_ffn(acts, w_outs, x, g1, g, sc, sh, g2, wg, wu, wd, fg, *, tm, th, final):
    m, d = x.shape
    nmod, rows, _ = sc.shape
    hdim = wg.shape[1]
    tiles_per_mod = (m // nmod) // tm
    vec = pl.BlockSpec((1, d), lambda i, j: (0, 0))
    mod = _mod_spec(rows, d, tiles_per_mod)
    in_specs = ([pl.BlockSpec((tm, a.shape[1]), lambda i, j: (i, 0)) for a in acts]
                + [pl.BlockSpec(w.shape, lambda i, j: (0, 0)) for w in w_outs]
                + [pl.BlockSpec((tm, d), lambda i, j: (i, 0)), mod, vec, mod, mod, mod,
                   pl.BlockSpec((d, th), lambda i, j: (0, j)), pl.BlockSpec((d, th), lambda i, j: (0, j)),
                   pl.BlockSpec((th, d), lambda i, j: (j, 0)), vec])
    return pl.pallas_call(
        functools.partial(_mix_ffn_kernel, n_in=len(acts), final=final),
        grid=(m // tm, hdim // th),
        in_specs=in_specs,
        out_specs=pl.BlockSpec((tm, d), lambda i, j: (i, 0)),
        out_shape=jax.ShapeDtypeStruct((m, d), F32),
        scratch_shapes=[pltpu.VMEM((tm, d), F32), pltpu.VMEM((tm, d), BF16), pltpu.VMEM((tm, d), F32)],
        compiler_params=_cparams(("arbitrary", "arbitrary")),
        name="mix_ffn",
    )(*acts, *w_outs, x, g1, g, sc, sh, g2, wg, wu, wd, fg)


def _odd_proj_kernel(x_ref, g_ref, sc_ref, sh_ref, w_ref, cos_ref, sin_ref, o_ref, hm_ref):
    j = pl.program_id(1)

    @pl.when(j == 0)
    def _():
        hm_ref[...] = _norm_mod(x_ref[...], g_ref[...], sc_ref[...], sh_ref[...]).astype(BF16)

    def slab():
        return jnp.dot(hm_ref[...], w_ref[...], preferred_element_type=F32)

    @pl.when(j == 0)
    def _():
        z = slab()
        cos = cos_ref[...]
        sin = sin_ref[...]
        for c0 in range(0, z.shape[1], 2 * LANES):
            x1 = z[:, c0:c0 + LANES]
            x2 = z[:, c0 + LANES:c0 + 2 * LANES]
            o_ref[:, c0:c0 + LANES] = (x1 * cos - x2 * sin).astype(o_ref.dtype)
            o_ref[:, c0 + LANES:c0 + 2 * LANES] = (x1 * sin + x2 * cos).astype(o_ref.dtype)

    @pl.when(j == 1)
    def _():
        o_ref[...] = slab().astype(o_ref.dtype)

    @pl.when(j == 2)
    def _():
        o_ref[...] = _silu(slab()).astype(o_ref.dtype)


def _odd_proj(x, g, sc, sh, w, cos, sin, *, tm, tn):
    m, d = x.shape
    nmod, rows, _ = sc.shape
    n = w.shape[1]
    tiles_per_mod = (m // nmod) // tm
    assert n == 3 * tn, "q|k, v and gate slabs must have equal widths"
    pos_tiles = cos.shape[0] // tm
    mod = _mod_spec(rows, d, tiles_per_mod)
    return pl.pallas_call(
        _odd_proj_kernel,
        grid=(m // tm, 3),
        in_specs=[pl.BlockSpec((tm, d), lambda i, j: (i, 0)), pl.BlockSpec((1, d), lambda i, j: (0, 0)), mod, mod,
                  pl.BlockSpec((d, tn), lambda i, j: (0, j)),
                  pl.BlockSpec((tm, LANES), lambda i, j: (i % pos_tiles, 0)),
                  pl.BlockSpec((tm, LANES), lambda i, j: (i % pos_tiles, 0))],
        out_specs=pl.BlockSpec((tm, tn), lambda i, j: (i, j)),
        out_shape=jax.ShapeDtypeStruct((m, n), BF16),
        scratch_shapes=[pltpu.VMEM((tm, d), BF16)],
        compiler_params=_cparams(("arbitrary", "arbitrary")),
        name="odd_proj",
    )(x, g, sc, sh, w, cos, sin)


def _ret_log_gamma(h):
    return float(np.log(np.float64(1.0) - np.float64(2.0) ** (-5.0 - h)))


def _ret_finish(o, sg, gain):
    mu = jnp.mean(o, axis=-1, keepdims=True)
    d = o - mu
    var = jnp.mean(d * d, axis=-1, keepdims=True)
    r = d * lax.rsqrt(var + EPS) * gain
    return sg.astype(F32) * r


def _ret_prompt_kernel(q_ref, k_ref, v_ref, sg_ref, gain_ref, y_ref, s_out_ref, s_ref, dmask_ref, *, chunk, dk, dv):
    c = pl.program_id(1)

    @pl.when(c == 0)
    def _():
        s_ref[...] = jnp.zeros_like(s_ref)

    @pl.when(jnp.logical_and(pl.program_id(0) == 0, c == 0))
    def _():
        ri = lax.broadcasted_iota(jnp.int32, (chunk, chunk), 0)
        ci = lax.broadcasted_iota(jnp.int32, (chunk, chunk), 1)
        diff = (ri - ci).astype(F32)
        for h in range(RET_HEADS):
            dmask_ref[h] = jnp.where(ri >= ci, jnp.exp(jnp.maximum(diff, 0.0) * _ret_log_gamma(h)), 0.0)

    pos = lax.broadcasted_iota(jnp.int32, (chunk, 1), 0).astype(F32)
    for h in range(RET_HEADS):
        lg = _ret_log_gamma(h)
        dmask = dmask_ref[h]
        q_dec = jnp.exp((pos + 1.0) * lg)
        k_dec = jnp.exp((chunk - 1.0 - pos) * lg)
        c_dec = math.exp(chunk * lg)
        q = q_ref[:, h * dk:(h + 1) * dk]
        k = k_ref[:, h * dk:(h + 1) * dk]
        v = v_ref[:, h * dv:(h + 1) * dv]
        s_prev = s_ref[h]
        a = _dot_nt(q, k) * dmask
        o = (jnp.dot(a.astype(BF16), v, preferred_element_type=F32)
             + jnp.dot(q, s_prev.astype(BF16), preferred_element_type=F32) * q_dec)
        kd = (k.astype(F32) * k_dec).T.astype(BF16)
        s_ref[h] = s_prev * c_dec + jnp.dot(kd, v, preferred_element_type=F32)
        cols = slice(h * dv, (h + 1) * dv)
        y_ref[:, cols] = _ret_finish(o, sg_ref[:, cols], gain_ref[:, cols]).astype(y_ref.dtype)

    @pl.when(c == pl.num_programs(1) - 1)
    def _():
        s_out_ref[...] = s_ref[...]


def _ret_prompt(z, gain, *, batch, seq_len, chunk, dk, dv):
    m = z.shape[0]
    nc = seq_len // chunk
    nk = RET_HEADS * dk
    nv = RET_HEADS * dv
    rowblk = lambda b, c: b * nc + c
    return pl.pallas_call(
        functools.partial(_ret_prompt_kernel, chunk=chunk, dk=dk, dv=dv),
        grid=(batch, nc),
        in_specs=[pl.BlockSpec((chunk, nk), lambda b, c: (rowblk(b, c), 0)),
                  pl.BlockSpec((chunk, nk), lambda b, c: (rowblk(b, c), 1)),
                  pl.BlockSpec((chunk, nv), lambda b, c: (rowblk(b, c), (2 * nk) // nv)),
                  pl.BlockSpec((chunk, nv), lambda b, c: (rowblk(b, c), (2 * nk) // nv + 1)),
                  pl.BlockSpec((1, nv), lambda b, c: (0, 0))],
        out_specs=[pl.BlockSpec((chunk, nv), lambda b, c: (rowblk(b, c), 0)),
                   pl.BlockSpec((None, RET_HEADS, dk, dv), lambda b, c: (b, 0, 0, 0))],
        out_shape=[jax.ShapeDtypeStruct((m, nv), BF16), jax.ShapeDtypeStruct((batch, RET_HEADS, dk, dv), F32)],
        scratch_shapes=[pltpu.VMEM((RET_HEADS, dk, dv), F32), pltpu.VMEM((RET_HEADS, chunk, chunk), F32)],
        compiler_params=_cparams(("arbitrary", "arbitrary")),
        name="ret_prompt",
    )(z, z, z, z, gain)


def _ret_step_kernel(z_ref, s_ref, gain_ref, *refs, dk, dv):
    y_ref, s_out_ref = refs[-2:]
    nk = RET_HEADS * dk
    nv = RET_HEADS * dv
    row0 = lax.broadcasted_iota(jnp.int32, (LANES, 1), 0) == 0
    for h in range(RET_HEADS):
        gamma = math.exp(_ret_log_gamma(h))
        q = z_ref[:, h * dk:(h + 1) * dk]
        k = z_ref[:, nk + h * dk:nk + (h + 1) * dk]
        v = z_ref[:, 2 * nk + h * dv:2 * nk + (h + 1) * dv]
        sg = z_ref[:, 2 * nk + nv + h * dv:2 * nk + nv + (h + 1) * dv]
        s_prev = s_ref[h]
        q_rows = jnp.broadcast_to(q.astype(F32), (16, dk)).astype(BF16)
        a = jnp.sum(q.astype(F32) * k.astype(F32), axis=-1, keepdims=True)
        o = a * v.astype(F32) + jnp.dot(q_rows, s_prev.astype(BF16), preferred_element_type=F32)[0:1, :] * gamma
        k_rows = jnp.where(row0, jnp.broadcast_to(k.astype(F32), (LANES, dk)), 0.0)
        v_rows = jnp.where(row0, jnp.broadcast_to(v.astype(F32), (LANES, dv)), 0.0)
        s_out_ref[h] = s_prev * gamma + jnp.dot(k_rows.T.astype(BF16), v_rows.astype(BF16),
                                                preferred_element_type=F32)
        cols = slice(h * dv, (h + 1) * dv)
        y_ref[:, cols] = _ret_finish(o, sg, gain_ref[:, cols]).astype(y_ref.dtype)


def _ret_step(z, states, layer, gain, carried, *, dk, dv):
    bd, n = z.shape
    nv = RET_HEADS * dv
    slab = pl.BlockSpec((None, None, RET_HEADS, dk, dv), lambda b: (layer, b, 0, 0, 0))
    operands = [z.reshape(bd, 1, n), states, gain]
    in_specs = [pl.BlockSpec((None, 1, n), lambda b: (b, 0, 0)), slab, pl.BlockSpec((1, nv), lambda b: (0, 0))]
    aliases = {}
    if carried is not None:
        operands.append(carried)
        in_specs.append(pl.BlockSpec(memory_space=pl.ANY))
        aliases = {3: 1}
    y, s_all = pl.pallas_call(
        functools.partial(_ret_step_kernel, dk=dk, dv=dv),
        grid=(bd,),
        in_specs=in_specs,
        out_specs=[pl.BlockSpec((None, 1, nv), lambda b: (b, 0, 0)), slab],
        out_shape=[jax.ShapeDtypeStruct((bd, 1, nv), BF16), jax.ShapeDtypeStruct(states.shape, F32)],
        input_output_aliases=aliases,
        compiler_params=_cparams(("arbitrary",)),
        name="ret_step",
    )(*operands)
    return y.reshape(bd, nv), s_all


def _fox_decode_kernel(pt_ref, q_ref, qb_ref, kn_ref, vn_ref, lfn_ref, *refs, group, n_pages):
    k_refs = refs[0:group]
    v_refs = refs[group:2 * group]
    lpt_ref = refs[2 * group]
    o_ref, m_ref, l_ref, w_ref, r_ref, acc_ref = refs[2 * group + 1:]
    c = pl.program_id(1)
    first_page = pl.program_id(0) * n_pages + (n_pages - 1) - c * group
    head = lax.broadcasted_iota(jnp.int32, (FOX_HEADS, FOX_DIM), 0)
    chan = lax.broadcasted_iota(jnp.int32, (FOX_HEADS, FOX_DIM), 1)
    own = chan // FOX_HEAD_DIM == head

    def per_head_to_channels(col):
        return jnp.sum(jnp.where(own, jnp.broadcast_to(col, own.shape), 0.0), axis=0, keepdims=True)

    @pl.when(c == 0)
    def _():
        prod = jnp.broadcast_to(q_ref[...] * kn_ref[...], own.shape)
        m_ref[...] = jnp.sum(jnp.where(own, prod, 0.0), axis=-1, keepdims=True)
        l_ref[...] = jnp.ones_like(l_ref)
        w_ref[...] = jnp.ones_like(w_ref)
        r_ref[...] = jnp.zeros_like(r_ref)
        acc_ref[...] = jnp.zeros_like(acc_ref)

    lane = lax.broadcasted_iota(jnp.int32, (1, PAGE_SIZE), 1)
    head_row = lax.broadcasted_iota(jnp.int32, (FOX_HEADS, 1), 0)
    lfn = lfn_ref[...]
    later = r_ref[...]
    scores = []
    for g in range(group):
        lp = lpt_ref[pt_ref[first_page - g]]
        suffix = lp
        stride = 1
        while stride < PAGE_SIZE:
            ahead = pltpu.roll(suffix, PAGE_SIZE - stride, axis=1)
            suffix = suffix + jnp.where(lane < PAGE_SIZE - stride, ahead, 0.0)
            stride *= 2
        s = lfn + later + (suffix - lp)
        later = later + jnp.sum(lp, axis=-1, keepdims=True)
        for h in range(FOX_HEADS):
            qk = jnp.sum(k_refs[g][h] * qb_ref[h], axis=0, keepdims=True)
            s = s + jnp.where(head_row == h, jnp.broadcast_to(qk, s.shape), 0.0)
        scores.append(s)
    r_ref[...] = later
    m_prev = m_ref[...]
    m_new = m_prev
    for s in scores:
        m_new = jnp.maximum(m_new, jnp.max(s, axis=-1, keepdims=True))
    alpha = jnp.exp(m_prev - m_new)
    l_new = alpha * l_ref[...]
    probs = []
    for s in scores:
        p = jnp.exp(s - m_new)
        l_new = l_new + jnp.sum(p, axis=-1, keepdims=True)
        probs.append(p)
    for h in range(FOX_HEADS):
        acc = acc_ref[h] * alpha[h:h + 1, :]
        for g in range(group):
            acc = acc + v_refs[g][h] * probs[g][h:h + 1, :]
        acc_ref[h] = acc
    m_ref[...] = m_new
    l_ref[...] = l_new
    w_ref[...] = alpha * w_ref[...]

    @pl.when(c == pl.num_programs(1) - 1)
    def _():
        acc_t = acc_ref[...].reshape(FOX_DIM, PAGE_SIZE).T
        pv = jnp.sum(acc_t, axis=0, keepdims=True)
        o = (pv + per_head_to_channels(w_ref[...]) * vn_ref[...]) / per_head_to_channels(l_ref[...])
        o_ref[...] = o.astype(o_ref.dtype)


def _fox_decode(q, k_new, v_new, logf_new, cache_kt, cache_vt, cache_lpt, page_table, layer, *, group):
    bd, c = q.shape
    n_pages = page_table.shape[1]
    group = min(group, n_pages)
    assert n_pages % group == 0
    pt_flat = page_table.reshape(-1)
    q = q.astype(F32)
    q_lanes = jnp.broadcast_to(q.reshape(bd, FOX_HEADS, FOX_HEAD_DIM, 1), (bd, FOX_HEADS, FOX_HEAD_DIM, PAGE_SIZE))

    def page_spec(g, shape):
        def index(b, s, pt):
            return (layer, pt[b * n_pages + (n_pages - 1 - (s * group + g))]) + (0,) * len(shape)
        return pl.BlockSpec((None, None) + shape, index)

    row = pl.BlockSpec((None, 1, c), lambda b, s, pt: (b, 0, 0))
    slab = (FOX_HEADS, FOX_HEAD_DIM, PAGE_SIZE)
    in_specs = ([row, pl.BlockSpec((None,) + slab, lambda b, s, pt: (b, 0, 0, 0)), row, row,
                 pl.BlockSpec((None, FOX_HEADS, 1), lambda b, s, pt: (b, 0, 0))]
                + [page_spec(g, slab) for g in range(group)]
                + [page_spec(g, slab) for g in range(group)]
                + [pl.BlockSpec((None,) + cache_lpt.shape[1:], lambda b, s, pt: (layer, 0, 0, 0))])
    col = pltpu.VMEM((FOX_HEADS, 1), F32)
    out = pl.pallas_call(
        functools.partial(_fox_decode_kernel, group=group, n_pages=n_pages),
        grid_spec=pltpu.PrefetchScalarGridSpec(
            num_scalar_prefetch=1,
            grid=(bd, n_pages // group),
            in_specs=in_specs,
            out_specs=row,
            scratch_shapes=[col, col, col, col, pltpu.VMEM(slab, F32)]),
        out_shape=jax.ShapeDtypeStruct((bd, 1, c), BF16),
        compiler_params=_cparams(("arbitrary", "arbitrary")),
        name="fox_decode",
    )(pt_flat, q.reshape(bd, 1, c), q_lanes, k_new.reshape(bd, 1, c), v_new.reshape(bd, 1, c),
      logf_new.reshape(bd, FOX_HEADS, 1), *([cache_kt] * group), *([cache_vt] * group), cache_lpt)
    return out.reshape(bd, c)


def _rope_tables(pos, half):
    inv = 1.0 / (ROPE_BASE ** jnp.linspace(0.0, 1.0, half, dtype=F32))
    ang = pos.astype(F32)[:, None] * inv[None, :]
    return jnp.cos(ang), jnp.sin(ang)


def _prep_weights(w_in_even, b_forget, conv_w, w_out_even, w_in_odd, w_ffn_gate, w_ffn_up, w_ffn_down, dk):
    c = FOX_DIM
    n_even, d, _ = w_in_even.shape
    qkv_scale = jnp.where(jnp.arange(3 * c) < c, FOX_HEAD_DIM ** -0.5, 1.0).astype(F32)
    w_qkv = (w_in_even[:, :, :3 * c] * qkv_scale).astype(BF16)
    w_f = jnp.pad(w_in_even[:, :, 3 * c:3 * c + FOX_HEADS], ((0, 0), (0, 0), (0, LANES - FOX_HEADS)))
    w_abf = jnp.concatenate([w_in_even[:, :, 3 * c + FOX_HEADS:], w_f], axis=-1).astype(BF16)
    bf_pad = jnp.pad(b_forget, ((0, 0), (0, LANES - FOX_HEADS)))[:, None, :]
    cw_pad = jnp.pad(conv_w, ((0, 0), (0, CONV_HALO - CONV_WIDTH), (0, 0)))
    nk = RET_HEADS * dk
    col = jnp.arange(w_in_odd.shape[-1])
    odd_scale = jnp.where((col >= nk) & (col < 2 * nk), dk ** -0.5, 1.0).astype(F32)
    w_odd = (w_in_odd * odd_scale).astype(BF16)
    return dict(w_even=(w_qkv, w_abf), bf_pad=bf_pad, cw_pad=cw_pad,
                w_out_fox=w_out_even[:, :c].astype(BF16), w_out_conv=w_out_even[:, c:].astype(BF16),
                w_odd=w_odd, wg=w_ffn_gate.astype(BF16), wu=w_ffn_up.astype(BF16), wd=w_ffn_down.astype(BF16))


def _tiles(rows, seq_len):
    if seq_len == 1:
        return dict(tm=rows, tm_ffn=rows, th=1408, tm_odd=rows)
    return dict(tm=min(512, seq_len), tm_ffn=min(512, seq_len), th=1408, tm_odd=min(1024, seq_len))


def _trunk(x, mod, pos, P, W, *, seq_len, even_mixer, odd_mixer):
    depth = mod.shape[0]
    d = x.shape[1]
    t = _tiles(x.shape[0], seq_len)
    tm = t['tm']
    cos, sin = _rope_tables(pos, LANES)
    ks, vs, lfs, cbs, rs = [], [], [], [], []
    for l in range(depth):
        sh1, sc1, g1, sh2, sc2, g2 = [mod[l][..., s * d:(s + 1) * d] for s in range(6)]
        gm = W['norm_mix_g'][l][None, :]
        if l % 2 == 0:
            e = l // 2
            stacked = dict(layer=e, n_layers=P['bf_pad'].shape[0],
                           carried=(ks[-1], vs[-1], lfs[-1]) if ks else ()) if even_mixer.needs_cum else {}
            out = _even_proj(x, gm, sc1, sh1, [w[e] for w in P['w_even']], P['bf_pad'][e], tm=tm, tq=min(FOX_TQ, tm),
                             seq_len=seq_len, with_cum=even_mixer.needs_cum, **stacked)
            q, k, v, logf, u = out[:5]
            o_fox, cv, new_buf = even_mixer(e, q, k, v, logf, u, out[5:])
            ks.append(k); vs.append(v); lfs.append(logf); cbs.append(new_buf)
            acts, w_outs = [o_fox, cv], [P['w_out_fox'][e], P['w_out_conv'][e]]
        else:
            o = l // 2
            z = _odd_proj(x, gm, sc1, sh1, P['w_odd'][o], cos, sin, tm=t['tm_odd'], tn=P['w_odd'].shape[-1] // 3)
            y, s_new = odd_mixer(o, z)
            rs.append(s_new)
            acts, w_outs = [y], [P['w_out_odd'][o]]
        x = _mix_ffn(acts, w_outs, x, g1, W['norm_ffn_g'][l][None, :], sc2, sh2, g2, P['wg'][l], P['wu'][l],
                     P['wd'][l], W['final_norm_g'][None, :], tm=t['tm_ffn'], th=t['th'], final=(l == depth - 1))
    return x, (ks, vs, lfs, cbs, rs)


class _PromptEven:
    needs_cum = True

    def __init__(self, batch, seq_len, W, P):
        self.batch, self.seq_len, self.W, self.P = batch, seq_len, W, P

    def __call__(self, e, q, k, v, logf, u, extra):
        kb, vt, cum, cumt = extra
        W, P = self.W, self.P
        o_fox = _fox_prompt(q, kb, vt, cum, cumt, batch=self.batch, seq_len=self.seq_len, tq=FOX_TQ)
        cv = _conv_prompt(u, P['cw_pad'][e], W['conv_b'][e][None, :], W['conv_norm_g'][e][None, :],
                          W['conv_norm_b'][e][None, :], batch=self.batch, seq_len=self.seq_len, tl=256)
        c = u.shape[1]
        new_buf = u.reshape(self.batch, self.seq_len, c)[:, self.seq_len - (CONV_WIDTH - 1):]
        return o_fox, cv, new_buf


class _SampleEven:
    needs_cum = False

    def __init__(self, W, P, cache_k, cache_v, cache_lp, state_conv, page_table):
        self.W, self.P = W, P
        self.cache_k, self.cache_v, self.cache_lp = cache_k, cache_v, cache_lp
        self.state_conv, self.page_table = state_conv, page_table

    def __call__(self, e, q, k, v, logf, u, extra):
        W, P = self.W, self.P
        o_fox = _fox_decode(q, k, v, logf, self.cache_k, self.cache_v, self.cache_lp, self.page_table, e, group=16)
        st = self.state_conv[e]
        cv = _conv_step(jnp.transpose(st, (1, 0, 2)), u, P['cw_pad'][e], W['conv_b'][e][None, :],
                        W['conv_norm_g'][e][None, :], W['conv_norm_b'][e][None, :])
        new_buf = jnp.concatenate([st[:, 1:], u[:, None, :]], axis=1)
        return o_fox, cv, new_buf


def kernel(x_prompt, x_sample, cache_k, cache_v, cache_logf, state_conv, state_ret, page_table, c_prompt, c_sample,
           ada_w, ada_b, norm_mix_g, norm_ffn_g, w_in_even, b_forget, conv_w, conv_b, conv_norm_g, conv_norm_b,
           w_out_even, w_in_odd, ret_norm_g, w_out_odd, w_ffn_gate, w_ffn_up, w_ffn_down, final_norm_g):
    bp, lp, d = x_prompt.shape
    bs, ls, _ = x_sample.shape
    assert ls == 1, "the decode path handles one new token per sequence"
    depth = ada_w.shape[0]
    n_even, n_phys, page, heads, hd = cache_k.shape
    assert (page, heads, hd) == (PAGE_SIZE, FOX_HEADS, FOX_HEAD_DIM)
    dk, dv = state_ret.shape[-2:]
    past_len = page_table.shape[1] * PAGE_SIZE

    W = dict(norm_mix_g=norm_mix_g, norm_ffn_g=norm_ffn_g, conv_b=conv_b, conv_norm_g=conv_norm_g,
             conv_norm_b=conv_norm_b, final_norm_g=final_norm_g)
    P = _prep_weights(w_in_even, b_forget, conv_w, w_out_even, w_in_odd, w_ffn_gate, w_ffn_up, w_ffn_down, dk)
    P['w_out_odd'] = w_out_odd.astype(BF16)
    gain = ret_norm_g[:, None, :]

    mod = _adaln(jnp.concatenate([c_prompt, c_sample], axis=0), ada_w, ada_b)
    mod_p = mod[:, :bp].reshape(depth, bp, 1, 6 * d)
    mod_s = mod[:, bp:].reshape(depth, 1, bs, 6 * d)

    def odd_prompt(o, z):
        return _ret_prompt(z, gain[o], batch=bp, seq_len=lp, chunk=min(256, lp), dk=dk, dv=dv)

    y_p, (k_p, v_p, lf_p, cb_p, r_p) = _trunk(
        x_prompt.reshape(bp * lp, d), mod_p, jnp.arange(lp), P, W, seq_len=lp,
        even_mixer=_PromptEven(bp, lp, W, P), odd_mixer=odd_prompt)

    ckt = jnp.transpose(cache_k, (0, 1, 3, 4, 2))
    cvt = jnp.transpose(cache_v, (0, 1, 3, 4, 2))
    clp = jnp.transpose(cache_logf, (0, 1, 3, 2))

    new_states = []

    def odd_sample(o, z):
        y, s_all = _ret_step(z, state_ret, o, gain[o], new_states[-1] if new_states else None, dk=dk, dv=dv)
        new_states.append(s_all)
        return y, s_all

    y_s, (k_s, v_s, lf_s, cb_s, r_s) = _trunk(
        x_sample.reshape(bs, d), mod_s, jnp.full((bs,), past_len, jnp.int32), P, W, seq_len=1,
        even_mixer=_SampleEven(W, P, ckt, cvt, clp, state_conv, page_table), odd_mixer=odd_sample)

    hshape = (FOX_HEADS, FOX_HEAD_DIM)

    def seq_last_to_heads(t):
        return jnp.transpose(t.reshape((n_even, bp) + hshape + (lp,)), (0, 1, 4, 2, 3))

    return (y_p.reshape(bp, lp, d), y_s.reshape(bs, ls, d),
            seq_last_to_heads(k_p[-1]), seq_last_to_heads(v_p[-1]),
            jnp.transpose(lf_p[-1], (0, 1, 3, 2)), jnp.stack(cb_p), jnp.stack(r_p),
            jnp.stack(k_s).reshape((n_even, bs, ls) + hshape), jnp.stack(v_s).reshape((n_even, bs, ls) + hshape),
            jnp.stack(lf_s).reshape(n_even, bs, ls, FOX_HEADS), jnp.stack(cb_s),
            r_s[-1].astype(state_ret.dtype))
```

```python
import functools
import math

import jax
import jax.numpy as jnp
import numpy as np
from jax import lax
from jax.experimental import pallas as pl
from jax.experimental.pallas import tpu as pltpu

F32 = jnp.float32
BF16 = jnp.bfloat16

EPS = 1e-6
ROPE_BASE = 10000.0
FOX_HEADS = 8
FOX_HEAD_DIM = 64
FOX_DIM = FOX_HEADS * FOX_HEAD_DIM
CONV_GROUPS = 8
CONV_WIDTH = 31
RET_HEADS = 4
PAGE_SIZE = 128
LANES = 128
SUBLANES = 8
CONV_HALO = 32
VMEM_LIMIT = 56 * 1024 * 1024
NEG_BIG = -1e30
LOG2E = math.log2(math.e)
FOX_TQ = 256


def _cparams(sem):
    return pltpu.CompilerParams(dimension_semantics=sem, vmem_limit_bytes=VMEM_LIMIT)


def _silu(x):
    return x * jax.nn.sigmoid(x)


def _log_sigmoid(x):
    return jnp.minimum(x, 0.0) - jnp.log1p(jnp.exp(-jnp.abs(x)))


def _norm_mod(x, g, sc, sh):
    ms = jnp.mean(x * x, axis=-1, keepdims=True)
    y = x * lax.rsqrt(ms + EPS) * g
    return y * (1.0 + sc) + sh


def _split3(x):
    hi = x.astype(BF16)
    r1 = x - hi.astype(F32)
    mid = r1.astype(BF16)
    lo = (r1 - mid.astype(F32)).astype(BF16)
    return hi, mid, lo


def _dot3(x, w_bf16):
    hi, mid, lo = _split3(x)
    d = functools.partial(jnp.dot, preferred_element_type=F32)
    return d(hi, w_bf16) + d(mid, w_bf16) + d(lo, w_bf16)


def _dot2(x, w_bf16):
    hi = x.astype(BF16)
    lo = (x - hi.astype(F32)).astype(BF16)
    d = functools.partial(jnp.dot, preferred_element_type=F32)
    return d(hi, w_bf16) + d(lo, w_bf16)


def _dot_nt(a, b):
    return lax.dot_general(a, b, (((1,), (1,)), ((), ())), preferred_element_type=F32)


def _mod_spec(rows, d, tiles_per_mod):
    return pl.BlockSpec((None, rows, d), lambda i, *_: (i // tiles_per_mod, 0, 0))


def _adaln_kernel(c_ref, w_ref, b_ref, o_ref):
    cm = _silu(c_ref[...]).astype(BF16)
    o_ref[...] = jnp.dot(cm, w_ref[...].astype(BF16), preferred_element_type=F32) + b_ref[...]


def _adaln(c_all, ada_w, ada_b, tn=1536):
    depth, d, n = ada_w.shape
    r = c_all.shape[0]
    return pl.pallas_call(
        _adaln_kernel,
        grid=(depth, n // tn),
        in_specs=[pl.BlockSpec((r, d), lambda l, j: (0, 0)),
                  pl.BlockSpec((None, d, tn), lambda l, j: (l, 0, j)),
                  pl.BlockSpec((None, 1, tn), lambda l, j: (l, 0, j))],
        out_specs=pl.BlockSpec((None, r, tn), lambda l, j: (l, 0, j)),
        out_shape=jax.ShapeDtypeStruct((depth, r, n), F32),
        compiler_params=_cparams(("arbitrary", "arbitrary")),
        name="adaln",
    )(c_all, ada_w, ada_b.reshape(depth, 1, n))


def _even_proj_kernel(x_ref, g_ref, sc_ref, sh_ref, wqkv_ref, wabf_ref, bf_ref, *refs,
                      tm, tq, tiles_per_seq, with_cum, n_carried):
    refs = refs[n_carried:]
    if with_cum:
        q_ref, k_ref, v_ref, lf_ref, u_ref, kb_ref, vt_ref, cum_ref, cumt_ref, carry_ref = refs
    else:
        q_ref, k_ref, v_ref, lf_ref, u_ref = refs
    hm = _norm_mod(x_ref[...], g_ref[...], sc_ref[...], sh_ref[...]).astype(BF16)
    c = FOX_DIM
    qkv = jnp.dot(hm, wqkv_ref[...], preferred_element_type=F32)
    q_ref[...] = qkv[:, 0:c].astype(BF16)
    k = qkv[:, c:2 * c]
    v = qkv[:, 2 * c:3 * c]
    abf = jnp.dot(hm, wabf_ref[...], preferred_element_type=F32)
    u_ref[...] = abf[:, 0:c] * jax.nn.sigmoid(abf[:, c:2 * c])
    logf = _log_sigmoid(abf[:, 2 * c:2 * c + LANES] + bf_ref[...])
    if not with_cum:
        k_ref[...] = k
        v_ref[...] = v
        lf_ref[...] = logf[:, 0:FOX_HEADS]
    else:
        v_t = v.T
        k_ref[...] = k.T
        v_ref[...] = v_t
        lf_ref[...] = logf.T[0:FOX_HEADS, :]
        kb_ref[...] = (k * LOG2E).astype(BF16)
        for r in range(tm // tq):
            vt_ref[r] = v_t[:, r * tq:(r + 1) * tq].astype(BF16)
        i = pl.program_id(0)

        @pl.when(i % tiles_per_seq == 0)
        def _():
            carry_ref[...] = jnp.zeros_like(carry_ref)

        row = lax.broadcasted_iota(jnp.int32, (tm, tm), 0)
        col = lax.broadcasted_iota(jnp.int32, (tm, tm), 1)
        tri = (col <= row).astype(BF16)
        cum = _dot3_left(tri, logf) + carry_ref[...]
        carry_ref[...] = cum[tm - 1:tm, :]
        cum = cum * LOG2E
        cum_ref[...] = cum[:, 0:FOX_HEADS]
        cum_t = cum.T
        for r in range(tm // tq):
            cumt_ref[r] = cum_t[0:FOX_HEADS, r * tq:(r + 1) * tq]


def _dot3_left(w_bf16, x):
    hi, mid, lo = _split3(x)
    d = functools.partial(jnp.dot, preferred_element_type=F32)
    return d(w_bf16, hi) + d(w_bf16, mid) + d(w_bf16, lo)


def _even_proj(x, g, sc, sh, weights, bf_pad, *, tm, tq, seq_len, with_cum, layer=0, n_layers=1, carried=()):
    m, d = x.shape
    nmod, rows, _ = sc.shape
    tiles_per_mod = (m // nmod) // tm
    c = FOX_DIM
    row_spec = lambda width: pl.BlockSpec((tm, width), lambda i: (i, 0))
    tiles_per_seq = max(seq_len // tm, 1)
    if with_cum:
        nseq = m // seq_len
        stacked = lambda width: (
            jax.ShapeDtypeStruct((n_layers, nseq, width, seq_len), F32),
            pl.BlockSpec((None, None, width, tm), lambda i: (layer, i // tiles_per_seq, 0, i % tiles_per_seq)))
        kvl = [stacked(c), stacked(c), stacked(FOX_HEADS)]
    else:
        kvl = [(jax.ShapeDtypeStruct((m, w_), F32), row_spec(w_)) for w_ in (c, c, FOX_HEADS)]
    out_shape = [jax.ShapeDtypeStruct((m, c), BF16)] + [s for s, _ in kvl] + [jax.ShapeDtypeStruct((m, c), F32)]
    out_specs = [row_spec(c)] + [s for _, s in kvl] + [row_spec(c)]
    scratch = []
    if with_cum:
        out_shape += [jax.ShapeDtypeStruct((m, c), BF16), jax.ShapeDtypeStruct((m // tq, c, tq), BF16),
                      jax.ShapeDtypeStruct((m, FOX_HEADS), F32), jax.ShapeDtypeStruct((m // tq, FOX_HEADS, tq), F32)]
        out_specs += [row_spec(c), pl.BlockSpec((tm // tq, c, tq), lambda i: (i, 0, 0)),
                      row_spec(FOX_HEADS), pl.BlockSpec((tm // tq, FOX_HEADS, tq), lambda i: (i, 0, 0))]
        scratch = [pltpu.VMEM((1, LANES), F32)]
    n_fixed = 5 + len(weights)
    return pl.pallas_call(
        functools.partial(_even_proj_kernel, tm=tm, tq=tq, tiles_per_seq=tiles_per_seq, with_cum=with_cum,
                          n_carried=len(carried)),
        grid=(m // tm,),
        in_specs=[row_spec(d), pl.BlockSpec((1, d), lambda i: (0, 0)),
                  _mod_spec(rows, d, tiles_per_mod), _mod_spec(rows, d, tiles_per_mod),
                  *[pl.BlockSpec(w.shape, lambda i: (0, 0)) for w in weights],
                  pl.BlockSpec((1, LANES), lambda i: (0, 0)),
                  *[pl.BlockSpec(memory_space=pl.ANY) for _ in carried]],
        out_specs=out_specs,
        out_shape=out_shape,
        scratch_shapes=scratch,
        input_output_aliases={n_fixed + t: 1 + t for t in range(len(carried))},
        compiler_params=_cparams(("arbitrary",)),
        name="even_proj",
    )(x, g, sc, sh, *weights, bf_pad, *carried)


def _fox_prompt_kernel(q_ref, k_ref, vt_ref, ck_ref, cq_ref, o_ref, qm_ref, m_ref, acc_ref, *, tq):
    i = pl.program_id(1)
    hd = FOX_HEAD_DIM
    low = lax.broadcasted_iota(jnp.int32, (1, LANES), 1) < hd
    for pair in range(FOX_HEADS // 2):
        qp = q_ref[:, pair * LANES:(pair + 1) * LANES].astype(F32)
        qm_ref[2 * pair] = jnp.where(low, qp, 0.0).astype(BF16)
        qm_ref[2 * pair + 1] = jnp.where(low, 0.0, qp).astype(BF16)
    m_ref[...] = jnp.full_like(m_ref, NEG_BIG)
    acc_ref[...] = jnp.zeros_like(acc_ref)
    key = lax.broadcasted_iota(jnp.int32, (tq, tq), 0)
    qry = lax.broadcasted_iota(jnp.int32, (tq, tq), 1)
    causal = key <= qry
    ones = jnp.ones((acc_ref.shape[1] - hd, tq), BF16)

    def scores(j, masked):
        start = pl.multiple_of(j * tq, tq)
        out = []
        for h in range(FOX_HEADS):
            pair = h // 2
            kj = k_ref[pl.ds(start, tq), pair * LANES:(pair + 1) * LANES]
            t = _dot_nt(kj, qm_ref[h]) - ck_ref[pl.ds(start, tq), h:h + 1]
            if masked:
                t = jnp.where(causal, t, NEG_BIG)
            out.append((t, jnp.max(t, axis=0, keepdims=True)))
        return out

    def softmax_pv(j, scored):
        cq_all = cq_ref[i]
        for h in range(FOX_HEADS):
            t, t_max = scored[h]
            cq = cq_all[h:h + 1, :]
            m_prev = m_ref[h:h + 1, :]
            m_new = jnp.maximum(m_prev, t_max + cq)
            p = jnp.exp2(t + (cq - m_new)).astype(BF16)
            alpha = jnp.exp2(m_prev - m_new)
            v_ones = jnp.concatenate([vt_ref[j][h * hd:(h + 1) * hd, :], ones], axis=0)
            acc_ref[h] = alpha * acc_ref[h] + jnp.dot(v_ones, p, preferred_element_type=F32)
            m_ref[h:h + 1, :] = m_new

    def steps(blocks):
        scored = [scores(j, masked) for j, masked in blocks]
        for (j, _), sc in zip(blocks, scored):
            softmax_pv(j, sc)

    def body(jj, carry):
        steps([(2 * jj, False), (2 * jj + 1, False)])
        return carry

    lax.fori_loop(0, i // 2, body, 0)

    @pl.when(i % 2 == 1)
    def _():
        steps([(i - 1, False), (i, True)])

    @pl.when(i % 2 == 0)
    def _():
        steps([(i, True)])
    for pair in range(FOX_HEADS // 2):
        halves = []
        for h in (2 * pair, 2 * pair + 1):
            halves.append(acc_ref[h, 0:hd, :] * (1.0 / acc_ref[h, hd:hd + 1, :]))
        o_ref[:, pair * LANES:(pair + 1) * LANES] = jnp.concatenate(halves, axis=0).T.astype(o_ref.dtype)


def _fox_prompt(q, kb, vt, cum, cumt, *, batch, seq_len, tq):
    m, c = q.shape
    nq = seq_len // tq
    return pl.pallas_call(
        functools.partial(_fox_prompt_kernel, tq=tq),
        grid=(batch, nq),
        in_specs=[pl.BlockSpec((tq, c), lambda b, i: (b * nq + i, 0)),
                  pl.BlockSpec((seq_len, c), lambda b, i: (b, 0)),
                  pl.BlockSpec((nq, c, tq), lambda b, i: (b, 0, 0)),
                  pl.BlockSpec((seq_len, FOX_HEADS), lambda b, i: (b, 0)),
                  pl.BlockSpec((nq, FOX_HEADS, tq), lambda b, i: (b, 0, 0))],
        out_specs=pl.BlockSpec((tq, c), lambda b, i: (b * nq + i, 0)),
        out_shape=jax.ShapeDtypeStruct((m, c), BF16),
        scratch_shapes=[pltpu.VMEM((FOX_HEADS, tq, LANES), BF16), pltpu.VMEM((FOX_HEADS, tq), F32),
                        pltpu.VMEM((FOX_HEADS, FOX_HEAD_DIM + 16, tq), F32)],
        compiler_params=_cparams(("arbitrary", "arbitrary")),
        name="fox_prompt",
    )(q, kb, vt, cum, cumt)


def _group_matrices(channels, groups):
    gsz = channels // groups
    ch = lax.broadcasted_iota(jnp.int32, (channels, LANES), 0)
    gr = lax.broadcasted_iota(jnp.int32, (channels, LANES), 1)
    gather = (ch // gsz == gr).astype(BF16)
    gr_t = lax.broadcasted_iota(jnp.int32, (LANES, channels), 0)
    ch_t = lax.broadcasted_iota(jnp.int32, (LANES, channels), 1)
    spread = (ch_t // gsz == gr_t).astype(BF16)
    return gather, spread, 1.0 / gsz


def _group_norm_rows(y, groups):
    gather, spread, inv = _group_matrices(y.shape[-1], groups)
    mu = _dot2(y, gather) * inv
    d = y - _dot2(mu, spread)
    var = _dot2(d * d, gather) * inv
    return d * _dot2(lax.rsqrt(var + EPS), spread)


def _conv_finish(acc, cb, gn_g, gn_b):
    cv = _group_norm_rows(acc + cb, CONV_GROUPS) * gn_g + gn_b
    return _silu(cv)


def _conv_prompt_kernel(u_ref, halo_ref, cw_ref, cb_ref, gg_ref, gb_ref, o_ref, xp_ref, sh_ref, y_ref, *, tl):
    i = pl.program_id(1)
    halo = halo_ref[...]
    xp_ref[0:CONV_HALO, :] = jnp.where(i == 0, jnp.zeros_like(halo), halo)
    xp_ref[CONV_HALO:CONV_HALO + tl, :] = u_ref[...]
    span = sh_ref.shape[1]
    for r in range(1, SUBLANES):
        sh_ref[r - 1] = xp_ref[r:r + span, :]
    base = CONV_HALO - (CONV_WIDTH - 1)
    rows = 64
    for r0 in range(0, tl, rows):
        for c0 in range(0, u_ref.shape[1], LANES):
            acc = jnp.zeros((rows, LANES), F32)
            for w in range(CONV_WIDTH):
                shift = (base + w) % SUBLANES
                start = r0 + base + w - shift
                if shift == 0:
                    tap = xp_ref[start:start + rows, c0:c0 + LANES]
                else:
                    tap = sh_ref[shift - 1, start:start + rows, c0:c0 + LANES]
                acc = acc + tap * cw_ref[w:w + 1, c0:c0 + LANES]
            y_ref[r0:r0 + rows, c0:c0 + LANES] = acc
    y = _conv_finish(y_ref[...], cb_ref[...], gg_ref[...], gb_ref[...])
    o_ref[...] = y.astype(o_ref.dtype)


def _conv_prompt(u, cw, cb, gn_g, gn_b, *, batch, seq_len, tl):
    m, c = u.shape
    nl = seq_len // tl
    hb = tl // CONV_HALO
    vec = pl.BlockSpec((1, c), lambda b, i: (0, 0))
    return pl.pallas_call(
        functools.partial(_conv_prompt_kernel, tl=tl),
        grid=(batch, nl),
        in_specs=[pl.BlockSpec((tl, c), lambda b, i: (b * nl + i, 0)),
                  pl.BlockSpec((CONV_HALO, c), lambda b, i: (jnp.maximum((b * nl + i) * hb - 1, 0), 0)),
                  pl.BlockSpec((CONV_HALO, c), lambda b, i: (0, 0)), vec, vec, vec],
        out_specs=pl.BlockSpec((tl, c), lambda b, i: (b * nl + i, 0)),
        out_shape=jax.ShapeDtypeStruct((m, c), BF16),
        scratch_shapes=[pltpu.VMEM((CONV_HALO + tl, c), F32),
                        pltpu.VMEM((SUBLANES - 1, CONV_HALO - SUBLANES + tl, c), F32), pltpu.VMEM((tl, c), F32)],
        compiler_params=_cparams(("arbitrary", "arbitrary")),
        name="conv_prompt",
    )(u, u, cw, cb, gn_g, gn_b)


def _conv_step_kernel(st_ref, u_ref, cw_ref, cb_ref, gg_ref, gb_ref, o_ref):
    acc = u_ref[...] * cw_ref[CONV_WIDTH - 1:CONV_WIDTH, :]
    for w in range(CONV_WIDTH - 1):
        acc = acc + st_ref[w] * cw_ref[w:w + 1, :]
    o_ref[...] = _conv_finish(acc, cb_ref[...], gg_ref[...], gb_ref[...]).astype(o_ref.dtype)


def _conv_step(state_t, u, cw, cb, gn_g, gn_b):
    bd, c = u.shape
    full = lambda shape: pl.BlockSpec(shape, lambda i: (0,) * len(shape))
    return pl.pallas_call(
        _conv_step_kernel,
        grid=(1,),
        in_specs=[full(state_t.shape), full((bd, c)), full(cw.shape), full((1, c)), full((1, c)), full((1, c))],
        out_specs=full((bd, c)),
        out_shape=jax.ShapeDtypeStruct((bd, c), BF16),
        compiler_params=_cparams(("arbitrary",)),
        name="conv_step",
    )(state_t, u, cw, cb, gn_g, gn_b)


def _mix_ffn_kernel(*refs, n_in, final):
    a_refs = refs[:n_in]
    w_refs = refs[n_in:2 * n_in]
    (x_ref, g1_ref, g_ref, sc_ref, sh_ref, g2_ref, wg_ref, wu_ref, wd_ref, fg_ref,
     o_ref, x1_ref, hf_ref, acc_ref) = refs[2 * n_in:]
    j = pl.program_id(1)

    @pl.when(j == 0)
    def _():
        mix = jnp.dot(a_refs[0][...], w_refs[0][...], preferred_element_type=F32)
        for a_ref, w_ref in zip(a_refs[1:], w_refs[1:]):
            mix = mix + jnp.dot(a_ref[...], w_ref[...], preferred_element_type=F32)
        x1 = x_ref[...] + g1_ref[...] * mix
        x1_ref[...] = x1
        hf_ref[...] = _norm_mod(x1, g_ref[...], sc_ref[...], sh_ref[...]).astype(BF16)
        acc_ref[...] = jnp.zeros_like(acc_ref)

    hf = hf_ref[...]
    gt = jnp.dot(hf, wg_ref[...], preferred_element_type=F32)
    up = jnp.dot(hf, wu_ref[...], preferred_element_type=F32)
    acc_ref[...] += jnp.dot((_silu(gt) * up).astype(BF16), wd_ref[...], preferred_element_type=F32)

    @pl.when(j == pl.num_programs(1) - 1)
    def _():
        y = x1_ref[...] + g2_ref[...] * acc_ref[...]
        if final:
            ms = jnp.mean(y * y, axis=-1, keepdims=True)
            y = y * lax.rsqrt(ms + EPS) * fg_ref[...]
        o_ref[...] = y


def _mix_ffn(acts, w_outs, x, g1, g, sc, sh, g2, wg, wu, wd, fg, *, tm, th, final):
    m, d = x.shape
    nmod, rows, _ = sc.shape
    hdim = wg.shape[1]
    tiles_per_mod = (m // nmod) // tm
    vec = pl.BlockSpec((1, d), lambda i, j: (0, 0))
    mod = _mod_spec(rows, d, tiles_per_mod)
    in_specs = ([pl.BlockSpec((tm, a.shape[1]), lambda i, j: (i, 0)) for a in acts]
                + [pl.BlockSpec(w.shape, lambda i, j: (0, 0)) for w in w_outs]
                + [pl.BlockSpec((tm, d), lambda i, j: (i, 0)), mod, vec, mod, mod, mod,
                   pl.BlockSpec((d, th), lambda i, j: (0, j)), pl.BlockSpec((d, th), lambda i, j: (0, j)),
                   pl.BlockSpec((th, d), lambda i, j: (j, 0)), vec])
    return pl.pallas_call(
        functools.partial(_mix_ffn_kernel, n_in=len(acts), final=final),
        grid=(m // tm, hdim // th),
        in_specs=in_specs,
        out_specs=pl.BlockSpec((tm, d), lambda i, j: (i, 0)),
        out_shape=jax.ShapeDtypeStruct((m, d), F32),
        scratch_shapes=[pltpu.VMEM((tm, d), F32), pltpu.VMEM((tm, d), BF16), pltpu.VMEM((tm, d), F32)],
        compiler_params=_cparams(("arbitrary", "arbitrary")),
        name="mix_ffn",
    )(*acts, *w_outs, x, g1, g, sc, sh, g2, wg, wu, wd, fg)


def _odd_proj_kernel(x_ref, g_ref, sc_ref, sh_ref, w_ref, cos_ref, sin_ref, o_ref, hm_ref):
    j = pl.program_id(1)

    @pl.when(j == 0)
    def _():
        hm_ref[...] = _norm_mod(x_ref[...], g_ref[...], sc_ref[...], sh_ref[...]).astype(BF16)

    def slab():
        return jnp.dot(hm_ref[...], w_ref[...], preferred_element_type=F32)

    @pl.when(j == 0)
    def _():
        z = slab()
        cos = cos_ref[...]
        sin = sin_ref[...]
        for c0 in range(0, z.shape[1], 2 * LANES):
            x1 = z[:, c0:c0 + LANES]
            x2 = z[:, c0 + LANES:c0 + 2 * LANES]
            o_ref[:, c0:c0 + LANES] = (x1 * cos - x2 * sin).astype(o_ref.dtype)
            o_ref[:, c0 + LANES:c0 + 2 * LANES] = (x1 * sin + x2 * cos).astype(o_ref.dtype)

    @pl.when(j == 1)
    def _():
        o_ref[...] = slab().astype(o_ref.dtype)

    @pl.when(j == 2)
    def _():
        o_ref[...] = _silu(slab()).astype(o_ref.dtype)


def _odd_proj(x, g, sc, sh, w, cos, sin, *, tm, tn):
    m, d = x.shape
    nmod, rows, _ = sc.shape
    n = w.shape[1]
    tiles_per_mod = (m // nmod) // tm
    assert n == 3 * tn, "q|k, v and gate slabs must have equal widths"
    pos_tiles = cos.shape[0] // tm
    mod = _mod_spec(rows, d, tiles_per_mod)
    return pl.pallas_call(
        _odd_proj_kernel,
        grid=(m // tm, 3),
        in_specs=[pl.BlockSpec((tm, d), lambda i, j: (i, 0)), pl.BlockSpec((1, d), lambda i, j: (0, 0)), mod, mod,
                  pl.BlockSpec((d, tn), lambda i, j: (0, j)),
                  pl.BlockSpec((tm, LANES), lambda i, j: (i % pos_tiles, 0)),
                  pl.BlockSpec((tm, LANES), lambda i, j: (i % pos_tiles, 0))],
        out_specs=pl.BlockSpec((tm, tn), lambda i, j: (i, j)),
        out_shape=jax.ShapeDtypeStruct((m, n), BF16),
        scratch_shapes=[pltpu.VMEM((tm, d), BF16)],
        compiler_params=_cparams(("arbitrary", "arbitrary")),
        name="odd_proj",
    )(x, g, sc, sh, w, cos, sin)


def _ret_log_gamma(h):
    return float(np.log(np.float64(1.0) - np.float64(2.0) ** (-5.0 - h)))


def _ret_finish(o, sg, gain):
    mu = jnp.mean(o, axis=-1, keepdims=True)
    d = o - mu
    var = jnp.mean(d * d, axis=-1, keepdims=True)
    r = d * lax.rsqrt(var + EPS) * gain
    return sg.astype(F32) * r


def _ret_prompt_kernel(q_ref, k_ref, v_ref, sg_ref, gain_ref, y_ref, s_out_ref, s_ref, dmask_ref, *, chunk, dk, dv):
    c = pl.program_id(1)

    @pl.when(c == 0)
    def _():
        s_ref[...] = jnp.zeros_like(s_ref)

    @pl.when(jnp.logical_and(pl.program_id(0) == 0, c == 0))
    def _():
        ri = lax.broadcasted_iota(jnp.int32, (chunk, chunk), 0)
        ci = lax.broadcasted_iota(jnp.int32, (chunk, chunk), 1)
        diff = (ri - ci).astype(F32)
        for h in range(RET_HEADS):
            dmask_ref[h] = jnp.where(ri >= ci, jnp.exp(jnp.maximum(diff, 0.0) * _ret_log_gamma(h)), 0.0)

    pos = lax.broadcasted_iota(jnp.int32, (chunk, 1), 0).astype(F32)
    for h in range(RET_HEADS):
        lg = _ret_log_gamma(h)
        dmask = dmask_ref[h]
        q_dec = jnp.exp((pos + 1.0) * lg)
        k_dec = jnp.exp((chunk - 1.0 - pos) * lg)
        c_dec = math.exp(chunk * lg)
        q = q_ref[:, h * dk:(h + 1) * dk]
        k = k_ref[:, h * dk:(h + 1) * dk]
        v = v_ref[:, h * dv:(h + 1) * dv]
        s_prev = s_ref[h]
        a = _dot_nt(q, k) * dmask
        o = (jnp.dot(a.astype(BF16), v, preferred_element_type=F32)
             + jnp.dot(q, s_prev.astype(BF16), preferred_element_type=F32) * q_dec)
        kd = (k.astype(F32) * k_dec).T.astype(BF16)
        s_ref[h] = s_prev * c_dec + jnp.dot(kd, v, preferred_element_type=F32)
        cols = slice(h * dv, (h + 1) * dv)
        y_ref[:, cols] = _ret_finish(o, sg_ref[:, cols], gain_ref[:, cols]).astype(y_ref.dtype)

    @pl.when(c == pl.num_programs(1) - 1)
    def _():
        s_out_ref[...] = s_ref[...]


def _ret_prompt(z, gain, *, batch, seq_len, chunk, dk, dv):
    m = z.shape[0]
    nc = seq_len // chunk
    nk = RET_HEADS * dk
    nv = RET_HEADS * dv
    rowblk = lambda b, c: b * nc + c
    return pl.pallas_call(
        functools.partial(_ret_prompt_kernel, chunk=chunk, dk=dk, dv=dv),
        grid=(batch, nc),
        in_specs=[pl.BlockSpec((chunk, nk), lambda b, c: (rowblk(b, c), 0)),
                  pl.BlockSpec((chunk, nk), lambda b, c: (rowblk(b, c), 1)),
                  pl.BlockSpec((chunk, nv), lambda b, c: (rowblk(b, c), (2 * nk) // nv)),
                  pl.BlockSpec((chunk, nv), lambda b, c: (rowblk(b, c), (2 * nk) // nv + 1)),
                  pl.BlockSpec((1, nv), lambda b, c: (0, 0))],
        out_specs=[pl.BlockSpec((chunk, nv), lambda b, c: (rowblk(b, c), 0)),
                   pl.BlockSpec((None, RET_HEADS, dk, dv), lambda b, c: (b, 0, 0, 0))],
        out_shape=[jax.ShapeDtypeStruct((m, nv), BF16), jax.ShapeDtypeStruct((batch, RET_HEADS, dk, dv), F32)],
        scratch_shapes=[pltpu.VMEM((RET_HEADS, dk, dv), F32), pltpu.VMEM((RET_HEADS, chunk, chunk), F32)],
        compiler_params=_cparams(("arbitrary", "arbitrary")),
        name="ret_prompt",
    )(z, z, z, z, gain)


def _ret_step_kernel(z_ref, s_ref, gain_ref, *refs, dk, dv):
    y_ref, s_out_ref = refs[-2:]
    nk = RET_HEADS * dk
    nv = RET_HEADS * dv
    row0 = lax.broadcasted_iota(jnp.int32, (LANES, 1), 0) == 0
    for h in range(RET_HEADS):
        gamma = math.exp(_ret_log_gamma(h))
        q = z_ref[:, h * dk:(h + 1) * dk]
        k = z_ref[:, nk + h * dk:nk + (h + 1) * dk]
        v = z_ref[:, 2 * nk + h * dv:2 * nk + (h + 1) * dv]
        sg = z_ref[:, 2 * nk + nv + h * dv:2 * nk + nv + (h + 1) * dv]
        s_prev = s_ref[h]
        q_rows = jnp.broadcast_to(q.astype(F32), (16, dk)).astype(BF16)
        a = jnp.sum(q.astype(F32) * k.astype(F32), axis=-1, keepdims=True)
        o = a * v.astype(F32) + jnp.dot(q_rows, s_prev.astype(BF16), preferred_element_type=F32)[0:1, :] * gamma
        k_rows = jnp.where(row0, jnp.broadcast_to(k.astype(F32), (LANES, dk)), 0.0)
        v_rows = jnp.where(row0, jnp.broadcast_to(v.astype(F32), (LANES, dv)), 0.0)
        s_out_ref[h] = s_prev * gamma + jnp.dot(k_rows.T.astype(BF16), v_rows.astype(BF16),
                                                preferred_element_type=F32)
        cols = slice(h * dv, (h + 1) * dv)
        y_ref[:, cols] = _ret_finish(o, sg, gain_ref[:, cols]).astype(y_ref.dtype)


def _ret_step(z, states, layer, gain, carried, *, dk, dv):
    bd, n = z.shape
    nv = RET_HEADS * dv
    slab = pl.BlockSpec((None, None, RET_HEADS, dk, dv), lambda b: (layer, b, 0, 0, 0))
    operands = [z.reshape(bd, 1, n), states, gain]
    in_specs = [pl.BlockSpec((None, 1, n), lambda b: (b, 0, 0)), slab, pl.BlockSpec((1, nv), lambda b: (0, 0))]
    aliases = {}
    if carried is not None:
        operands.append(carried)
        in_specs.append(pl.BlockSpec(memory_space=pl.ANY))
        aliases = {3: 1}
    y, s_all = pl.pallas_call(
        functools.partial(_ret_step_kernel, dk=dk, dv=dv),
        grid=(bd,),
        in_specs=in_specs,
        out_specs=[pl.BlockSpec((None, 1, nv), lambda b: (b, 0, 0)), slab],
        out_shape=[jax.ShapeDtypeStruct((bd, 1, nv), BF16), jax.ShapeDtypeStruct(states.shape, F32)],
        input_output_aliases=aliases,
        compiler_params=_cparams(("arbitrary",)),
        name="ret_step",
    )(*operands)
    return y.reshape(bd, nv), s_all


def _fox_decode_kernel(pt_ref, q_ref, kn_ref, vn_ref, lfn_ref, *refs, group, n_pages):
    k_refs = refs[0:group]
    v_refs = refs[group:2 * group]
    lpt_ref = refs[2 * group]
    o_ref, m_ref, l_ref, w_ref, r_ref, acc_ref = refs[2 * group + 1:]
    c = pl.program_id(1)
    first_page = pl.program_id(0) * n_pages + (n_pages - 1) - c * group
    head = lax.broadcasted_iota(jnp.int32, (FOX_HEADS, FOX_DIM), 0)
    chan = lax.broadcasted_iota(jnp.int32, (FOX_HEADS, FOX_DIM), 1)
    own = chan // FOX_HEAD_DIM == head

    def per_head_to_channels(col):
        return jnp.sum(jnp.where(own, jnp.broadcast_to(col, own.shape), 0.0), axis=0, keepdims=True)

    @pl.when(c == 0)
    def _():
        prod = jnp.broadcast_to(q_ref[...] * kn_ref[...], own.shape)
        m_ref[...] = jnp.sum(jnp.where(own, prod, 0.0), axis=-1, keepdims=True)
        l_ref[...] = jnp.ones_like(l_ref)
        w_ref[...] = jnp.ones_like(w_ref)
        r_ref[...] = jnp.zeros_like(r_ref)
        acc_ref[...] = jnp.zeros_like(acc_ref)

    lane = lax.broadcasted_iota(jnp.int32, (1, PAGE_SIZE), 1)
    q_heads = jnp.where(own, jnp.broadcast_to(q_ref[...], own.shape), 0.0)
    q_rows = jnp.concatenate([q_heads, jnp.zeros_like(q_heads)], axis=0).astype(BF16)
    lfn = lfn_ref[...]
    later = r_ref[...]
    scores = []
    for g in range(group):
        lp = lpt_ref[pt_ref[first_page - g]]
        suffix = lp
        stride = 1
        while stride < PAGE_SIZE:
            ahead = pltpu.roll(suffix, PAGE_SIZE - stride, axis=1)
            suffix = suffix + jnp.where(lane < PAGE_SIZE - stride, ahead, 0.0)
            stride *= 2
        kt = k_refs[g][...].reshape(FOX_DIM, PAGE_SIZE).astype(BF16)
        qk = jnp.dot(q_rows, kt, preferred_element_type=F32)[0:FOX_HEADS, :]
        scores.append(qk + (lfn + later + (suffix - lp)))
        later = later + jnp.sum(lp, axis=-1, keepdims=True)
    r_ref[...] = later
    m_prev = m_ref[...]
    m_new = m_prev
    for s in scores:
        m_new = jnp.maximum(m_new, jnp.max(s, axis=-1, keepdims=True))
    alpha = jnp.exp(m_prev - m_new)
    l_new = alpha * l_ref[...]
    probs = []
    for s in scores:
        p = jnp.exp(s - m_new)
        l_new = l_new + jnp.sum(p, axis=-1, keepdims=True)
        probs.append(p)
    for h in range(FOX_HEADS):
        acc = acc_ref[h] * alpha[h:h + 1, :]
        for g in range(group):
            acc = acc + v_refs[g][h] * probs[g][h:h + 1, :]
        acc_ref[h] = acc
    m_ref[...] = m_new
    l_ref[...] = l_new
    w_ref[...] = alpha * w_ref[...]

    @pl.when(c == pl.num_programs(1) - 1)
    def _():
        acc_t = acc_ref[...].reshape(FOX_DIM, PAGE_SIZE).T
        pv = jnp.sum(acc_t, axis=0, keepdims=True)
        o = (pv + per_head_to_channels(w_ref[...]) * vn_ref[...]) / per_head_to_channels(l_ref[...])
        o_ref[...] = o.astype(o_ref.dtype)


def _fox_decode(q, k_new, v_new, logf_new, cache_kt, cache_vt, cache_lpt, page_table, layer, *, group):
    bd, c = q.shape
    n_pages = page_table.shape[1]
    group = min(group, n_pages)
    assert n_pages % group == 0
    pt_flat = page_table.reshape(-1)
    q = q.astype(F32)

    def page_spec(g, shape):
        def index(b, s, pt):
            return (layer, pt[b * n_pages + (n_pages - 1 - (s * group + g))]) + (0,) * len(shape)
        return pl.BlockSpec((None, None) + shape, index)

    row = pl.BlockSpec((None, 1, c), lambda b, s, pt: (b, 0, 0))
    slab = (FOX_HEADS, FOX_HEAD_DIM, PAGE_SIZE)
    in_specs = ([row, row, row, pl.BlockSpec((None, FOX_HEADS, 1), lambda b, s, pt: (b, 0, 0))]
                + [page_spec(g, slab) for g in range(group)]
                + [page_spec(g, slab) for g in range(group)]
                + [pl.BlockSpec((None,) + cache_lpt.shape[1:], lambda b, s, pt: (layer, 0, 0, 0))])
    col = pltpu.VMEM((FOX_HEADS, 1), F32)
    out = pl.pallas_call(
        functools.partial(_fox_decode_kernel, group=group, n_pages=n_pages),
        grid_spec=pltpu.PrefetchScalarGridSpec(
            num_scalar_prefetch=1,
            grid=(bd, n_pages // group),
            in_specs=in_specs,
            out_specs=row,
            scratch_shapes=[col, col, col, col, pltpu.VMEM(slab, F32)]),
        out_shape=jax.ShapeDtypeStruct((bd, 1, c), BF16),
        compiler_params=_cparams(("arbitrary", "arbitrary")),
        name="fox_decode",
    )(pt_flat, q.reshape(bd, 1, c), k_new.reshape(bd, 1, c), v_new.reshape(bd, 1, c),
      logf_new.reshape(bd, FOX_HEADS, 1), *([cache_kt] * group), *([cache_vt] * group), cache_lpt)
    return out.reshape(bd, c)


def _rope_tables(pos, half):
    inv = 1.0 / (ROPE_BASE ** jnp.linspace(0.0, 1.0, half, dtype=F32))
    ang = pos.astype(F32)[:, None] * inv[None, :]
    return jnp.cos(ang), jnp.sin(ang)


def _prep_weights(w_in_even, b_forget, conv_w, w_out_even, w_in_odd, w_ffn_gate, w_ffn_up, w_ffn_down, dk):
    c = FOX_DIM
    n_even, d, _ = w_in_even.shape
    qkv_scale = jnp.where(jnp.arange(3 * c) < c, FOX_HEAD_DIM ** -0.5, 1.0).astype(F32)
    w_qkv = (w_in_even[:, :, :3 * c] * qkv_scale).astype(BF16)
    w_f = jnp.pad(w_in_even[:, :, 3 * c:3 * c + FOX_HEADS], ((0, 0), (0, 0), (0, LANES - FOX_HEADS)))
    w_abf = jnp.concatenate([w_in_even[:, :, 3 * c + FOX_HEADS:], w_f], axis=-1).astype(BF16)
    bf_pad = jnp.pad(b_forget, ((0, 0), (0, LANES - FOX_HEADS)))[:, None, :]
    cw_pad = jnp.pad(conv_w, ((0, 0), (0, CONV_HALO - CONV_WIDTH), (0, 0)))
    nk = RET_HEADS * dk
    col = jnp.arange(w_in_odd.shape[-1])
    odd_scale = jnp.where((col >= nk) & (col < 2 * nk), dk ** -0.5, 1.0).astype(F32)
    w_odd = (w_in_odd * odd_scale).astype(BF16)
    return dict(w_even=(w_qkv, w_abf), bf_pad=bf_pad, cw_pad=cw_pad,
                w_out_fox=w_out_even[:, :c].astype(BF16), w_out_conv=w_out_even[:, c:].astype(BF16),
                w_odd=w_odd, wg=w_ffn_gate.astype(BF16), wu=w_ffn_up.astype(BF16), wd=w_ffn_down.astype(BF16))


def _tiles(rows, seq_len):
    if seq_len == 1:
        return dict(tm=rows, tm_ffn=rows, th=1408, tm_odd=rows)
    return dict(tm=min(512, seq_len), tm_ffn=min(512, seq_len), th=1408, tm_odd=min(1024, seq_len))


def _trunk(x, mod, pos, P, W, *, seq_len, even_mixer, odd_mixer):
    depth = mod.shape[0]
    d = x.shape[1]
    t = _tiles(x.shape[0], seq_len)
    tm = t['tm']
    cos, sin = _rope_tables(pos, LANES)
    ks, vs, lfs, cbs, rs = [], [], [], [], []
    for l in range(depth):
        sh1, sc1, g1, sh2, sc2, g2 = [mod[l][..., s * d:(s + 1) * d] for s in range(6)]
        gm = W['norm_mix_g'][l][None, :]
        if l % 2 == 0:
            e = l // 2
            stacked = dict(layer=e, n_layers=P['bf_pad'].shape[0],
                           carried=(ks[-1], vs[-1], lfs[-1]) if ks else ()) if even_mixer.needs_cum else {}
            out = _even_proj(x, gm, sc1, sh1, [w[e] for w in P['w_even']], P['bf_pad'][e], tm=tm, tq=min(FOX_TQ, tm),
                             seq_len=seq_len, with_cum=even_mixer.needs_cum, **stacked)
            q, k, v, logf, u = out[:5]
            o_fox, cv, new_buf = even_mixer(e, q, k, v, logf, u, out[5:])
            ks.append(k); vs.append(v); lfs.append(logf); cbs.append(new_buf)
            acts, w_outs = [o_fox, cv], [P['w_out_fox'][e], P['w_out_conv'][e]]
        else:
            o = l // 2
            z = _odd_proj(x, gm, sc1, sh1, P['w_odd'][o], cos, sin, tm=t['tm_odd'], tn=P['w_odd'].shape[-1] // 3)
            y, s_new = odd_mixer(o, z)
            rs.append(s_new)
            acts, w_outs = [y], [P['w_out_odd'][o]]
        x = _mix_ffn(acts, w_outs, x, g1, W['norm_ffn_g'][l][None, :], sc2, sh2, g2, P['wg'][l], P['wu'][l],
                     P['wd'][l], W['final_norm_g'][None, :], tm=t['tm_ffn'], th=t['th'], final=(l == depth - 1))
    return x, (ks, vs, lfs, cbs, rs)


class _PromptEven:
    needs_cum = True

    def __init__(self, batch, seq_len, W, P):
        self.batch, self.seq_len, self.W, self.P = batch, seq_len, W, P

    def __call__(self, e, q, k, v, logf, u, extra):
        kb, vt, cum, cumt = extra
        W, P = self.W, self.P
        o_fox = _fox_prompt(q, kb, vt, cum, cumt, batch=self.batch, seq_len=self.seq_len, tq=FOX_TQ)
        cv = _conv_prompt(u, P['cw_pad'][e], W['conv_b'][e][None, :], W['conv_norm_g'][e][None, :],
                          W['conv_norm_b'][e][None, :], batch=self.batch, seq_len=self.seq_len, tl=256)
        c = u.shape[1]
        new_buf = u.reshape(self.batch, self.seq_len, c)[:, self.seq_len - (CONV_WIDTH - 1):]
        return o_fox, cv, new_buf


class _SampleEven:
    needs_cum = False

    def __init__(self, W, P, cache_k, cache_v, cache_lp, state_conv, page_table):
        self.W, self.P = W, P
        self.cache_k, self.cache_v, self.cache_lp = cache_k, cache_v, cache_lp
        self.state_conv, self.page_table = state_conv, page_table

    def __call__(self, e, q, k, v, logf, u, extra):
        W, P = self.W, self.P
        o_fox = _fox_decode(q, k, v, logf, self.cache_k, self.cache_v, self.cache_lp, self.page_table, e, group=16)
        st = self.state_conv[e]
        cv = _conv_step(jnp.transpose(st, (1, 0, 2)), u, P['cw_pad'][e], W['conv_b'][e][None, :],
                        W['conv_norm_g'][e][None, :], W['conv_norm_b'][e][None, :])
        new_buf = jnp.concatenate([st[:, 1:], u[:, None, :]], axis=1)
        return o_fox, cv, new_buf


def kernel(x_prompt, x_sample, cache_k, cache_v, cache_logf, state_conv, state_ret, page_table, c_prompt, c_sample,
           ada_w, ada_b, norm_mix_g, norm_ffn_g, w_in_even, b_forget, conv_w, conv_b, conv_norm_g, conv_norm_b,
           w_out_even, w_in_odd, ret_norm_g, w_out_odd, w_ffn_gate, w_ffn_up, w_ffn_down, final_norm_g):
    bp, lp, d = x_prompt.shape
    bs, ls, _ = x_sample.shape
    assert ls == 1, "the decode path handles one new token per sequence"
    depth = ada_w.shape[0]
    n_even, n_phys, page, heads, hd = cache_k.shape
    assert (page, heads, hd) == (PAGE_SIZE, FOX_HEADS, FOX_HEAD_DIM)
    dk, dv = state_ret.shape[-2:]
    past_len = page_table.shape[1] * PAGE_SIZE

    W = dict(norm_mix_g=norm_mix_g, norm_ffn_g=norm_ffn_g, conv_b=conv_b, conv_norm_g=conv_norm_g,
             conv_norm_b=conv_norm_b, final_norm_g=final_norm_g)
    P = _prep_weights(w_in_even, b_forget, conv_w, w_out_even, w_in_odd, w_ffn_gate, w_ffn_up, w_ffn_down, dk)
    P['w_out_odd'] = w_out_odd.astype(BF16)
    gain = ret_norm_g[:, None, :]

    mod = _adaln(jnp.concatenate([c_prompt, c_sample], axis=0), ada_w, ada_b)
    mod_p = mod[:, :bp].reshape(depth, bp, 1, 6 * d)
    mod_s = mod[:, bp:].reshape(depth, 1, bs, 6 * d)

    def odd_prompt(o, z):
        return _ret_prompt(z, gain[o], batch=bp, seq_len=lp, chunk=min(256, lp), dk=dk, dv=dv)

    y_p, (k_p, v_p, lf_p, cb_p, r_p) = _trunk(
        x_prompt.reshape(bp * lp, d), mod_p, jnp.arange(lp), P, W, seq_len=lp,
        even_mixer=_PromptEven(bp, lp, W, P), odd_mixer=odd_prompt)

    ckt = jnp.transpose(cache_k, (0, 1, 3, 4, 2))
    cvt = jnp.transpose(cache_v, (0, 1, 3, 4, 2))
    clp = jnp.transpose(cache_logf, (0, 1, 3, 2))

    new_states = []

    def odd_sample(o, z):
        y, s_all = _ret_step(z, state_ret, o, gain[o], new_states[-1] if new_states else None, dk=dk, dv=dv)
        new_states.append(s_all)
        return y, s_all

    y_s, (k_s, v_s, lf_s, cb_s, r_s) = _trunk(
        x_sample.reshape(bs, d), mod_s, jnp.full((bs,), past_len, jnp.int32), P, W, seq_len=1,
        even_mixer=_SampleEven(W, P, ckt, cvt, clp, state_conv, page_table), odd_mixer=odd_sample)

    hshape = (FOX_HEADS, FOX_HEAD_DIM)

    def seq_last_to_heads(t):
        return jnp.transpose(t.reshape((n_even, bp) + hshape + (lp,)), (0, 1, 4, 2, 3))

    return (y_p.reshape(bp, lp, d), y_s.reshape(bs, ls, d),
            seq_last_to_heads(k_p[-1]), seq_last_to_heads(v_p[-1]),
            jnp.transpose(lf_p[-1], (0, 1, 3, 2)), jnp.stack(cb_p), jnp.stack(r_p),
            jnp.stack(k_s).reshape((n_even, bs, ls) + hshape), jnp.stack(v_s).reshape((n_even, bs, ls) + hshape),
            jnp.stack(lf_s).reshape(n_even, bs, ls, FOX_HEADS), jnp.stack(cb_s),
            r_s[-1].astype(state_ret.dtype))
```
